```python
import math
import numpy as np
import jax
import jax.numpy as jnp
from jax import lax

D_MODEL = 2048
BATCH = 16
SEQ = 2048
DEPTH = 2

GRID_W = 64
CTX_LEN = 256
N_BRANCH = 4
BR_W = D_MODEL // N_BRANCH

NA_HEADS = 8
NA_DH = BR_W // NA_HEADS
WIN_R = 8
WIN_C = 16
QBLK_C = 16
KBLK_C = WIN_C + QBLK_C
N_CBLK = GRID_W // QBLK_C

GLA_HEADS = 4
GLA_DV = BR_W // GLA_HEADS
GLA_DK = GLA_DV // 2
GLA_RANK = 16
GLA_TAU = 16.0
CHUNK = 64

GDN_HEADS = 4
GDN_DH = BR_W // GDN_HEADS
GDN_CONV = 5

HY_CONV = 3
HY_EMB = 33
HY_FFN = 64
HY_INNER = 2
HY_TARGET = 1e-2
HY_FAST = 0.3
HY_SLOW = 1.5

ROPE_THETA = 10000.0
EPS = 1e-6
F32 = jnp.float32

SPLITS = (
    ("na_qkv", 3 * BR_W), ("na_z", BR_W),
    ("gla_q", GLA_HEADS * GLA_DK), ("gla_k", GLA_HEADS * GLA_DK), ("gla_v", BR_W), ("gla_z", BR_W),
    ("gla_g", 2 * GLA_RANK),
    ("gdn_qkv", 3 * BR_W), ("gdn_z", BR_W), ("gdn_a", 2 * GDN_HEADS), ("gdn_b", 2 * GDN_HEADS),
    ("hy_xv", 3 * BR_W), ("hy_z", BR_W),
)
N_IN = sum(w for _, w in SPLITS)

kernel_name = "hybrid_na_gla_gdn_hyena_prefix_dit"


def rmsnorm(x, g):
    xf = x.astype(F32)
    y = xf * lax.rsqrt(jnp.mean(xf * xf, axis=-1, keepdims=True) + EPS)
    return (y * g.astype(F32)).astype(x.dtype)


def l2norm(x):
    return x * lax.rsqrt(jnp.sum(x * x, axis=-1, keepdims=True) + EPS)


def split_cols(p):
    out, o = {}, 0
    for name, w in SPLITS:
        out[name] = p[..., o:o + w]
        o += w
    return out


def to_heads(x, h):
    b, t, _ = x.shape
    return x.reshape(b, t, h, -1).transpose(0, 2, 1, 3)


def from_heads(x):
    b, h, t, d = x.shape
    return x.transpose(0, 2, 1, 3).reshape(b, t, h * d)


def dwconv(x, w):
    ch, k = w.shape
    return lax.conv_general_dilated(x, w.T[:, None, :].astype(x.dtype), window_strides=(1,),
                                    padding=[(k // 2, k // 2)], dimension_numbers=("NWC", "WIO", "NWC"),
                                    feature_group_count=ch)


def axial_rope(t_len):
    pos = jnp.arange(t_len)
    row = (pos // GRID_W).astype(F32)
    col = (pos % GRID_W).astype(F32)
    n = GLA_DK // 4
    freqs = ROPE_THETA ** (-jnp.arange(n, dtype=F32) / n)
    ang = jnp.concatenate([row[:, None] * freqs, col[:, None] * freqs], axis=-1)
    return jnp.cos(ang), jnp.sin(ang)


def apply_rope(x, cos, sin):
    x1, x2 = jnp.split(x, 2, axis=-1)
    return jnp.concatenate([x1 * cos - x2 * sin, x2 * cos + x1 * sin], axis=-1)


def to_chunks(x):
    b, h, t = x.shape[:3]
    return jnp.moveaxis(x.reshape(b, h, t // CHUNK, CHUNK, *x.shape[3:]), 2, 0)


def from_chunks(x):
    n, b, h, cl = x.shape[:4]
    return jnp.moveaxis(x, 0, 2).reshape(b, h, n * cl, *x.shape[4:])


def na_col_tables():
    qcol = np.arange(N_CBLK)[:, None] * QBLK_C + np.arange(QBLK_C)[None]
    kstart = np.clip(np.arange(N_CBLK) * QBLK_C - WIN_C // 2, 0, GRID_W - KBLK_C)
    kcol = kstart[:, None] + np.arange(KBLK_C)[None]
    cstart = np.clip(qcol - WIN_C // 2, 0, GRID_W - WIN_C)
    kc = kcol[:, None, :]
    cmask = (kc >= cstart[..., None]) & (kc < cstart[..., None] + WIN_C)
    dc_idx = np.clip(kc - qcol[..., None] + WIN_C - 1, 0, 2 * WIN_C - 2)
    return kcol, cmask, dc_idx


def neighbourhood_attention(q, k, v, k_ctx, v_ctx, rpb):
    b, h, t, dh = q.shape
    rows = t // GRID_W
    wr = min(WIN_R, rows)
    kcol, cmask, dc_idx = na_col_tables()
    scale = dh ** -0.5
    qg = q.reshape(b, h, rows, N_CBLK, QBLK_C, dh).transpose(2, 0, 1, 3, 4, 5)
    kg = k.reshape(b, h, rows, GRID_W, dh)
    vg = v.reshape(b, h, rows, GRID_W, dh)

    def row_block(args):
        r, q_r = args
        rs = jnp.clip(r - wr // 2, 0, rows - wr)
        k_r = lax.dynamic_slice_in_dim(kg, rs, wr, axis=2)[:, :, :, kcol]
        v_r = lax.dynamic_slice_in_dim(vg, rs, wr, axis=2)[:, :, :, kcol]
        dr_idx = rs + jnp.arange(wr) - r + WIN_R - 1
        bias = rpb[:, dr_idx[None, None, :, None], dc_idx[:, :, None, :]]
        s_win = jnp.einsum("bhcqd,bhrckd->bhcqrk", q_r, k_r) * scale + bias
        s_win = jnp.where(cmask[:, :, None, :], s_win, -jnp.inf).reshape(b, h, N_CBLK, QBLK_C, wr * KBLK_C)
        s_ctx = jnp.einsum("bhcqd,bhkd->bhcqk", q_r, k_ctx) * scale
        p = jax.nn.softmax(jnp.concatenate([s_win, s_ctx], axis=-1), axis=-1)
        p_win = p[..., :wr * KBLK_C].reshape(b, h, N_CBLK, QBLK_C, wr, KBLK_C)
        return (jnp.einsum("bhcqrk,bhrckd->bhcqd", p_win, v_r)
                + jnp.einsum("bhcqk,bhkd->bhcqd", p[..., wr * KBLK_C:], v_ctx))

    o = lax.map(row_block, (jnp.arange(rows), qg))
    return o.transpose(1, 2, 0, 3, 4, 5).reshape(b, h, t, dh)


def dense_attention(q, k, v):
    s = jnp.einsum("bhqd,bhkd->bhqk", q, k) * (q.shape[-1] ** -0.5)
    return jnp.einsum("bhqk,bhkd->bhqd", jax.nn.softmax(s, axis=-1), v)


def gla_scan(q, k, v, g, s0):
    qc, kc, vc = to_chunks(q), to_chunks(k), to_chunks(v)
    bc = jnp.cumsum(to_chunks(g), axis=3)
    incl = jnp.tril(jnp.ones((CHUNK, CHUNK), bool))

    def step(s, inp):
        q_, k_, v_, b_ = inp
        diff = jnp.where(incl[:, :, None], b_[..., :, None, :] - b_[..., None, :, :], -jnp.inf)
        att = jnp.einsum("bhtd,bhsd,bhtsd->bhts", q_, k_, jnp.exp(diff))
        b_end = b_[..., -1:, :]
        o = att @ v_ + (q_ * jnp.exp(b_)) @ s
        s = jnp.exp(b_end[..., 0, :])[..., None] * s + jnp.einsum("bhsd,bhse->bhde", k_ * jnp.exp(b_end - b_), v_)
        return s, o

    s_fin, o = lax.scan(step, s0, (qc, kc, vc, bc))
    return from_chunks(o), s_fin


def gla_inputs(p, wg2, bg, rope):
    q = to_heads(p["gla_q"], GLA_HEADS).astype(F32) * GLA_DK ** -0.5
    k = to_heads(p["gla_k"], GLA_HEADS).astype(F32)
    v = to_heads(p["gla_v"], GLA_HEADS).astype(F32)
    if rope is not None:
        q, k = apply_rope(q, *rope), apply_rope(k, *rope)
    lr = p["gla_g"].astype(F32)
    dirs = []
    for d in range(2):
        gl = jax.nn.log_sigmoid(lr[..., d * GLA_RANK:(d + 1) * GLA_RANK] @ wg2[d] + bg[d]) / GLA_TAU
        gl = to_heads(gl, GLA_HEADS)
        dirs.append((q, k, v, jnp.concatenate([gl, gl], axis=-1)))
    return dirs


def gdn_scan(q, k, v, beta, g, s0):
    qc, kc, vc = to_chunks(q), to_chunks(k), to_chunks(v)
    bt = to_chunks(beta)
    gam = jnp.cumsum(to_chunks(g), axis=-1)
    dv = v.shape[-1]
    strict = jnp.tril(jnp.ones((CHUNK, CHUNK), bool), -1)
    incl = jnp.tril(jnp.ones((CHUNK, CHUNK), bool))
    diff = gam[..., :, None] - gam[..., None, :]
    a_kk = jnp.einsum("nbhtd,nbhsd->nbhts", kc * bt[..., None], kc) * jnp.exp(jnp.where(strict, diff, -jnp.inf))
    rhs = jnp.concatenate([vc * bt[..., None], kc * (bt * jnp.exp(gam))[..., None]], axis=-1)
    sol = lax.linalg.triangular_solve(a_kk + jnp.eye(CHUNK, dtype=F32), rhs, left_side=True, lower=True,
                                      unit_diagonal=True)
    u, w = sol[..., :dv], sol[..., dv:]
    a_qk = jnp.einsum("nbhtd,nbhsd->nbhts", qc, kc) * jnp.exp(jnp.where(incl, diff, -jnp.inf))
    q_dec = qc * jnp.exp(gam)[..., None]
    k_dec = kc * jnp.exp(gam[..., -1:] - gam)[..., None]
    d_end = jnp.exp(gam[..., -1])

    def step(s, inp):
        u_, w_, aqk_, qd_, kd_, de_ = inp
        e = u_ - w_ @ s
        o = qd_ @ s + aqk_ @ e
        s = de_[..., None, None] * s + jnp.einsum("bhsd,bhse->bhde", kd_, e)
        return s, o

    s_fin, o = lax.scan(step, s0, (u, w, a_qk, q_dec, k_dec, d_end))
    return from_chunks(o), s_fin


def gdn_inputs(p, conv_w, a_log, dt_bias):
    qkv = jax.nn.silu(dwconv(p["gdn_qkv"], conv_w).astype(F32))
    q, k, v = [to_heads(a, GDN_HEADS) for a in jnp.split(qkv, 3, axis=-1)]
    q = l2norm(q) * GDN_DH ** -0.5
    k = l2norm(k)
    a = p["gdn_a"].astype(F32)
    bb = p["gdn_b"].astype(F32)
    dirs = []
    for d in range(2):
        sl = slice(d * GDN_HEADS, (d + 1) * GDN_HEADS)
        beta = jax.nn.sigmoid(bb[..., sl]).transpose(0, 2, 1)
        g = (-jnp.exp(a_log[d]) * jax.nn.softplus(a[..., sl] + dt_bias[d])).transpose(0, 2, 1)
        dirs.append((q, k, v, beta, g))
    return dirs


def bidir_scan(scan_fn, ctx_dirs, lat_dirs, s0):
    def flip(arrs):
        return [jnp.flip(a, axis=2) for a in arrs]
    oc_f, sc_f = scan_fn(*ctx_dirs[0], s0)
    ol_f, _ = scan_fn(*lat_dirs[0], sc_f)
    oc_b, sc_b = scan_fn(*flip(ctx_dirs[1]), s0)
    ol_b, _ = scan_fn(*flip(lat_dirs[1]), sc_b)
    return oc_f + jnp.flip(oc_b, axis=2), ol_f + jnp.flip(ol_b, axis=2)


def hyena_filters(length, w_in, b_in, w_mid, b_mid, freq, w_out):
    t = jnp.linspace(0.0, 1.0, length, dtype=F32)[:, None]
    bands = (HY_EMB - 1) // 2
    wpos = 2.0 * math.pi * jnp.arange(length, dtype=F32)[:, None] / length
    f = jnp.linspace(1e-4, bands - 1, bands, dtype=F32)[None]
    z = jnp.concatenate([t, jnp.cos(f * wpos), -jnp.sin(f * wpos)], axis=-1)
    h = jnp.sin(freq[0] * (z @ w_in + b_in))
    for i in range(HY_INNER):
        h = jnp.sin(freq[i + 1] * (h @ w_mid[i] + b_mid[i]))
    h = h @ w_out
    deltas = jnp.abs(jnp.linspace(math.log(HY_TARGET) / HY_SLOW, math.log(HY_TARGET) / HY_FAST, BR_W, dtype=F32))
    decay = jnp.exp(-t * deltas)
    return h[:, :BR_W] * decay, h[:, BR_W:] * decay


def two_sided_long_conv(u, h_f, h_b):
    length = u.shape[1]
    n = 2 * length
    kc = jnp.concatenate([h_f[:1] + h_b[:1], h_f[1:], jnp.zeros_like(h_f[:1]), h_b[:0:-1]], axis=0)
    y = jnp.fft.irfft(jnp.fft.rfft(u, n=n, axis=1) * jnp.fft.rfft(kc, n=n, axis=0)[None], n=n, axis=1)
    return y[:, :length]


def hyena_stream(p_xv, conv_w, conv_b, filt, skip):
    xv = (dwconv(p_xv, conv_w) + conv_b).astype(F32)
    x0, x1, v = jnp.split(xv, 3, axis=-1)
    h_f, h_b = hyena_filters(xv.shape[1], *filt)
    v = v * x1
    v = two_sided_long_conv(v, h_f, h_b) + skip * v
    return v * x0


def merge_branches(u, ys, w_gate, b_gate, w_branch, w_out):
    merged = None
    for i, y in enumerate(ys):
        term = jax.nn.sigmoid(u @ w_gate[i] + b_gate[i]) * (y @ w_branch[i])
        merged = term if merged is None else merged + term
    return merged @ w_out


def hybrid_mixer(u, uc, w_in, w_gate, b_gate, w_branch, w_out, na_rpb, gla_wg2, gla_bg, gla_norm,
                 gdn_conv, gdn_a_log, gdn_dt_bias, gdn_norm, hy_conv, hy_conv_b, hy_filt, hy_skip, with_ctx):
    dt = u.dtype
    b, t, _ = u.shape
    pl = split_cols(u @ w_in)
    pc = split_cols(uc @ w_in)

    qa, ka, va = [to_heads(a, NA_HEADS).astype(F32) for a in jnp.split(pl["na_qkv"], 3, axis=-1)]
    qac, kac, vac = [to_heads(a, NA_HEADS).astype(F32) for a in jnp.split(pc["na_qkv"], 3, axis=-1)]
    oa = neighbourhood_attention(qa, ka, va, kac, vac, na_rpb.astype(F32))

    wg2, bg = gla_wg2.astype(F32), gla_bg.astype(F32)
    s0_b = jnp.zeros((b, GLA_HEADS, GLA_DK, GLA_DV), F32)
    ob_c, ob = bidir_scan(gla_scan, gla_inputs(pc, wg2, bg, None), gla_inputs(pl, wg2, bg, axial_rope(t)), s0_b)

    a_log, dtb = gdn_a_log.astype(F32), gdn_dt_bias.astype(F32)
    s0_c = jnp.zeros((b, GDN_HEADS, GDN_DH, GDN_DH), F32)
    oc_c, oc = bidir_scan(gdn_scan, gdn_inputs(pc, gdn_conv, a_log, dtb), gdn_inputs(pl, gdn_conv, a_log, dtb), s0_c)

    filt = tuple(a.astype(F32) for a in hy_filt)
    skip = hy_skip.astype(F32)
    od = hyena_stream(pl["hy_xv"], hy_conv, hy_conv_b, filt, skip)

    def finish(p, oa_, ob_, oc_, od_):
        return [from_heads(oa_).astype(dt) * jax.nn.silu(p["na_z"]),
                from_heads(rmsnorm(ob_, gla_norm)).astype(dt) * jax.nn.silu(p["gla_z"]),
                from_heads(rmsnorm(oc_, gdn_norm)).astype(dt) * jax.nn.silu(p["gdn_z"]),
                od_.astype(dt) * jax.nn.silu(p["hy_z"])]

    y = merge_branches(u, finish(pl, oa, ob, oc, od), w_gate, b_gate, w_branch, w_out)
    if not with_ctx:
        return y, None
    oa_c = dense_attention(qac, kac, vac)
    od_c = hyena_stream(pc["hy_xv"], hy_conv, hy_conv_b, filt, skip)
    yc = merge_branches(uc, finish(pc, oa_c, ob_c, oc_c, od_c), w_gate, b_gate, w_branch, w_out)
    return y, yc


def setup_inputs(seed: int = 0) -> dict:
    key = jax.random.key(seed)
    ks = iter(jax.random.split(key, 40))

    def nrm(shape, s):
        return jax.random.normal(next(ks), shape, F32) * s

    d = D_MODEL
    dt = jnp.exp(jax.random.uniform(next(ks), (DEPTH, 2, GDN_HEADS), F32, math.log(1e-3), math.log(1e-1)))
    a_log = jnp.log(jax.random.uniform(next(ks), (DEPTH, 2, GDN_HEADS), F32, 1.0, 16.0))
    return {
        "x": nrm((BATCH, SEQ, d), 1.0),
        "c": nrm((BATCH, d), 1.0),
        "ctx": nrm((BATCH, CTX_LEN, d), 1.0),
        "c_ctx": nrm((d,), 1.0),
        "w_mod": nrm((DEPTH, d, 3 * d), 0.5 * d ** -0.5),
        "b_mod": nrm((DEPTH, 3 * d), 0.01),
        "g_pre": 1.0 + nrm((DEPTH, d), 0.01),
        "g_post": 1.0 + nrm((DEPTH, d), 0.01),
        "w_in": nrm((DEPTH, d, N_IN), d ** -0.5),
        "w_gate": nrm((DEPTH, N_BRANCH, d, d), d ** -0.5),
        "b_gate": nrm((DEPTH, N_BRANCH, d), 0.01),
        "w_branch": nrm((DEPTH, N_BRANCH, BR_W, d), BR_W ** -0.5),
        "w_out": nrm((DEPTH, d, d), d ** -0.5),
        "na_rpb": nrm((DEPTH, NA_HEADS, 2 * WIN_R - 1, 2 * WIN_C - 1), 0.1),
        "gla_wg2": nrm((DEPTH, 2, GLA_RANK, GLA_HEADS * GLA_DK // 2), GLA_RANK ** -0.5),
        "gla_bg": nrm((DEPTH, 2, GLA_HEADS * GLA_DK // 2), 0.01),
        "gla_norm": 1.0 + nrm((DEPTH, GLA_DV), 0.01),
        "gdn_conv": nrm((DEPTH, 3 * BR_W, GDN_CONV), GDN_CONV ** -0.5),
        "gdn_a_log": a_log,
        "gdn_dt_bias": jnp.log(jnp.expm1(dt)),
        "gdn_norm": 1.0 + nrm((DEPTH, GDN_DH), 0.01),
        "hy_conv": nrm((DEPTH, 3 * BR_W, HY_CONV), HY_CONV ** -0.5),
        "hy_conv_b": nrm((DEPTH, 3 * BR_W), 0.01),
        "hy_w_in": nrm((DEPTH, HY_EMB, HY_FFN), HY_EMB ** -0.5),
        "hy_b_in": nrm((DEPTH, HY_FFN), 0.01),
        "hy_w_mid": nrm((DEPTH, HY_INNER, HY_FFN, HY_FFN), HY_FFN ** -0.5),
        "hy_b_mid": nrm((DEPTH, HY_INNER, HY_FFN), 0.01),
        "hy_freq": 1.0 + nrm((DEPTH, HY_INNER + 1, HY_FFN), 0.01),
        "hy_w_out": nrm((DEPTH, HY_FFN, 2 * BR_W), 0.05 * HY_FFN ** -0.5),
        "hy_skip": nrm((DEPTH, BR_W), 1.0),
    }


def reference(x, c, ctx, c_ctx, w_mod, b_mod, g_pre, g_post, w_in, w_gate, b_gate, w_branch, w_out,
              na_rpb, gla_wg2, gla_bg, gla_norm, gdn_conv, gdn_a_log, gdn_dt_bias, gdn_norm,
              hy_conv, hy_conv_b, hy_w_in, hy_b_in, hy_w_mid, hy_b_mid, hy_freq, hy_w_out, hy_skip):
    h, hc = x, ctx
    for l in range(DEPTH):
        with_ctx = l < DEPTH - 1
        shift, scale, gate = jnp.split((jax.nn.silu(c) @ w_mod[l] + b_mod[l])[:, None, :], 3, axis=-1)
        shift_c, scale_c, gate_c = jnp.split(jax.nn.silu(c_ctx) @ w_mod[l] + b_mod[l], 3, axis=-1)
        u = rmsnorm(h, g_pre[l]) * (1 + scale) + shift
        uc = rmsnorm(hc, g_pre[l]) * (1 + scale_c) + shift_c
        y, yc = hybrid_mixer(u, uc, w_in[l], w_gate[l], b_gate[l], w_branch[l], w_out[l], na_rpb[l],
                             gla_wg2[l], gla_bg[l], gla_norm[l], gdn_conv[l], gdn_a_log[l], gdn_dt_bias[l],
                             gdn_norm[l], hy_conv[l], hy_conv_b[l],
                             (hy_w_in[l], hy_b_in[l], hy_w_mid[l], hy_b_mid[l], hy_freq[l], hy_w_out[l]),
                             hy_skip[l], with_ctx)
        h = h + gate * rmsnorm(y, g_post[l])
        if with_ctx:
            hc = hc + gate_c * rmsnorm(yc, g_post[l])
    return h
```

```python
import functools
import math

import numpy as np
import jax
import jax.numpy as jnp
from jax import lax
from jax.experimental import pallas as pl
from jax.experimental.pallas import tpu as pltpu

D_MODEL = 2048
BATCH = 16
SEQ = 2048
DEPTH = 2
GRID_W = 64
CTX_LEN = 256
N_BRANCH = 4
BR_W = D_MODEL // N_BRANCH
NA_HEADS = 8
NA_DH = BR_W // NA_HEADS
WIN_R = 8
WIN_C = 16
QBLK_C = 16
KBLK_C = WIN_C + QBLK_C
N_CBLK = GRID_W // QBLK_C
GLA_HEADS = 4
GLA_DV = BR_W // GLA_HEADS
GLA_DK = GLA_DV // 2
GLA_RANK = 16
GLA_TAU = 16.0
CHUNK = 64
GDN_HEADS = 4
GDN_DH = BR_W // GDN_HEADS
GDN_CONV = 5
HY_CONV = 3
HY_EMB = 33
HY_FFN = 64
HY_INNER = 2
HY_TARGET = 1e-2
HY_FAST = 0.3
HY_SLOW = 1.5
ROPE_THETA = 10000.0
EPS = 1e-6
F32 = jnp.float32
BF16 = jnp.bfloat16

REF_SPLITS = (
    ("na_qkv", 3 * BR_W), ("na_z", BR_W),
    ("gla_q", GLA_HEADS * GLA_DK), ("gla_k", GLA_HEADS * GLA_DK), ("gla_v", BR_W), ("gla_z", BR_W),
    ("gla_g", 2 * GLA_RANK),
    ("gdn_qkv", 3 * BR_W), ("gdn_z", BR_W), ("gdn_a", 2 * GDN_HEADS), ("gdn_b", 2 * GDN_HEADS),
    ("hy_xv", 3 * BR_W), ("hy_z", BR_W),
)
MAIN_ORDER = ("na_qkv", "na_z", "gla_q", "gla_k", "gla_v", "gla_z", "gdn_qkv", "gdn_z", "hy_xv", "hy_z")
SMALL_ORDER = ("gla_g", "gdn_a", "gdn_b")
LANE = 128
N_MAIN = sum(w for n, w in REF_SPLITS if n in MAIN_ORDER)
N_SMALL = LANE
VMEM_LIMIT = 48 * 1024 * 1024


def _ref_offsets():
    out, o = {}, 0
    for name, w in REF_SPLITS:
        out[name] = (o, w)
        o += w
    return out


def _my_offsets():
    out, o = {}, 0
    for name in MAIN_ORDER:
        w = dict(REF_SPLITS)[name]
        out[name] = (o, w)
        o += w
    o = 0
    for name in SMALL_ORDER:
        w = dict(REF_SPLITS)[name]
        out[name] = (o, w)
        o += w
    return out


REF_OFF = _ref_offsets()
MY_OFF = _my_offsets()


def _permute_w_in(w_in_l):
    main = jnp.concatenate([w_in_l[:, REF_OFF[n][0]:REF_OFF[n][0] + REF_OFF[n][1]] for n in MAIN_ORDER], axis=1)
    small = jnp.concatenate([w_in_l[:, REF_OFF[n][0]:REF_OFF[n][0] + REF_OFF[n][1]] for n in SMALL_ORDER], axis=1)
    small = jnp.pad(small, ((0, 0), (0, N_SMALL - small.shape[1])))
    return main.astype(BF16), small.astype(BF16)


def _mod_kernel(c_ref, w_ref, b_ref, o_ref):
    c = c_ref[...]
    a = (c * jax.nn.sigmoid(c)).astype(BF16)
    o_ref[...] = jnp.dot(a, w_ref[...].astype(BF16), preferred_element_type=F32) + b_ref[...]


def modulation(c_all, w_mod_l, b_mod_l):
    r, d = c_all.shape
    n = w_mod_l.shape[1]
    tn = 768
    return pl.pallas_call(
        _mod_kernel,
        grid=(n // tn,),
        in_specs=[pl.BlockSpec((r, d), lambda j: (0, 0)),
                  pl.BlockSpec((d, tn), lambda j: (0, j)),
                  pl.BlockSpec((1, tn), lambda j: (0, j))],
        out_specs=pl.BlockSpec((r, tn), lambda j: (0, j)),
        out_shape=jax.ShapeDtypeStruct((r, n), F32),
        compiler_params=pltpu.CompilerParams(vmem_limit_bytes=VMEM_LIMIT),
        name="modulation",
    )(c_all, w_mod_l, b_mod_l.reshape(1, n))


def _prenorm_kernel(x_ref, g_ref, scale_ref, shift_ref, o_ref):
    x = x_ref[...]
    y = x * lax.rsqrt(jnp.mean(x * x, axis=-1, keepdims=True) + EPS)
    u = (y * g_ref[...]) * (1.0 + scale_ref[0]) + shift_ref[0]
    o_ref[...] = u.astype(o_ref.dtype)


def prenorm_mod(x2d, g_pre_l, scale, shift, rows_per_group, out_dtype=BF16):
    n, d = x2d.shape
    tm = 512
    tiles_per_group = rows_per_group // tm
    return pl.pallas_call(
        _prenorm_kernel,
        grid=(n // tm,),
        in_specs=[pl.BlockSpec((tm, d), lambda i: (i, 0)),
                  pl.BlockSpec((1, d), lambda i: (0, 0)),
                  pl.BlockSpec((1, 1, d), lambda i: (i // tiles_per_group, 0, 0)),
                  pl.BlockSpec((1, 1, d), lambda i: (i // tiles_per_group, 0, 0))],
        out_specs=pl.BlockSpec((tm, d), lambda i: (i, 0)),
        out_shape=jax.ShapeDtypeStruct((n, d), out_dtype),
        compiler_params=pltpu.CompilerParams(vmem_limit_bytes=VMEM_LIMIT),
        name="prenorm_mod",
    )(x2d, g_pre_l.reshape(1, d), scale, shift)


def _matmul_kernel(u_ref, w_ref, o_ref):
    o_ref[...] = jnp.dot(u_ref[...], w_ref[...], preferred_element_type=F32).astype(o_ref.dtype)


def matmul(u, w, tm, tn, out_dtype=F32, name="matmul"):
    n, k = u.shape
    m = w.shape[1]
    return pl.pallas_call(
        _matmul_kernel,
        grid=(m // tn, n // tm),
        in_specs=[pl.BlockSpec((tm, k), lambda j, i: (i, 0)),
                  pl.BlockSpec((k, tn), lambda j, i: (0, j))],
        out_specs=pl.BlockSpec((tm, tn), lambda j, i: (i, j)),
        out_shape=jax.ShapeDtypeStruct((n, m), out_dtype),
        compiler_params=pltpu.CompilerParams(vmem_limit_bytes=VMEM_LIMIT),
        name=name,
    )(u, w)


def _merge_kernel(u_ref, y_ref, wg_ref, bg_ref, wb_ref, wo_ref, h_ref, gate_ref, gpost_ref, o_ref, acc_ref):
    n = pl.program_id(1)
    u = u_ref[...]
    merged = None
    for i in range(N_BRANCH):
        g = jnp.dot(u, wg_ref[i], preferred_element_type=F32) + bg_ref[i]
        yb = jnp.dot(y_ref[i], wb_ref[i], preferred_element_type=F32)
        term = jax.nn.sigmoid(g) * yb
        merged = term if merged is None else merged + term
    part = jnp.dot(merged.astype(BF16), wo_ref[...], preferred_element_type=F32)

    @pl.when(n == 0)
    def _():
        acc_ref[...] = part

    @pl.when(n > 0)
    def _():
        acc_ref[...] += part

    @pl.when(n == pl.num_programs(1) - 1)
    def _():
        y = acc_ref[...]
        yn = y * lax.rsqrt(jnp.mean(y * y, axis=-1, keepdims=True) + EPS) * gpost_ref[...]
        o_ref[...] = h_ref[...] + gate_ref[0] * yn


def merge(u, ys, w_gate_l, b_gate_l, w_branch_l, w_out_l, h2d, gate, g_post_l, rows_per_group):
    n, d = u.shape
    tm, tn = 512, 256
    tiles_per_group = rows_per_group // tm
    return pl.pallas_call(
        _merge_kernel,
        grid=(n // tm, d // tn),
        in_specs=[pl.BlockSpec((tm, d), lambda i, j: (i, 0)),
                  pl.BlockSpec((N_BRANCH, tm, BR_W), lambda i, j: (0, i, 0)),
                  pl.BlockSpec((N_BRANCH, d, tn), lambda i, j: (0, 0, j)),
                  pl.BlockSpec((N_BRANCH, 1, tn), lambda i, j: (0, 0, j)),
                  pl.BlockSpec((N_BRANCH, BR_W, tn), lambda i, j: (0, 0, j)),
                  pl.BlockSpec((tn, d), lambda i, j: (j, 0)),
                  pl.BlockSpec((tm, d), lambda i, j: (i, 0)),
                  pl.BlockSpec((1, 1, d), lambda i, j: (i // tiles_per_group, 0, 0)),
                  pl.BlockSpec((1, d), lambda i, j: (0, 0))],
        out_specs=pl.BlockSpec((tm, d), lambda i, j: (i, 0)),
        out_shape=jax.ShapeDtypeStruct((n, d), F32),
        scratch_shapes=[pltpu.VMEM((tm, d), F32)],
        compiler_params=pltpu.CompilerParams(vmem_limit_bytes=VMEM_LIMIT),
        name="merge",
    )(u, ys, w_gate_l.astype(BF16), b_gate_l.reshape(N_BRANCH, 1, d), w_branch_l.astype(BF16),
      w_out_l.astype(BF16), h2d, gate, g_post_l.reshape(1, d))


def rmsnorm(x, g):
    xf = x.astype(F32)
    y = xf * lax.rsqrt(jnp.mean(xf * xf, axis=-1, keepdims=True) + EPS)
    return (y * g.astype(F32)).astype(x.dtype)


def l2norm(x):
    return x * lax.rsqrt(jnp.sum(x * x, axis=-1, keepdims=True) + EPS)


def to_heads(x, h):
    b, t, _ = x.shape
    return x.reshape(b, t, h, -1).transpose(0, 2, 1, 3)


def from_heads(x):
    b, h, t, d = x.shape
    return x.transpose(0, 2, 1, 3).reshape(b, t, h * d)


def dwconv(x, w):
    ch, k = w.shape
    return lax.conv_general_dilated(x, w.T[:, None, :].astype(x.dtype), window_strides=(1,),
                                    padding=[(k // 2, k // 2)], dimension_numbers=("NWC", "WIO", "NWC"),
                                    feature_group_count=ch)


def axial_rope(t_len):
    pos = jnp.arange(t_len)
    row = (pos // GRID_W).astype(F32)
    col = (pos % GRID_W).astype(F32)
    n = GLA_DK // 4
    freqs = ROPE_THETA ** (-jnp.arange(n, dtype=F32) / n)
    ang = jnp.concatenate([row[:, None] * freqs, col[:, None] * freqs], axis=-1)
    return jnp.cos(ang), jnp.sin(ang)


def apply_rope(x, cos, sin):
    x1, x2 = jnp.split(x, 2, axis=-1)
    return jnp.concatenate([x1 * cos - x2 * sin, x2 * cos + x1 * sin], axis=-1)


def to_chunks(x):
    b, h, t = x.shape[:3]
    return jnp.moveaxis(x.reshape(b, h, t // CHUNK, CHUNK, *x.shape[3:]), 2, 0)


def from_chunks(x):
    n, b, h, cl = x.shape[:4]
    return jnp.moveaxis(x, 0, 2).reshape(b, h, n * cl, *x.shape[4:])


def na_col_tables():
    qcol = np.arange(N_CBLK)[:, None] * QBLK_C + np.arange(QBLK_C)[None]
    kstart = np.clip(np.arange(N_CBLK) * QBLK_C - WIN_C // 2, 0, GRID_W - KBLK_C)
    kcol = kstart[:, None] + np.arange(KBLK_C)[None]
    cstart = np.clip(qcol - WIN_C // 2, 0, GRID_W - WIN_C)
    kc = kcol[:, None, :]
    cmask = (kc >= cstart[..., None]) & (kc < cstart[..., None] + WIN_C)
    dc_idx = np.clip(kc - qcol[..., None] + WIN_C - 1, 0, 2 * WIN_C - 2)
    return kcol, cmask, dc_idx


def neighbourhood_attention(q, k, v, k_ctx, v_ctx, rpb):
    b, h, t, dh = q.shape
    rows = t // GRID_W
    wr = min(WIN_R, rows)
    kcol, cmask, dc_idx = na_col_tables()
    scale = dh ** -0.5
    qg = q.reshape(b, h, rows, N_CBLK, QBLK_C, dh).transpose(2, 0, 1, 3, 4, 5)
    kg = k.reshape(b, h, rows, GRID_W, dh)
    vg = v.reshape(b, h, rows, GRID_W, dh)

    def row_block(args):
        r, q_r = args
        rs = jnp.clip(r - wr // 2, 0, rows - wr)
        k_r = lax.dynamic_slice_in_dim(kg, rs, wr, axis=2)[:, :, :, kcol]
        v_r = lax.dynamic_slice_in_dim(vg, rs, wr, axis=2)[:, :, :, kcol]
        dr_idx = rs + jnp.arange(wr) - r + WIN_R - 1
        bias = rpb[:, dr_idx[None, None, :, None], dc_idx[:, :, None, :]]
        s_win = jnp.einsum("bhcqd,bhrckd->bhcqrk", q_r, k_r) * scale + bias
        s_win = jnp.where(cmask[:, :, None, :], s_win, -jnp.inf).reshape(b, h, N_CBLK, QBLK_C, wr * KBLK_C)
        s_ctx = jnp.einsum("bhcqd,bhkd->bhcqk", q_r, k_ctx) * scale
        p = jax.nn.softmax(jnp.concatenate([s_win, s_ctx], axis=-1), axis=-1)
        p_win = p[..., :wr * KBLK_C].reshape(b, h, N_CBLK, QBLK_C, wr, KBLK_C)
        return (jnp.einsum("bhcqrk,bhrckd->bhcqd", p_win, v_r)
                + jnp.einsum("bhcqk,bhkd->bhcqd", p[..., wr * KBLK_C:], v_ctx))

    o = lax.map(row_block, (jnp.arange(rows), qg))
    return o.transpose(1, 2, 0, 3, 4, 5).reshape(b, h, t, dh)


def dense_attention(q, k, v):
    s = jnp.einsum("bhqd,bhkd->bhqk", q, k) * (q.shape[-1] ** -0.5)
    return jnp.einsum("bhqk,bhkd->bhqd", jax.nn.softmax(s, axis=-1), v)


def gla_scan(q, k, v, g, s0):
    qc, kc, vc = to_chunks(q), to_chunks(k), to_chunks(v)
    bc = jnp.cumsum(to_chunks(g), axis=3)
    incl = jnp.tril(jnp.ones((CHUNK, CHUNK), bool))

    def step(s, inp):
        q_, k_, v_, b_ = inp
        diff = jnp.where(incl[:, :, None], b_[..., :, None, :] - b_[..., None, :, :], -jnp.inf)
        att = jnp.einsum("bhtd,bhsd,bhtsd->bhts", q_, k_, jnp.exp(diff))
        b_end = b_[..., -1:, :]
        o = att @ v_ + (q_ * jnp.exp(b_)) @ s
        s = jnp.exp(b_end[..., 0, :])[..., None] * s + jnp.einsum("bhsd,bhse->bhde", k_ * jnp.exp(b_end - b_), v_)
        return s, o

    s_fin, o = lax.scan(step, s0, (qc, kc, vc, bc))
    return from_chunks(o), s_fin


def gla_inputs(p, wg2, bg, rope):
    q = to_heads(p["gla_q"], GLA_HEADS).astype(F32) * GLA_DK ** -0.5
    k = to_heads(p["gla_k"], GLA_HEADS).astype(F32)
    v = to_heads(p["gla_v"], GLA_HEADS).astype(F32)
    if rope is not None:
        q, k = apply_rope(q, *rope), apply_rope(k, *rope)
    lr = p["gla_g"].astype(F32)
    dirs = []
    for d in range(2):
        gl = jax.nn.log_sigmoid(lr[..., d * GLA_RANK:(d + 1) * GLA_RANK] @ wg2[d] + bg[d]) / GLA_TAU
        gl = to_heads(gl, GLA_HEADS)
        dirs.append((q, k, v, jnp.concatenate([gl, gl], axis=-1)))
    return dirs


def gdn_scan(q, k, v, beta, g, s0):
    qc, kc, vc = to_chunks(q), to_chunks(k), to_chunks(v)
    bt = to_chunks(beta)
    gam = jnp.cumsum(to_chunks(g), axis=-1)
    dv = v.shape[-1]
    strict = jnp.tril(jnp.ones((CHUNK, CHUNK), bool), -1)
    incl = jnp.tril(jnp.ones((CHUNK, CHUNK), bool))
    diff = gam[..., :, None] - gam[..., None, :]
    a_kk = jnp.einsum("nbhtd,nbhsd->nbhts", kc * bt[..., None], kc) * jnp.exp(jnp.where(strict, diff, -jnp.inf))
    rhs = jnp.concatenate([vc * bt[..., None], kc * (bt * jnp.exp(gam))[..., None]], axis=-1)
    sol = lax.linalg.triangular_solve(a_kk + jnp.eye(CHUNK, dtype=F32), rhs, left_side=True, lower=True,
                                      unit_diagonal=True)
    u, w = sol[..., :dv], sol[..., dv:]
    a_qk = jnp.einsum("nbhtd,nbhsd->nbhts", qc, kc) * jnp.exp(jnp.where(incl, diff, -jnp.inf))
    q_dec = qc * jnp.exp(gam)[..., None]
    k_dec = kc * jnp.exp(gam[..., -1:] - gam)[..., None]
    d_end = jnp.exp(gam[..., -1])

    def step(s, inp):
        u_, w_, aqk_, qd_, kd_, de_ = inp
        e = u_ - w_ @ s
        o = qd_ @ s + aqk_ @ e
        s = de_[..., None, None] * s + jnp.einsum("bhsd,bhse->bhde", kd_, e)
        return s, o

    s_fin, o = lax.scan(step, s0, (u, w, a_qk, q_dec, k_dec, d_end))
    return from_chunks(o), s_fin


def gdn_inputs(p, conv_w, a_log, dt_bias):
    qkv = jax.nn.silu(dwconv(p["gdn_qkv"], conv_w).astype(F32))
    q, k, v = [to_heads(a, GDN_HEADS) for a in jnp.split(qkv, 3, axis=-1)]
    q = l2norm(q) * GDN_DH ** -0.5
    k = l2norm(k)
    a = p["gdn_a"].astype(F32)
    bb = p["gdn_b"].astype(F32)
    dirs = []
    for d in range(2):
        sl = slice(d * GDN_HEADS, (d + 1) * GDN_HEADS)
        beta = jax.nn.sigmoid(bb[..., sl]).transpose(0, 2, 1)
        g = (-jnp.exp(a_log[d]) * jax.nn.softplus(a[..., sl] + dt_bias[d])).transpose(0, 2, 1)
        dirs.append((q, k, v, beta, g))
    return dirs


def bidir_scan(scan_fn, ctx_dirs, lat_dirs, s0):
    def flip(arrs):
        return [jnp.flip(a, axis=2) for a in arrs]
    oc_f, sc_f = scan_fn(*ctx_dirs[0], s0)
    ol_f, _ = scan_fn(*lat_dirs[0], sc_f)
    oc_b, sc_b = scan_fn(*flip(ctx_dirs[1]), s0)
    ol_b, _ = scan_fn(*flip(lat_dirs[1]), sc_b)
    return oc_f + jnp.flip(oc_b, axis=2), ol_f + jnp.flip(ol_b, axis=2)


def hyena_filters(length, w_in, b_in, w_mid, b_mid, freq, w_out):
    t = jnp.linspace(0.0, 1.0, length, dtype=F32)[:, None]
    bands = (HY_EMB - 1) // 2
    wpos = 2.0 * math.pi * jnp.arange(length, dtype=F32)[:, None] / length
    f = jnp.linspace(1e-4, bands - 1, bands, dtype=F32)[None]
    z = jnp.concatenate([t, jnp.cos(f * wpos), -jnp.sin(f * wpos)], axis=-1)
    h = jnp.sin(freq[0] * (z @ w_in + b_in))
    for i in range(HY_INNER):
        h = jnp.sin(freq[i + 1] * (h @ w_mid[i] + b_mid[i]))
    h = h @ w_out
    deltas = jnp.abs(jnp.linspace(math.log(HY_TARGET) / HY_SLOW, math.log(HY_TARGET) / HY_FAST, BR_W, dtype=F32))
    decay = jnp.exp(-t * deltas)
    return h[:, :BR_W] * decay, h[:, BR_W:] * decay


def two_sided_long_conv(u, h_f, h_b):
    length = u.shape[1]
    n = 2 * length
    kc = jnp.concatenate([h_f[:1] + h_b[:1], h_f[1:], jnp.zeros_like(h_f[:1]), h_b[:0:-1]], axis=0)
    y = jnp.fft.irfft(jnp.fft.rfft(u, n=n, axis=1) * jnp.fft.rfft(kc, n=n, axis=0)[None], n=n, axis=1)
    return y[:, :length]


def hyena_stream(p_xv, conv_w, conv_b, filt, skip):
    xv = (dwconv(p_xv, conv_w) + conv_b).astype(F32)
    x0, x1, v = jnp.split(xv, 3, axis=-1)
    h_f, h_b = hyena_filters(xv.shape[1], *filt)
    v = v * x1
    v = two_sided_long_conv(v, h_f, h_b) + skip * v
    return v * x0


def _split_cols(p_main, p_small):
    out = {}
    for name in MAIN_ORDER:
        o, w = MY_OFF[name]
        out[name] = p_main[..., o:o + w]
    for name in SMALL_ORDER:
        o, w = MY_OFF[name]
        out[name] = p_small[..., o:o + w]
    return out


def mixers(pl_, pc_, na_rpb, gla_wg2, gla_bg, gla_norm, gdn_conv, gdn_a_log, gdn_dt_bias, gdn_norm,
           hy_conv, hy_conv_b, hy_filt, hy_skip, with_ctx):
    b, t, _ = pl_["na_z"].shape
    qa, ka, va = [to_heads(a, NA_HEADS).astype(F32) for a in jnp.split(pl_["na_qkv"], 3, axis=-1)]
    qac, kac, vac = [to_heads(a, NA_HEADS).astype(F32) for a in jnp.split(pc_["na_qkv"], 3, axis=-1)]
    oa = neighbourhood_attention(qa, ka, va, kac, vac, na_rpb.astype(F32))

    wg2, bg = gla_wg2.astype(F32), gla_bg.astype(F32)
    s0_b = jnp.zeros((b, GLA_HEADS, GLA_DK, GLA_DV), F32)
    ob_c, ob = bidir_scan(gla_scan, gla_inputs(pc_, wg2, bg, None), gla_inputs(pl_, wg2, bg, axial_rope(t)), s0_b)

    a_log, dtb = gdn_a_log.astype(F32), gdn_dt_bias.astype(F32)
    s0_c = jnp.zeros((b, GDN_HEADS, GDN_DH, GDN_DH), F32)
    oc_c, oc = bidir_scan(gdn_scan, gdn_inputs(pc_, gdn_conv, a_log, dtb), gdn_inputs(pl_, gdn_conv, a_log, dtb), s0_c)

    filt = tuple(a.astype(F32) for a in hy_filt)
    skip = hy_skip.astype(F32)
    od = hyena_stream(pl_["hy_xv"], hy_conv, hy_conv_b, filt, skip)

    def finish(p, oa_, ob_, oc_, od_):
        return jnp.stack([from_heads(oa_) * jax.nn.silu(p["na_z"]),
                          from_heads(rmsnorm(ob_, gla_norm)) * jax.nn.silu(p["gla_z"]),
                          from_heads(rmsnorm(oc_, gdn_norm)) * jax.nn.silu(p["gdn_z"]),
                          od_ * jax.nn.silu(p["hy_z"])]).astype(BF16)

    ys = finish(pl_, oa, ob, oc, od)
    if not with_ctx:
        return ys, None
    oa_c = dense_attention(qac, kac, vac)
    od_c = hyena_stream(pc_["hy_xv"], hy_conv, hy_conv_b, filt, skip)
    return ys, finish(pc_, oa_c, ob_c, oc_c, od_c)


def kernel(x, c, ctx, c_ctx, w_mod, b_mod, g_pre, g_post, w_in, w_gate, b_gate, w_branch, w_out, na_rpb, gla_wg2, gla_bg, gla_norm, gdn_conv, gdn_a_log, gdn_dt_bias, gdn_norm, hy_conv, hy_conv_b, hy_w_in, hy_b_in, hy_w_mid, hy_b_mid, hy_freq, hy_w_out, hy_skip):
    b, t, d = x.shape
    tc = ctx.shape[1]
    h = x.reshape(b * t, d)
    hc = ctx.reshape(b * tc, d)
    c_all = jnp.concatenate([c, c_ctx[None], jnp.zeros((7, d), F32)], axis=0)
    for l in range(DEPTH):
        with_ctx = l < DEPTH - 1
        mod = modulation(c_all, w_mod[l], b_mod[l])
        shift, scale, gate = [mod[:, i * d:(i + 1) * d].reshape(-1, 1, d) for i in range(3)]
        u = prenorm_mod(h, g_pre[l], scale[:b], shift[:b], t)
        uc = prenorm_mod(hc, g_pre[l], scale[b:b + 1], shift[b:b + 1], b * tc)
        w_main, w_small = _permute_w_in(w_in[l])
        p_main = matmul(u, w_main, 1024, 1536, name="in_proj").reshape(b, t, N_MAIN)
        p_small = matmul(u, w_small, 1024, N_SMALL, name="in_proj_small").reshape(b, t, N_SMALL)
        pc_main = matmul(uc, w_main, 1024, 1536, name="in_proj_ctx").reshape(b, tc, N_MAIN)
        pc_small = matmul(uc, w_small, 1024, N_SMALL, name="in_proj_ctx_small").reshape(b, tc, N_SMALL)
        ys, ycs = mixers(_split_cols(p_main, p_small), _split_cols(pc_main, pc_small),
                         na_rpb[l], gla_wg2[l], gla_bg[l], gla_norm[l], gdn_conv[l], gdn_a_log[l], gdn_dt_bias[l],
                         gdn_norm[l], hy_conv[l], hy_conv_b[l],
                         (hy_w_in[l], hy_b_in[l], hy_w_mid[l], hy_b_mid[l], hy_freq[l], hy_w_out[l]),
                         hy_skip[l], with_ctx)
        h_new = merge(u, ys.reshape(N_BRANCH, b * t, BR_W), w_gate[l], b_gate[l], w_branch[l], w_out[l],
                      h, gate[:b], g_post[l], t)
        if with_ctx:
            hc = merge(uc, ycs.reshape(N_BRANCH, b * tc, BR_W), w_gate[l], b_gate[l], w_branch[l], w_out[l],
                       hc, gate[b:b + 1], g_post[l], b * tc)
        h = h_new
    return h.reshape(b, t, d)
```

```python
import functools
import math

import numpy as np
import jax
import jax.numpy as jnp
from jax import lax
from jax.experimental import pallas as pl
from jax.experimental.pallas import tpu as pltpu

D_MODEL = 2048
BATCH = 16
SEQ = 2048
DEPTH = 2
GRID_W = 64
CTX_LEN = 256
N_BRANCH = 4
BR_W = D_MODEL // N_BRANCH
NA_HEADS = 8
NA_DH = BR_W // NA_HEADS
WIN_R = 8
WIN_C = 16
QBLK_C = 16
KBLK_C = WIN_C + QBLK_C
N_CBLK = GRID_W // QBLK_C
GLA_HEADS = 4
GLA_DV = BR_W // GLA_HEADS
GLA_DK = GLA_DV // 2
GLA_RANK = 16
GLA_TAU = 16.0
CHUNK = 64
GDN_HEADS = 4
GDN_DH = BR_W // GDN_HEADS
GDN_CONV = 5
HY_CONV = 3
HY_EMB = 33
HY_FFN = 64
HY_INNER = 2
HY_TARGET = 1e-2
HY_FAST = 0.3
HY_SLOW = 1.5
ROPE_THETA = 10000.0
EPS = 1e-6
F32 = jnp.float32
BF16 = jnp.bfloat16

REF_SPLITS = (
    ("na_qkv", 3 * BR_W), ("na_z", BR_W),
    ("gla_q", GLA_HEADS * GLA_DK), ("gla_k", GLA_HEADS * GLA_DK), ("gla_v", BR_W), ("gla_z", BR_W),
    ("gla_g", 2 * GLA_RANK),
    ("gdn_qkv", 3 * BR_W), ("gdn_z", BR_W), ("gdn_a", 2 * GDN_HEADS), ("gdn_b", 2 * GDN_HEADS),
    ("hy_xv", 3 * BR_W), ("hy_z", BR_W),
)
MAIN_ORDER = ("na_qkv", "na_z", "gla_q", "gla_k", "gla_v", "gla_z", "gdn_qkv", "gdn_z", "hy_xv", "hy_z")
SMALL_ORDER = ("gla_g", "gdn_a", "gdn_b")
LANE = 128
N_MAIN = sum(w for n, w in REF_SPLITS if n in MAIN_ORDER)
N_SMALL = LANE
VMEM_LIMIT = 48 * 1024 * 1024


def _ref_offsets():
    out, o = {}, 0
    for name, w in REF_SPLITS:
        out[name] = (o, w)
        o += w
    return out


def _my_offsets():
    out, o = {}, 0
    for name in MAIN_ORDER:
        w = dict(REF_SPLITS)[name]
        out[name] = (o, w)
        o += w
    o = 0
    for name in SMALL_ORDER:
        w = dict(REF_SPLITS)[name]
        out[name] = (o, w)
        o += w
    return out


REF_OFF = _ref_offsets()
MY_OFF = _my_offsets()


def _permute_w_in(w_in_l):
    main = jnp.concatenate([w_in_l[:, REF_OFF[n][0]:REF_OFF[n][0] + REF_OFF[n][1]] for n in MAIN_ORDER], axis=1)
    small = jnp.concatenate([w_in_l[:, REF_OFF[n][0]:REF_OFF[n][0] + REF_OFF[n][1]] for n in SMALL_ORDER], axis=1)
    small = jnp.pad(small, ((0, 0), (0, N_SMALL - small.shape[1])))
    return main.astype(BF16), small.astype(BF16)


def _mod_kernel(c_ref, w_ref, b_ref, o_ref):
    c = c_ref[...]
    a = (c * jax.nn.sigmoid(c)).astype(BF16)
    o_ref[...] = jnp.dot(a, w_ref[...].astype(BF16), preferred_element_type=F32) + b_ref[...]


def modulation(c_all, w_mod_l, b_mod_l):
    r, d = c_all.shape
    n = w_mod_l.shape[1]
    tn = 768
    return pl.pallas_call(
        _mod_kernel,
        grid=(n // tn,),
        in_specs=[pl.BlockSpec((r, d), lambda j: (0, 0)),
                  pl.BlockSpec((d, tn), lambda j: (0, j)),
                  pl.BlockSpec((1, tn), lambda j: (0, j))],
        out_specs=pl.BlockSpec((r, tn), lambda j: (0, j)),
        out_shape=jax.ShapeDtypeStruct((r, n), F32),
        compiler_params=pltpu.CompilerParams(vmem_limit_bytes=VMEM_LIMIT),
        name="modulation",
    )(c_all, w_mod_l, b_mod_l.reshape(1, n))


def _prenorm_kernel(x_ref, g_ref, scale_ref, shift_ref, o_ref):
    x = x_ref[...]
    y = x * lax.rsqrt(jnp.mean(x * x, axis=-1, keepdims=True) + EPS)
    u = (y * g_ref[...]) * (1.0 + scale_ref[0]) + shift_ref[0]
    o_ref[...] = u.astype(o_ref.dtype)


def prenorm_mod(x2d, g_pre_l, scale, shift, rows_per_group, out_dtype=BF16):
    n, d = x2d.shape
    tm = 512
    tiles_per_group = rows_per_group // tm
    return pl.pallas_call(
        _prenorm_kernel,
        grid=(n // tm,),
        in_specs=[pl.BlockSpec((tm, d), lambda i: (i, 0)),
                  pl.BlockSpec((1, d), lambda i: (0, 0)),
                  pl.BlockSpec((1, 1, d), lambda i: (i // tiles_per_group, 0, 0)),
                  pl.BlockSpec((1, 1, d), lambda i: (i // tiles_per_group, 0, 0))],
        out_specs=pl.BlockSpec((tm, d), lambda i: (i, 0)),
        out_shape=jax.ShapeDtypeStruct((n, d), out_dtype),
        compiler_params=pltpu.CompilerParams(vmem_limit_bytes=VMEM_LIMIT),
        name="prenorm_mod",
    )(x2d, g_pre_l.reshape(1, d), scale, shift)


def _matmul_kernel(u_ref, w_ref, o_ref):
    o_ref[...] = jnp.dot(u_ref[...], w_ref[...], preferred_element_type=F32).astype(o_ref.dtype)


def matmul(u, w, tm, tn, out_dtype=F32, name="matmul"):
    n, k = u.shape
    m = w.shape[1]
    return pl.pallas_call(
        _matmul_kernel,
        grid=(m // tn, n // tm),
        in_specs=[pl.BlockSpec((tm, k), lambda j, i: (i, 0)),
                  pl.BlockSpec((k, tn), lambda j, i: (0, j))],
        out_specs=pl.BlockSpec((tm, tn), lambda j, i: (i, j)),
        out_shape=jax.ShapeDtypeStruct((n, m), out_dtype),
        compiler_params=pltpu.CompilerParams(vmem_limit_bytes=VMEM_LIMIT),
        name=name,
    )(u, w)


def _merge_kernel(u_ref, y0_ref, y1_ref, y2_ref, y3_ref, wg_ref, bg_ref, wb_ref, wo_ref, h_ref, gate_ref, gpost_ref,
                  o_ref, acc_ref):
    n = pl.program_id(1)
    u = u_ref[...]
    merged = None
    for i, y_ref in enumerate((y0_ref, y1_ref, y2_ref, y3_ref)):
        g = jnp.dot(u, wg_ref[i], preferred_element_type=F32) + bg_ref[i]
        yb = jnp.dot(y_ref[...], wb_ref[i], preferred_element_type=F32)
        term = jax.nn.sigmoid(g) * yb
        merged = term if merged is None else merged + term
    part = jnp.dot(merged.astype(BF16), wo_ref[...], preferred_element_type=F32)

    @pl.when(n == 0)
    def _():
        acc_ref[...] = part

    @pl.when(n > 0)
    def _():
        acc_ref[...] += part

    @pl.when(n == pl.num_programs(1) - 1)
    def _():
        y = acc_ref[...]
        yn = y * lax.rsqrt(jnp.mean(y * y, axis=-1, keepdims=True) + EPS) * gpost_ref[...]
        o_ref[...] = h_ref[...] + gate_ref[0] * yn


def merge(u, ys, w_gate_l, b_gate_l, w_branch_l, w_out_l, h2d, gate, g_post_l, rows_per_group):
    n, d = u.shape
    tm, tn = 512, 256
    tiles_per_group = rows_per_group // tm
    yspec = pl.BlockSpec((tm, BR_W), lambda i, j: (i, 0))
    return pl.pallas_call(
        _merge_kernel,
        grid=(n // tm, d // tn),
        in_specs=[pl.BlockSpec((tm, d), lambda i, j: (i, 0)),
                  yspec, yspec, yspec, yspec,
                  pl.BlockSpec((N_BRANCH, d, tn), lambda i, j: (0, 0, j)),
                  pl.BlockSpec((N_BRANCH, 1, tn), lambda i, j: (0, 0, j)),
                  pl.BlockSpec((N_BRANCH, BR_W, tn), lambda i, j: (0, 0, j)),
                  pl.BlockSpec((tn, d), lambda i, j: (j, 0)),
                  pl.BlockSpec((tm, d), lambda i, j: (i, 0)),
                  pl.BlockSpec((1, 1, d), lambda i, j: (i // tiles_per_group, 0, 0)),
                  pl.BlockSpec((1, d), lambda i, j: (0, 0))],
        out_specs=pl.BlockSpec((tm, d), lambda i, j: (i, 0)),
        out_shape=jax.ShapeDtypeStruct((n, d), F32),
        scratch_shapes=[pltpu.VMEM((tm, d), F32)],
        compiler_params=pltpu.CompilerParams(vmem_limit_bytes=VMEM_LIMIT),
        name="merge",
    )(u, *[y.reshape(n, BR_W) for y in ys], w_gate_l.astype(BF16), b_gate_l.reshape(N_BRANCH, 1, d),
      w_branch_l.astype(BF16),
      w_out_l.astype(BF16), h2d, gate, g_post_l.reshape(1, d))


def rmsnorm(x, g):
    xf = x.astype(F32)
    y = xf * lax.rsqrt(jnp.mean(xf * xf, axis=-1, keepdims=True) + EPS)
    return (y * g.astype(F32)).astype(x.dtype)


def l2norm(x):
    return x * lax.rsqrt(jnp.sum(x * x, axis=-1, keepdims=True) + EPS)


def to_heads(x, h):
    b, t, _ = x.shape
    return x.reshape(b, t, h, -1).transpose(0, 2, 1, 3)


def from_heads(x):
    b, h, t, d = x.shape
    return x.transpose(0, 2, 1, 3).reshape(b, t, h * d)


def dwconv(x, w):
    ch, k = w.shape
    return lax.conv_general_dilated(x, w.T[:, None, :].astype(x.dtype), window_strides=(1,),
                                    padding=[(k // 2, k // 2)], dimension_numbers=("NWC", "WIO", "NWC"),
                                    feature_group_count=ch)


def axial_rope(t_len):
    pos = jnp.arange(t_len)
    row = (pos // GRID_W).astype(F32)
    col = (pos % GRID_W).astype(F32)
    n = GLA_DK // 4
    freqs = ROPE_THETA ** (-jnp.arange(n, dtype=F32) / n)
    ang = jnp.concatenate([row[:, None] * freqs, col[:, None] * freqs], axis=-1)
    return jnp.cos(ang), jnp.sin(ang)


def apply_rope(x, cos, sin):
    x1, x2 = jnp.split(x, 2, axis=-1)
    return jnp.concatenate([x1 * cos - x2 * sin, x2 * cos + x1 * sin], axis=-1)


def to_chunks(x):
    b, h, t = x.shape[:3]
    return jnp.moveaxis(x.reshape(b, h, t // CHUNK, CHUNK, *x.shape[3:]), 2, 0)


def from_chunks(x):
    n, b, h, cl = x.shape[:4]
    return jnp.moveaxis(x, 0, 2).reshape(b, h, n * cl, *x.shape[4:])


NA_NEG = -1e30
NA_PAIR = LANE // NA_DH


def na_bias_table(rpb, rows):
    assert rows >= WIN_R
    off = np.arange(WIN_R)[:, None, None, None]
    c = np.arange(GRID_W)[None, :, None, None]
    j = np.arange(WIN_R)[None, None, :, None]
    kc = np.arange(GRID_W)[None, None, None, :]
    shape = (WIN_R, GRID_W, WIN_R, GRID_W)
    dr = np.broadcast_to(j - off + WIN_R - 1, shape)
    cstart = np.clip(c - WIN_C // 2, 0, GRID_W - WIN_C)
    valid = np.broadcast_to((kc >= cstart) & (kc < cstart + WIN_C), shape)
    dc = np.broadcast_to(np.clip(kc - c + WIN_C - 1, 0, 2 * WIN_C - 2), shape)
    tab = jnp.where(valid, rpb.astype(F32)[:, dr, dc], NA_NEG)
    return tab.reshape(rpb.shape[0], WIN_R, GRID_W, WIN_R * GRID_W)


def _nt_dot(a, b):
    return lax.dot_general(a, b, (((1,), (1,)), ((), ())), preferred_element_type=F32)


def _na_kernel(q_ref, k_ref, v_ref, kc_ref, vc_ref, z_ref, bias_ref, o_ref, *, rows):
    lane = lax.broadcasted_iota(jnp.int32, (GRID_W, LANE), 1)
    head_mask = [(lane >= hh * NA_DH) & (lane < (hh + 1) * NA_DH) for hh in range(NA_PAIR)]
    kc = kc_ref[0].astype(BF16)
    vc = vc_ref[0].astype(BF16)
    win = WIN_R * GRID_W

    def body(r, carry):
        rs = jnp.clip(r - WIN_R // 2, 0, rows - WIN_R)
        off = r - rs
        q0 = pl.multiple_of(r * GRID_W, GRID_W)
        k0 = pl.multiple_of(rs * GRID_W, GRID_W)
        q = q_ref[0, pl.ds(q0, GRID_W), :] * (NA_DH ** -0.5)
        kw = k_ref[0, pl.ds(k0, win), :].astype(BF16)
        vw = v_ref[0, pl.ds(k0, win), :].astype(BF16)
        out = None
        for hh in range(NA_PAIR):
            qh = jnp.where(head_mask[hh], q, 0.0).astype(BF16)
            s = _nt_dot(qh, kw) + bias_ref[hh, off]
            sc = _nt_dot(qh, kc)
            m = jnp.maximum(jnp.max(s, axis=-1, keepdims=True), jnp.max(sc, axis=-1, keepdims=True))
            p = jnp.exp(s - m)
            pc = jnp.exp(sc - m)
            l = jnp.sum(p, axis=-1, keepdims=True) + jnp.sum(pc, axis=-1, keepdims=True)
            o = (jnp.dot(p.astype(BF16), vw, preferred_element_type=F32)
                 + jnp.dot(pc.astype(BF16), vc, preferred_element_type=F32)) / l
            out = o if out is None else jnp.where(head_mask[hh], o, out)
        z = z_ref[0, pl.ds(q0, GRID_W), :]
        o_ref[0, pl.ds(q0, GRID_W), :] = (out * (z * jax.nn.sigmoid(z))).astype(o_ref.dtype)
        return carry

    lax.fori_loop(0, rows, body, 0)


def na_branch(p_main, pc_main, rpb):
    b, t, _ = p_main.shape
    tc = pc_main.shape[1]
    rows = t // GRID_W
    nj = BR_W // LANE
    o_q = MY_OFF["na_qkv"][0] // LANE
    o_z = MY_OFF["na_z"][0] // LANE
    bias = na_bias_table(rpb, rows)
    lat = lambda o: pl.BlockSpec((1, t, LANE), lambda i, j: (i, 0, o + j))
    cx = lambda o: pl.BlockSpec((1, tc, LANE), lambda i, j: (i, 0, o + j))
    return pl.pallas_call(
        functools.partial(_na_kernel, rows=rows),
        grid=(b, nj),
        in_specs=[lat(o_q), lat(o_q + nj), lat(o_q + 2 * nj), cx(o_q + nj), cx(o_q + 2 * nj), lat(o_z),
                  pl.BlockSpec((NA_PAIR, WIN_R, GRID_W, WIN_R * GRID_W), lambda i, j: (j, 0, 0, 0))],
        out_specs=pl.BlockSpec((1, t, LANE), lambda i, j: (i, 0, j)),
        out_shape=jax.ShapeDtypeStruct((b, t, BR_W), BF16),
        compiler_params=pltpu.CompilerParams(vmem_limit_bytes=VMEM_LIMIT),
        name="na_attention",
    )(p_main, p_main, p_main, pc_main, pc_main, p_main, bias)


def _ctx_attn_kernel(q_ref, k_ref, v_ref, z_ref, o_ref):
    tc = q_ref.shape[1]
    lane = lax.broadcasted_iota(jnp.int32, (tc, LANE), 1)
    q = q_ref[0] * (NA_DH ** -0.5)
    k = k_ref[0].astype(BF16)
    v = v_ref[0].astype(BF16)
    out = None
    for hh in range(NA_PAIR):
        mask = (lane >= hh * NA_DH) & (lane < (hh + 1) * NA_DH)
        s = _nt_dot(jnp.where(mask, q, 0.0).astype(BF16), k)
        p = jnp.exp(s - jnp.max(s, axis=-1, keepdims=True))
        o = jnp.dot(p.astype(BF16), v, preferred_element_type=F32) / jnp.sum(p, axis=-1, keepdims=True)
        out = o if out is None else jnp.where(mask, o, out)
    z = z_ref[0]
    o_ref[0] = (out * (z * jax.nn.sigmoid(z))).astype(o_ref.dtype)


def ctx_attention_branch(pc_main):
    b, tc, _ = pc_main.shape
    nj = BR_W // LANE
    o_q = MY_OFF["na_qkv"][0] // LANE
    o_z = MY_OFF["na_z"][0] // LANE
    cx = lambda o: pl.BlockSpec((1, tc, LANE), lambda i, j: (i, 0, o + j))
    return pl.pallas_call(
        _ctx_attn_kernel,
        grid=(b, nj),
        in_specs=[cx(o_q), cx(o_q + nj), cx(o_q + 2 * nj), cx(o_z)],
        out_specs=pl.BlockSpec((1, tc, LANE), lambda i, j: (i, 0, j)),
        out_shape=jax.ShapeDtypeStruct((b, tc, BR_W), BF16),
        compiler_params=pltpu.CompilerParams(vmem_limit_bytes=VMEM_LIMIT),
        name="ctx_attention",
    )(pc_main, pc_main, pc_main, pc_main)


def gla_scan(q, k, v, g, s0):
    qc, kc, vc = to_chunks(q), to_chunks(k), to_chunks(v)
    bc = jnp.cumsum(to_chunks(g), axis=3)
    incl = jnp.tril(jnp.ones((CHUNK, CHUNK), bool))

    def step(s, inp):
        q_, k_, v_, b_ = inp
        diff = jnp.where(incl[:, :, None], b_[..., :, None, :] - b_[..., None, :, :], -jnp.inf)
        att = jnp.einsum("bhtd,bhsd,bhtsd->bhts", q_, k_, jnp.exp(diff))
        b_end = b_[..., -1:, :]
        o = att @ v_ + (q_ * jnp.exp(b_)) @ s
        s = jnp.exp(b_end[..., 0, :])[..., None] * s + jnp.einsum("bhsd,bhse->bhde", k_ * jnp.exp(b_end - b_), v_)
        return s, o

    s_fin, o = lax.scan(step, s0, (qc, kc, vc, bc))
    return from_chunks(o), s_fin


def gla_inputs(p, wg2, bg, rope):
    q = to_heads(p["gla_q"], GLA_HEADS).astype(F32) * GLA_DK ** -0.5
    k = to_heads(p["gla_k"], GLA_HEADS).astype(F32)
    v = to_heads(p["gla_v"], GLA_HEADS).astype(F32)
    if rope is not None:
        q, k = apply_rope(q, *rope), apply_rope(k, *rope)
    lr = p["gla_g"].astype(F32)
    dirs = []
    for d in range(2):
        gl = jax.nn.log_sigmoid(lr[..., d * GLA_RANK:(d + 1) * GLA_RANK] @ wg2[d] + bg[d]) / GLA_TAU
        gl = to_heads(gl, GLA_HEADS)
        dirs.append((q, k, v, jnp.concatenate([gl, gl], axis=-1)))
    return dirs


def gdn_scan(q, k, v, beta, g, s0):
    qc, kc, vc = to_chunks(q), to_chunks(k), to_chunks(v)
    bt = to_chunks(beta)
    gam = jnp.cumsum(to_chunks(g), axis=-1)
    dv = v.shape[-1]
    strict = jnp.tril(jnp.ones((CHUNK, CHUNK), bool), -1)
    incl = jnp.tril(jnp.ones((CHUNK, CHUNK), bool))
    diff = gam[..., :, None] - gam[..., None, :]
    a_kk = jnp.einsum("nbhtd,nbhsd->nbhts", kc * bt[..., None], kc) * jnp.exp(jnp.where(strict, diff, -jnp.inf))
    rhs = jnp.concatenate([vc * bt[..., None], kc * (bt * jnp.exp(gam))[..., None]], axis=-1)
    sol = lax.linalg.triangular_solve(a_kk + jnp.eye(CHUNK, dtype=F32), rhs, left_side=True, lower=True,
                                      unit_diagonal=True)
    u, w = sol[..., :dv], sol[..., dv:]
    a_qk = jnp.einsum("nbhtd,nbhsd->nbhts", qc, kc) * jnp.exp(jnp.where(incl, diff, -jnp.inf))
    q_dec = qc * jnp.exp(gam)[..., None]
    k_dec = kc * jnp.exp(gam[..., -1:] - gam)[..., None]
    d_end = jnp.exp(gam[..., -1])

    def step(s, inp):
        u_, w_, aqk_, qd_, kd_, de_ = inp
        e = u_ - w_ @ s
        o = qd_ @ s + aqk_ @ e
        s = de_[..., None, None] * s + jnp.einsum("bhsd,bhse->bhde", kd_, e)
        return s, o

    s_fin, o = lax.scan(step, s0, (u, w, a_qk, q_dec, k_dec, d_end))
    return from_chunks(o), s_fin


def gdn_inputs(p, conv_w, a_log, dt_bias):
    qkv = jax.nn.silu(dwconv(p["gdn_qkv"], conv_w).astype(F32))
    q, k, v = [to_heads(a, GDN_HEADS) for a in jnp.split(qkv, 3, axis=-1)]
    q = l2norm(q) * GDN_DH ** -0.5
    k = l2norm(k)
    a = p["gdn_a"].astype(F32)
    bb = p["gdn_b"].astype(F32)
    dirs = []
    for d in range(2):
        sl = slice(d * GDN_HEADS, (d + 1) * GDN_HEADS)
        beta = jax.nn.sigmoid(bb[..., sl]).transpose(0, 2, 1)
        g = (-jnp.exp(a_log[d]) * jax.nn.softplus(a[..., sl] + dt_bias[d])).transpose(0, 2, 1)
        dirs.append((q, k, v, beta, g))
    return dirs


def bidir_scan(scan_fn, ctx_dirs, lat_dirs, s0):
    def flip(arrs):
        return [jnp.flip(a, axis=2) for a in arrs]
    oc_f, sc_f = scan_fn(*ctx_dirs[0], s0)
    ol_f, _ = scan_fn(*lat_dirs[0], sc_f)
    oc_b, sc_b = scan_fn(*flip(ctx_dirs[1]), s0)
    ol_b, _ = scan_fn(*flip(lat_dirs[1]), sc_b)
    return oc_f + jnp.flip(oc_b, axis=2), ol_f + jnp.flip(ol_b, axis=2)


HY_COLS = 256
HY_ROWS = 512


def dft_matrices(length):
    n = 2 * length
    r = jnp.arange(n, dtype=jnp.int32)[:, None]
    t = jnp.arange(length, dtype=jnp.int32)[None, :]
    k = jnp.where(r <= length, r, r - length)
    ang = ((k * t) % n).astype(F32) * (2.0 * math.pi / n)
    f = jnp.where(r <= length, jnp.cos(ang), -jnp.sin(ang))
    w = jnp.where((r == 0) | (r == length), 1.0, 2.0) / n
    return f.astype(BF16), (f * w).T.astype(BF16)


def _hy_embedding(length):
    t = np.linspace(0.0, 1.0, length)[:, None]
    bands = (HY_EMB - 1) // 2
    wpos = 2.0 * math.pi * np.arange(length)[:, None] / length
    f = np.linspace(1e-4, bands - 1, bands)[None]
    z = np.concatenate([t, np.cos(f * wpos), -np.sin(f * wpos)], axis=-1)
    z = np.pad(z, ((0, 0), (0, HY_FFN - HY_EMB)))
    deltas = np.abs(np.linspace(math.log(HY_TARGET) / HY_SLOW, math.log(HY_TARGET) / HY_FAST, BR_W))[None]
    return z.astype(np.float32), deltas.astype(np.float32)


def _hy_filter_kernel(z_ref, win_ref, bin_ref, wmid_ref, bmid_ref, freq_ref, wout_ref, delta_ref, o_ref):
    hp = lax.Precision.HIGHEST
    length = z_ref.shape[0]
    h = jnp.sin(freq_ref[0] * (jnp.dot(z_ref[...], win_ref[...], precision=hp, preferred_element_type=F32)
                               + bin_ref[...]))
    for i in range(HY_INNER):
        h = jnp.sin(freq_ref[i + 1] * (jnp.dot(h, wmid_ref[i], precision=hp, preferred_element_type=F32)
                                       + bmid_ref[i]))
    h = jnp.dot(h, wout_ref[...], precision=hp, preferred_element_type=F32)
    t = lax.broadcasted_iota(jnp.int32, (length, BR_W), 0).astype(F32) * (1.0 / (length - 1))
    decay = jnp.exp(-t * delta_ref[...])
    h_f = h[:, :BR_W] * decay
    h_b = h[:, BR_W:] * decay
    for i, a in enumerate((h_f + h_b, h_f - h_b)):
        hi = a.astype(BF16)
        lo = (a - hi.astype(F32)).astype(BF16)
        o_ref[:, (2 * i) * BR_W:(2 * i + 1) * BR_W] = hi
        o_ref[:, (2 * i + 1) * BR_W:(2 * i + 2) * BR_W] = lo


def _hy_spectrum_kernel(f_ref, a_ref, ks_ref, kd_ref, *, length):
    i = pl.program_id(0)
    tm = f_ref.shape[0]
    r = jnp.dot(f_ref[...], a_ref[...], preferred_element_type=F32)
    ks_ref[...] = r[:, :BR_W] + r[:, BR_W:2 * BR_W]
    kd = r[:, 2 * BR_W:3 * BR_W] + r[:, 3 * BR_W:]
    row = lax.broadcasted_iota(jnp.int32, kd.shape, 0) + i * tm
    kd_ref[...] = jnp.where(row == length, 0.0, kd)


def hyena_filter_spectrum(length, fwd, filt):
    w_in, b_in, w_mid, b_mid, freq, w_out = filt
    z, deltas = _hy_embedding(length)
    n = 2 * length
    parts = pl.pallas_call(
        _hy_filter_kernel,
        out_shape=jax.ShapeDtypeStruct((length, 4 * BR_W), BF16),
        compiler_params=pltpu.CompilerParams(vmem_limit_bytes=VMEM_LIMIT),
        name="hyena_filter",
    )(jnp.asarray(z), jnp.pad(w_in, ((0, HY_FFN - HY_EMB), (0, 0))), b_in.reshape(1, HY_FFN), w_mid,
      b_mid.reshape(HY_INNER, 1, HY_FFN), freq.reshape(HY_INNER + 1, 1, HY_FFN), w_out, jnp.asarray(deltas))
    tm = min(HY_ROWS, n)
    return pl.pallas_call(
        functools.partial(_hy_spectrum_kernel, length=length),
        grid=(n // tm,),
        in_specs=[pl.BlockSpec((tm, length), lambda i: (i, 0)),
                  pl.BlockSpec((length, 4 * BR_W), lambda i: (0, 0))],
        out_specs=[pl.BlockSpec((tm, BR_W), lambda i: (i, 0)), pl.BlockSpec((tm, BR_W), lambda i: (i, 0))],
        out_shape=[jax.ShapeDtypeStruct((n, BR_W), F32), jax.ShapeDtypeStruct((n, BR_W), F32)],
        compiler_params=pltpu.CompilerParams(vmem_limit_bytes=VMEM_LIMIT),
        name="hyena_spectrum",
    )(fwd, parts)


def _shift_rows(x, off):
    t = x.shape[0]
    rolled = pltpu.roll(x, (-off) % t, axis=0)
    row = lax.broadcasted_iota(jnp.int32, x.shape, 0)
    valid = (row >= -off) if off < 0 else (row < t - off)
    return jnp.where(valid, rolled, 0.0)


def _dwconv_rows(x, w_ref, k):
    out = x * w_ref[k // 2:k // 2 + 1, :]
    for j in range(k):
        if j != k // 2:
            out = out + _shift_rows(x, j - k // 2) * w_ref[j:j + 1, :]
    return out


def _hy_prep_kernel(x0_ref, x1_ref, v_ref, z_ref, w0_ref, w1_ref, wv_ref, b0_ref, b1_ref, bv_ref, vx_ref, x0g_ref):
    x0 = _dwconv_rows(x0_ref[0], w0_ref, HY_CONV) + b0_ref[...]
    x1 = _dwconv_rows(x1_ref[0], w1_ref, HY_CONV) + b1_ref[...]
    v = _dwconv_rows(v_ref[0], wv_ref, HY_CONV) + bv_ref[...]
    z = z_ref[0]
    vx_ref[0] = v * x1
    x0g_ref[0] = x0 * (z * jax.nn.sigmoid(z))


def hyena_prep(p_main, conv_w, conv_b):
    b, length, _ = p_main.shape
    cb = LANE
    nj = BR_W // cb
    o_xv = MY_OFF["hy_xv"][0] // cb
    o_z = MY_OFF["hy_z"][0] // cb
    wt = conv_w.T
    bt = conv_b.reshape(1, 3 * BR_W)
    dspec = lambda o: pl.BlockSpec((1, length, cb), lambda i, j: (i, 0, o + j))
    wspec = lambda o: pl.BlockSpec((HY_CONV, cb), lambda i, j: (0, o + j))
    bspec = lambda o: pl.BlockSpec((1, cb), lambda i, j: (0, o + j))
    ospec = pl.BlockSpec((1, length, cb), lambda i, j: (i, 0, j))
    return pl.pallas_call(
        _hy_prep_kernel,
        grid=(b, nj),
        in_specs=[dspec(o_xv), dspec(o_xv + nj), dspec(o_xv + 2 * nj), dspec(o_z),
                  wspec(0), wspec(nj), wspec(2 * nj), bspec(0), bspec(nj), bspec(2 * nj)],
        out_specs=[ospec, ospec],
        out_shape=[jax.ShapeDtypeStruct((b, length, BR_W), F32)] * 2,
        compiler_params=pltpu.CompilerParams(vmem_limit_bytes=VMEM_LIMIT),
        name="hyena_prep",
    )(p_main, p_main, p_main, p_main, wt, wt, wt, bt, bt, bt)


def _hy_fwd_kernel(vx_ref, f_ref, kr_ref, ki_ref, knyq_ref, y_ref, *, length):
    v = vx_ref[0].astype(BF16)
    tm = min(HY_ROWS, length)
    for kt in range(length // tm):
        lo = kt * tm
        a = jnp.dot(f_ref[lo:lo + tm, :], v, preferred_element_type=F32)
        bh = jnp.dot(f_ref[length + lo:length + lo + tm, :], v, preferred_element_type=F32)
        kr = kr_ref[lo:lo + tm, :]
        ki = ki_ref[lo:lo + tm, :]
        if kt == 0:
            row = lax.broadcasted_iota(jnp.int32, kr.shape, 0)
            kb = jnp.where(row == 0, knyq_ref[0:1, :], kr)
        else:
            kb = kr
        y_ref[0, lo:lo + tm, :] = (a * kr - bh * ki).astype(BF16)
        y_ref[0, length + lo:length + lo + tm, :] = (a * ki + bh * kb).astype(BF16)


def _hy_inv_kernel(y_ref, inv_ref, vx_ref, x0g_ref, skip_ref, o_ref, *, length):
    tm = min(HY_ROWS, length)
    y = y_ref[0]
    for tt in range(length // tm):
        lo = tt * tm
        conv = jnp.dot(inv_ref[lo:lo + tm, :], y, preferred_element_type=F32)
        o_ref[0, lo:lo + tm, :] = ((conv + skip_ref[...] * vx_ref[0, lo:lo + tm, :])
                                   * x0g_ref[0, lo:lo + tm, :]).astype(o_ref.dtype)


def hyena_branch(p_main, conv_w, conv_b, skip, fwd, inv, ks, kd):
    b, length, _ = p_main.shape
    n = 2 * length
    vx, x0g = hyena_prep(p_main, conv_w, conv_b)
    cb = HY_COLS
    nj = BR_W // cb
    resident = lambda shape: pl.BlockSpec(shape, lambda j, i: (0, 0), pipeline_mode=pl.Buffered(1))
    y = pl.pallas_call(
        functools.partial(_hy_fwd_kernel, length=length),
        grid=(nj, b),
        in_specs=[pl.BlockSpec((1, length, cb), lambda j, i: (i, 0, j)),
                  resident((n, length)),
                  pl.BlockSpec((length, cb), lambda j, i: (0, j)),
                  pl.BlockSpec((length, cb), lambda j, i: (1, j)),
                  pl.BlockSpec((8, cb), lambda j, i: (length // 8, j))],
        out_specs=pl.BlockSpec((1, n, cb), lambda j, i: (i, 0, j)),
        out_shape=jax.ShapeDtypeStruct((b, n, BR_W), BF16),
        compiler_params=pltpu.CompilerParams(vmem_limit_bytes=VMEM_LIMIT),
        name="hyena_dft",
    )(vx, fwd, ks, kd, ks)
    return pl.pallas_call(
        functools.partial(_hy_inv_kernel, length=length),
        grid=(nj, b),
        in_specs=[pl.BlockSpec((1, n, cb), lambda j, i: (i, 0, j)),
                  resident((length, n)),
                  pl.BlockSpec((1, length, cb), lambda j, i: (i, 0, j)),
                  pl.BlockSpec((1, length, cb), lambda j, i: (i, 0, j)),
                  pl.BlockSpec((1, cb), lambda j, i: (0, j))],
        out_specs=pl.BlockSpec((1, length, cb), lambda j, i: (i, 0, j)),
        out_shape=jax.ShapeDtypeStruct((b, length, BR_W), BF16),
        compiler_params=pltpu.CompilerParams(vmem_limit_bytes=VMEM_LIMIT),
        name="hyena_idft",
    )(y, inv, vx, x0g, skip.reshape(1, BR_W))


def _split_cols(p_main, p_small):
    out = {}
    for name in MAIN_ORDER:
        o, w = MY_OFF[name]
        out[name] = p_main[..., o:o + w]
    for name in SMALL_ORDER:
        o, w = MY_OFF[name]
        out[name] = p_small[..., o:o + w]
    return out


def mixers(p_main, p_small, pc_main, pc_small, na_rpb, gla_wg2, gla_bg, gla_norm, gdn_conv, gdn_a_log, gdn_dt_bias,
           gdn_norm, hy_conv, hy_conv_b, hy_filt, hy_skip, with_ctx, dft_lat, dft_ctx):
    pl_ = _split_cols(p_main, p_small)
    pc_ = _split_cols(pc_main, pc_small)
    b, t, _ = pl_["na_z"].shape
    ya = na_branch(p_main, pc_main, na_rpb)

    wg2, bg = gla_wg2.astype(F32), gla_bg.astype(F32)
    s0_b = jnp.zeros((b, GLA_HEADS, GLA_DK, GLA_DV), F32)
    ob_c, ob = bidir_scan(gla_scan, gla_inputs(pc_, wg2, bg, None), gla_inputs(pl_, wg2, bg, axial_rope(t)), s0_b)

    a_log, dtb = gdn_a_log.astype(F32), gdn_dt_bias.astype(F32)
    s0_c = jnp.zeros((b, GDN_HEADS, GDN_DH, GDN_DH), F32)
    oc_c, oc = bidir_scan(gdn_scan, gdn_inputs(pc_, gdn_conv, a_log, dtb), gdn_inputs(pl_, gdn_conv, a_log, dtb), s0_c)

    fwd, inv = dft_lat
    ks, kd = hyena_filter_spectrum(t, fwd, hy_filt)
    yd = hyena_branch(p_main, hy_conv, hy_conv_b, hy_skip, fwd, inv, ks, kd)

    def finish(p, ob_, oc_):
        return ((from_heads(rmsnorm(ob_, gla_norm)) * jax.nn.silu(p["gla_z"])).astype(BF16),
                (from_heads(rmsnorm(oc_, gdn_norm)) * jax.nn.silu(p["gdn_z"])).astype(BF16))

    yb, yc = finish(pl_, ob, oc)
    if not with_ctx:
        return (ya, yb, yc, yd), None
    ya_c = ctx_attention_branch(pc_main)
    fwd_c, inv_c = dft_ctx
    ks_c, kd_c = hyena_filter_spectrum(pc_main.shape[1], fwd_c, hy_filt)
    yd_c = hyena_branch(pc_main, hy_conv, hy_conv_b, hy_skip, fwd_c, inv_c, ks_c, kd_c)
    yb_c, yc_c = finish(pc_, ob_c, oc_c)
    return (ya, yb, yc, yd), (ya_c, yb_c, yc_c, yd_c)


def kernel(x, c, ctx, c_ctx, w_mod, b_mod, g_pre, g_post, w_in, w_gate, b_gate, w_branch, w_out, na_rpb, gla_wg2, gla_bg, gla_norm, gdn_conv, gdn_a_log, gdn_dt_bias, gdn_norm, hy_conv, hy_conv_b, hy_w_in, hy_b_in, hy_w_mid, hy_b_mid, hy_freq, hy_w_out, hy_skip):
    b, t, d = x.shape
    tc = ctx.shape[1]
    h = x.reshape(b * t, d)
    hc = ctx.reshape(b * tc, d)
    c_all = jnp.concatenate([c, c_ctx[None], jnp.zeros((7, d), F32)], axis=0)
    dft_lat = dft_matrices(t)
    dft_ctx = dft_matrices(tc)
    for l in range(DEPTH):
        with_ctx = l < DEPTH - 1
        mod = modulation(c_all, w_mod[l], b_mod[l])
        shift, scale, gate = [mod[:, i * d:(i + 1) * d].reshape(-1, 1, d) for i in range(3)]
        u = prenorm_mod(h, g_pre[l], scale[:b], shift[:b], t)
        uc = prenorm_mod(hc, g_pre[l], scale[b:b + 1], shift[b:b + 1], b * tc)
        w_main, w_small = _permute_w_in(w_in[l])
        p_main = matmul(u, w_main, 1024, 1536, name="in_proj").reshape(b, t, N_MAIN)
        p_small = matmul(u, w_small, 1024, N_SMALL, name="in_proj_small").reshape(b, t, N_SMALL)
        pc_main = matmul(uc, w_main, 1024, 1536, name="in_proj_ctx").reshape(b, tc, N_MAIN)
        pc_small = matmul(uc, w_small, 1024, N_SMALL, name="in_proj_ctx_small").reshape(b, tc, N_SMALL)
        ys, ycs = mixers(p_main, p_small, pc_main, pc_small,
                         na_rpb[l], gla_wg2[l], gla_bg[l], gla_norm[l], gdn_conv[l], gdn_a_log[l], gdn_dt_bias[l],
                         gdn_norm[l], hy_conv[l], hy_conv_b[l],
                         (hy_w_in[l], hy_b_in[l], hy_w_mid[l], hy_b_mid[l], hy_freq[l], hy_w_out[l]),
                         hy_skip[l], with_ctx, dft_lat, dft_ctx)
        h_new = merge(u, ys, w_gate[l], b_gate[l], w_branch[l], w_out[l], h, gate[:b], g_post[l], t)
        if with_ctx:
            hc = merge(uc, ycs, w_gate[l], b_gate[l], w_branch[l], w_out[l], hc, gate[b:b + 1], g_post[l], b * tc)
        h = h_new
    return h.reshape(b, t, d)
```

```python
import functools
import math

import numpy as np
import jax
import jax.numpy as jnp
from jax import lax
from jax.experimental import pallas as pl
from jax.experimental.pallas import tpu as pltpu

D_MODEL = 2048
BATCH = 16
SEQ = 2048
DEPTH = 2
GRID_W = 64
CTX_LEN = 256
N_BRANCH = 4
BR_W = D_MODEL // N_BRANCH
NA_HEADS = 8
NA_DH = BR_W // NA_HEADS
WIN_R = 8
WIN_C = 16
QBLK_C = 16
KBLK_C = WIN_C + QBLK_C
N_CBLK = GRID_W // QBLK_C
GLA_HEADS = 4
GLA_DV = BR_W // GLA_HEADS
GLA_DK = GLA_DV // 2
GLA_RANK = 16
GLA_TAU = 16.0
CHUNK = 64
GDN_HEADS = 4
GDN_DH = BR_W // GDN_HEADS
GDN_CONV = 5
HY_CONV = 3
HY_EMB = 33
HY_FFN = 64
HY_INNER = 2
HY_TARGET = 1e-2
HY_FAST = 0.3
HY_SLOW = 1.5
ROPE_THETA = 10000.0
EPS = 1e-6
F32 = jnp.float32
BF16 = jnp.bfloat16

REF_SPLITS = (
    ("na_qkv", 3 * BR_W), ("na_z", BR_W),
    ("gla_q", GLA_HEADS * GLA_DK), ("gla_k", GLA_HEADS * GLA_DK), ("gla_v", BR_W), ("gla_z", BR_W),
    ("gla_g", 2 * GLA_RANK),
    ("gdn_qkv", 3 * BR_W), ("gdn_z", BR_W), ("gdn_a", 2 * GDN_HEADS), ("gdn_b", 2 * GDN_HEADS),
    ("hy_xv", 3 * BR_W), ("hy_z", BR_W),
)
MAIN_ORDER = ("na_qkv", "na_z", "gla_q", "gla_k", "gla_v", "gla_z", "gdn_qkv", "gdn_z", "hy_xv", "hy_z")
SMALL_ORDER = ("gla_g", "gdn_a", "gdn_b")
LANE = 128
N_MAIN = sum(w for n, w in REF_SPLITS if n in MAIN_ORDER)
N_SMALL = LANE
VMEM_LIMIT = 48 * 1024 * 1024


def _ref_offsets():
    out, o = {}, 0
    for name, w in REF_SPLITS:
        out[name] = (o, w)
        o += w
    return out


def _my_offsets():
    out, o = {}, 0
    for name in MAIN_ORDER:
        w = dict(REF_SPLITS)[name]
        out[name] = (o, w)
        o += w
    o = 0
    for name in SMALL_ORDER:
        w = dict(REF_SPLITS)[name]
        out[name] = (o, w)
        o += w
    return out


REF_OFF = _ref_offsets()
MY_OFF = _my_offsets()


def _permute_w_in(w_in_l):
    main = jnp.concatenate([w_in_l[:, REF_OFF[n][0]:REF_OFF[n][0] + REF_OFF[n][1]] for n in MAIN_ORDER], axis=1)
    small = jnp.concatenate([w_in_l[:, REF_OFF[n][0]:REF_OFF[n][0] + REF_OFF[n][1]] for n in SMALL_ORDER], axis=1)
    small = jnp.pad(small, ((0, 0), (0, N_SMALL - small.shape[1])))
    return main.astype(BF16), small.astype(BF16)


def _mod_kernel(c_ref, w_ref, b_ref, o_ref):
    c = c_ref[...]
    a = (c * jax.nn.sigmoid(c)).astype(BF16)
    o_ref[...] = jnp.dot(a, w_ref[...].astype(BF16), preferred_element_type=F32) + b_ref[...]


def modulation(c_all, w_mod_l, b_mod_l):
    r, d = c_all.shape
    n = w_mod_l.shape[1]
    tn = 768
    return pl.pallas_call(
        _mod_kernel,
        grid=(n // tn,),
        in_specs=[pl.BlockSpec((r, d), lambda j: (0, 0)),
                  pl.BlockSpec((d, tn), lambda j: (0, j)),
                  pl.BlockSpec((1, tn), lambda j: (0, j))],
        out_specs=pl.BlockSpec((r, tn), lambda j: (0, j)),
        out_shape=jax.ShapeDtypeStruct((r, n), F32),
        compiler_params=pltpu.CompilerParams(vmem_limit_bytes=VMEM_LIMIT),
        name="modulation",
    )(c_all, w_mod_l, b_mod_l.reshape(1, n))


def _prenorm_kernel(x_ref, g_ref, scale_ref, shift_ref, o_ref):
    x = x_ref[...]
    y = x * lax.rsqrt(jnp.mean(x * x, axis=-1, keepdims=True) + EPS)
    u = (y * g_ref[...]) * (1.0 + scale_ref[0]) + shift_ref[0]
    o_ref[...] = u.astype(o_ref.dtype)


def prenorm_mod(x2d, g_pre_l, scale, shift, rows_per_group, out_dtype=BF16):
    n, d = x2d.shape
    tm = 512
    tiles_per_group = rows_per_group // tm
    return pl.pallas_call(
        _prenorm_kernel,
        grid=(n // tm,),
        in_specs=[pl.BlockSpec((tm, d), lambda i: (i, 0)),
                  pl.BlockSpec((1, d), lambda i: (0, 0)),
                  pl.BlockSpec((1, 1, d), lambda i: (i // tiles_per_group, 0, 0)),
                  pl.BlockSpec((1, 1, d), lambda i: (i // tiles_per_group, 0, 0))],
        out_specs=pl.BlockSpec((tm, d), lambda i: (i, 0)),
        out_shape=jax.ShapeDtypeStruct((n, d), out_dtype),
        compiler_params=pltpu.CompilerParams(vmem_limit_bytes=VMEM_LIMIT),
        name="prenorm_mod",
    )(x2d, g_pre_l.reshape(1, d), scale, shift)


def _matmul_kernel(u_ref, w_ref, o_ref):
    o_ref[...] = jnp.dot(u_ref[...], w_ref[...], preferred_element_type=F32).astype(o_ref.dtype)


def matmul(u, w, tm, tn, out_dtype=F32, name="matmul"):
    n, k = u.shape
    m = w.shape[1]
    return pl.pallas_call(
        _matmul_kernel,
        grid=(m // tn, n // tm),
        in_specs=[pl.BlockSpec((tm, k), lambda j, i: (i, 0)),
                  pl.BlockSpec((k, tn), lambda j, i: (0, j))],
        out_specs=pl.BlockSpec((tm, tn), lambda j, i: (i, j)),
        out_shape=jax.ShapeDtypeStruct((n, m), out_dtype),
        compiler_params=pltpu.CompilerParams(vmem_limit_bytes=VMEM_LIMIT),
        name=name,
    )(u, w)


def _merge_kernel(u_ref, y0_ref, y1_ref, y2_ref, y3_ref, wg_ref, bg_ref, wb_ref, wo_ref, h_ref, gate_ref, gpost_ref,
                  o_ref, acc_ref):
    n = pl.program_id(1)
    u = u_ref[...]
    merged = None
    for i, y_ref in enumerate((y0_ref, y1_ref, y2_ref, y3_ref)):
        g = jnp.dot(u, wg_ref[i], preferred_element_type=F32) + bg_ref[i]
        yb = jnp.dot(y_ref[...], wb_ref[i], preferred_element_type=F32)
        term = jax.nn.sigmoid(g) * yb
        merged = term if merged is None else merged + term
    part = jnp.dot(merged.astype(BF16), wo_ref[...], preferred_element_type=F32)

    @pl.when(n == 0)
    def _():
        acc_ref[...] = part

    @pl.when(n > 0)
    def _():
        acc_ref[...] += part

    @pl.when(n == pl.num_programs(1) - 1)
    def _():
        y = acc_ref[...]
        yn = y * lax.rsqrt(jnp.mean(y * y, axis=-1, keepdims=True) + EPS) * gpost_ref[...]
        o_ref[...] = h_ref[...] + gate_ref[0] * yn


def merge(u, ys, w_gate_l, b_gate_l, w_branch_l, w_out_l, h2d, gate, g_post_l, rows_per_group):
    n, d = u.shape
    tm, tn = 512, 256
    tiles_per_group = rows_per_group // tm
    yspec = pl.BlockSpec((tm, BR_W), lambda i, j: (i, 0))
    return pl.pallas_call(
        _merge_kernel,
        grid=(n // tm, d // tn),
        in_specs=[pl.BlockSpec((tm, d), lambda i, j: (i, 0)),
                  yspec, yspec, yspec, yspec,
                  pl.BlockSpec((N_BRANCH, d, tn), lambda i, j: (0, 0, j)),
                  pl.BlockSpec((N_BRANCH, 1, tn), lambda i, j: (0, 0, j)),
                  pl.BlockSpec((N_BRANCH, BR_W, tn), lambda i, j: (0, 0, j)),
                  pl.BlockSpec((tn, d), lambda i, j: (j, 0)),
                  pl.BlockSpec((tm, d), lambda i, j: (i, 0)),
                  pl.BlockSpec((1, 1, d), lambda i, j: (i // tiles_per_group, 0, 0)),
                  pl.BlockSpec((1, d), lambda i, j: (0, 0))],
        out_specs=pl.BlockSpec((tm, d), lambda i, j: (i, 0)),
        out_shape=jax.ShapeDtypeStruct((n, d), F32),
        scratch_shapes=[pltpu.VMEM((tm, d), F32)],
        compiler_params=pltpu.CompilerParams(vmem_limit_bytes=VMEM_LIMIT),
        name="merge",
    )(u, *[y.reshape(n, BR_W) for y in ys], w_gate_l.astype(BF16), b_gate_l.reshape(N_BRANCH, 1, d),
      w_branch_l.astype(BF16),
      w_out_l.astype(BF16), h2d, gate, g_post_l.reshape(1, d))


NA_NEG = -1e30
NA_PAIR = LANE // NA_DH


def na_bias_table(rpb, rows):
    assert rows >= WIN_R
    heads = rpb.shape[0]
    c = np.arange(GRID_W)[:, None]
    kc = np.arange(GRID_W)[None, :]
    cstart = np.clip(c - WIN_C // 2, 0, GRID_W - WIN_C)
    valid = (kc >= cstart) & (kc < cstart + WIN_C)
    dc = np.clip(kc - c + WIN_C - 1, 0, 2 * WIN_C - 2)
    onehot = ((dc[None] == np.arange(2 * WIN_C - 1)[:, None, None]) & valid[None]).astype(np.float32)
    by_dr = jnp.einsum("hrd,dck->hrck", rpb.astype(F32), jnp.asarray(onehot), precision=lax.Precision.HIGHEST)
    by_dr = jnp.where(valid, by_dr, NA_NEG)
    tabs = [by_dr[:, WIN_R - 1 - off:2 * WIN_R - 1 - off].transpose(0, 2, 1, 3).reshape(heads, GRID_W, WIN_R * GRID_W)
            for off in range(WIN_R)]
    return jnp.stack(tabs, axis=1)


def _nt_dot(a, b):
    return lax.dot_general(a, b, (((1,), (1,)), ((), ())), preferred_element_type=F32)


def _na_kernel(q_ref, k_ref, v_ref, kc_ref, vc_ref, z_ref, bias_ref, o_ref, *, rows):
    lane = lax.broadcasted_iota(jnp.int32, (GRID_W, LANE), 1)
    head_mask = [(lane >= hh * NA_DH) & (lane < (hh + 1) * NA_DH) for hh in range(NA_PAIR)]
    kc = kc_ref[0].astype(BF16)
    vc = vc_ref[0].astype(BF16)
    win = WIN_R * GRID_W

    def body(r, carry):
        rs = jnp.clip(r - WIN_R // 2, 0, rows - WIN_R)
        off = r - rs
        q0 = pl.multiple_of(r * GRID_W, GRID_W)
        k0 = pl.multiple_of(rs * GRID_W, GRID_W)
        q = q_ref[0, pl.ds(q0, GRID_W), :] * (NA_DH ** -0.5)
        kw = k_ref[0, pl.ds(k0, win), :].astype(BF16)
        vw = v_ref[0, pl.ds(k0, win), :].astype(BF16)
        out = None
        for hh in range(NA_PAIR):
            qh = jnp.where(head_mask[hh], q, 0.0).astype(BF16)
            s = _nt_dot(qh, kw) + bias_ref[hh, off]
            sc = _nt_dot(qh, kc)
            m = jnp.maximum(jnp.max(s, axis=-1, keepdims=True), jnp.max(sc, axis=-1, keepdims=True))
            p = jnp.exp(s - m)
            pc = jnp.exp(sc - m)
            l = jnp.sum(p, axis=-1, keepdims=True) + jnp.sum(pc, axis=-1, keepdims=True)
            o = (jnp.dot(p.astype(BF16), vw, preferred_element_type=F32)
                 + jnp.dot(pc.astype(BF16), vc, preferred_element_type=F32)) / l
            out = o if out is None else jnp.where(head_mask[hh], o, out)
        z = z_ref[0, pl.ds(q0, GRID_W), :]
        o_ref[0, pl.ds(q0, GRID_W), :] = (out * (z * jax.nn.sigmoid(z))).astype(o_ref.dtype)
        return carry

    lax.fori_loop(0, rows, body, 0)


def na_branch(p_main, pc_main, rpb):
    b, t, _ = p_main.shape
    tc = pc_main.shape[1]
    rows = t // GRID_W
    nj = BR_W // LANE
    o_q = MY_OFF["na_qkv"][0] // LANE
    o_z = MY_OFF["na_z"][0] // LANE
    bias = na_bias_table(rpb, rows)
    lat = lambda o: pl.BlockSpec((1, t, LANE), lambda i, j: (i, 0, o + j))
    cx = lambda o: pl.BlockSpec((1, tc, LANE), lambda i, j: (i, 0, o + j))
    return pl.pallas_call(
        functools.partial(_na_kernel, rows=rows),
        grid=(b, nj),
        in_specs=[lat(o_q), lat(o_q + nj), lat(o_q + 2 * nj), cx(o_q + nj), cx(o_q + 2 * nj), lat(o_z),
                  pl.BlockSpec((NA_PAIR, WIN_R, GRID_W, WIN_R * GRID_W), lambda i, j: (j, 0, 0, 0))],
        out_specs=pl.BlockSpec((1, t, LANE), lambda i, j: (i, 0, j)),
        out_shape=jax.ShapeDtypeStruct((b, t, BR_W), BF16),
        compiler_params=pltpu.CompilerParams(vmem_limit_bytes=VMEM_LIMIT),
        name="na_attention",
    )(p_main, p_main, p_main, pc_main, pc_main, p_main, bias)


def _ctx_attn_kernel(q_ref, k_ref, v_ref, z_ref, o_ref):
    tc = q_ref.shape[1]
    lane = lax.broadcasted_iota(jnp.int32, (tc, LANE), 1)
    q = q_ref[0] * (NA_DH ** -0.5)
    k = k_ref[0].astype(BF16)
    v = v_ref[0].astype(BF16)
    out = None
    for hh in range(NA_PAIR):
        mask = (lane >= hh * NA_DH) & (lane < (hh + 1) * NA_DH)
        s = _nt_dot(jnp.where(mask, q, 0.0).astype(BF16), k)
        p = jnp.exp(s - jnp.max(s, axis=-1, keepdims=True))
        o = jnp.dot(p.astype(BF16), v, preferred_element_type=F32) / jnp.sum(p, axis=-1, keepdims=True)
        out = o if out is None else jnp.where(mask, o, out)
    z = z_ref[0]
    o_ref[0] = (out * (z * jax.nn.sigmoid(z))).astype(o_ref.dtype)


def ctx_attention_branch(pc_main):
    b, tc, _ = pc_main.shape
    nj = BR_W // LANE
    o_q = MY_OFF["na_qkv"][0] // LANE
    o_z = MY_OFF["na_z"][0] // LANE
    cx = lambda o: pl.BlockSpec((1, tc, LANE), lambda i, j: (i, 0, o + j))
    return pl.pallas_call(
        _ctx_attn_kernel,
        grid=(b, nj),
        in_specs=[cx(o_q), cx(o_q + nj), cx(o_q + 2 * nj), cx(o_z)],
        out_specs=pl.BlockSpec((1, tc, LANE), lambda i, j: (i, 0, j)),
        out_shape=jax.ShapeDtypeStruct((b, tc, BR_W), BF16),
        compiler_params=pltpu.CompilerParams(vmem_limit_bytes=VMEM_LIMIT),
        name="ctx_attention",
    )(pc_main, pc_main, pc_main, pc_main)


def _split3(x):
    x1 = x.astype(BF16)
    r1 = x - x1.astype(F32)
    x2 = r1.astype(BF16)
    x3 = (r1 - x2.astype(F32)).astype(BF16)
    return x1, x2, x3


def _sum_rows(m_bf16, x):
    return sum(jnp.dot(m_bf16, p, preferred_element_type=F32) for p in _split3(x))


def _tn_dot(a, b):
    return lax.dot_general(a, b, (((0,), (0,)), ((), ())), preferred_element_type=F32)


def _softplus(x):
    return jnp.maximum(x, 0.0) + jnp.log1p(jnp.exp(-jnp.abs(x)))


def _rms_rows(o, g):
    return o * lax.rsqrt(jnp.mean(o * o, axis=-1, keepdims=True) + EPS) * g


GLA_LEVELS = 6
GLA_PAIR = LANE // GLA_DK
GLA_SUM_ROWS = (2 + 2 * GLA_LEVELS) * CHUNK


def _gla_tables():
    assert CHUNK == 1 << GLA_LEVELS
    t = np.arange(CHUNK)[:, None]
    r = np.arange(CHUNK)[None, :]
    sums, masks = [], []
    for rev in (False, True):
        before = (r >= t) if rev else (r <= t)
        after = (r < t) if rev else (r > t)
        blocks = [before, after]
        lev_masks = []
        for lv in range(1, GLA_LEVELS + 1):
            seg = CHUNK >> lv
            st, sr = t // seg, r // seg
            q_half = (st % 2 == 0) if rev else (st % 2 == 1)
            k_half = ~q_half
            blocks.append(before & (st == sr) & q_half)
            blocks.append(after & (st == sr) & k_half)
            lev_masks.append(q_half & (sr == (st + 1 if rev else st - 1)))
        lev_masks.append(r == t)
        sums.append(np.concatenate(blocks, axis=0))
        masks.append(np.stack([np.concatenate([m, m], axis=0) for m in lev_masks]))
    return np.stack(sums).astype(np.float32), np.stack(masks).astype(np.float32)


def _gla_rope_tables(t_len):
    pos = np.arange(t_len)
    row = (pos // GRID_W).astype(np.float64)
    col = (pos % GRID_W).astype(np.float64)
    n = GLA_DK // 4
    freqs = ROPE_THETA ** (-np.arange(n, dtype=np.float64) / n)
    ang = np.concatenate([row[:, None] * freqs, col[:, None] * freqs], axis=-1)
    cos = np.concatenate([np.cos(ang), np.cos(ang)] * GLA_PAIR, axis=-1)
    sin = np.concatenate([-np.sin(ang), np.sin(ang)] * GLA_PAIR, axis=-1)
    return cos.astype(np.float32), sin.astype(np.float32)


def _gla_gate_weights(wg2, bg):
    half = GLA_DK // 2
    lane = np.arange(LANE)
    w_all, b_all = [], []
    for d in range(2):
        wd, bd = [], []
        for j in range(GLA_HEADS // GLA_PAIR):
            cols = half * (GLA_PAIR * j + lane // GLA_DK) + lane % half
            w = jnp.zeros((N_SMALL, LANE), F32).at[MY_OFF["gla_g"][0] + d * GLA_RANK:
                                                    MY_OFF["gla_g"][0] + (d + 1) * GLA_RANK].set(wg2[d][:, cols])
            wd.append(w)
            bd.append(bg[d][cols].reshape(1, LANE))
        w_all.append(jnp.stack(wd))
        b_all.append(jnp.stack(bd))
    return jnp.stack(w_all), jnp.stack(b_all)


def _gla_chunk(c, rev, q_ref, k_ref, v_ref, gs_ref, rope, wg_ref, bgate_ref, sums_ref, masks_ref, st_ref, acc_ref):
    d = 1 if rev else 0
    hp = lax.Precision.HIGHEST
    r0 = pl.multiple_of(c * CHUNK, CHUNK)
    rows = pl.ds(r0, CHUNK)
    lane = lax.broadcasted_iota(jnp.int32, (CHUNK, LANE), 1)
    q = q_ref[0, rows, :] * (GLA_DK ** -0.5)
    k = k_ref[0, rows, :]
    v = v_ref[0, rows, :]
    if rope is not None:
        cos = rope[0][rows, :]
        sin = rope[1][rows, :]
        first_half = (lane % GLA_DK) < GLA_DK // 2

        def rot(x):
            partner = jnp.where(first_half, pltpu.roll(x, LANE - GLA_DK // 2, axis=1), pltpu.roll(x, GLA_DK // 2, axis=1))
            return x * cos + partner * sin
        q, k = rot(q), rot(k)
    x = jnp.dot(gs_ref[0, rows, :], wg_ref[d, 0], precision=hp, preferred_element_type=F32) + bgate_ref[d, 0]
    g = (jnp.minimum(x, 0.0) - jnp.log1p(jnp.exp(-jnp.abs(x)))) * (1.0 / GLA_TAU)
    sums = _sum_rows(sums_ref[d], g)
    run = sums[0:CHUNK]
    total = run[0:1] if rev else run[CHUNK - 1:CHUNK]
    q_run = (q * jnp.exp(run)).astype(BF16)
    k_end = (k * jnp.exp(sums[CHUNK:2 * CHUNK])).astype(BF16)

    def stack_heads(a):
        return jnp.concatenate([jnp.where((lane >= hh * GLA_DK) & (lane < (hh + 1) * GLA_DK), a, 0.0)
                                for hh in range(GLA_PAIR)], axis=0).astype(BF16)

    att = _nt_dot(stack_heads(q), k.astype(BF16)) * masks_ref[d, GLA_LEVELS]
    for lv in range(GLA_LEVELS):
        o = (2 + 2 * lv) * CHUNK
        ql = q * jnp.exp(sums[o:o + CHUNK])
        kl = k * jnp.exp(sums[o + CHUNK:o + 2 * CHUNK])
        att = att + _nt_dot(stack_heads(ql), kl.astype(BF16)) * masks_ref[d, lv]
    av = jnp.dot(att.astype(BF16), v.astype(BF16), preferred_element_type=F32)
    intra = jnp.concatenate([av[hh * CHUNK:(hh + 1) * CHUNK, hh * GLA_DV:(hh + 1) * GLA_DV]
                             for hh in range(GLA_PAIR)], axis=1)
    st = st_ref[d]
    inter = _nt_dot(q_run, st.astype(BF16))
    acc_ref[rows, :] += intra + inter
    upd = _tn_dot(v.astype(BF16), k_end)
    srow = lax.broadcasted_iota(jnp.int32, upd.shape, 0) // GLA_DV
    scol = lax.broadcasted_iota(jnp.int32, upd.shape, 1) // GLA_DK
    st_ref[d] = jnp.where(srow == scol, st * jnp.exp(total) + upd, 0.0)


def _gla_finish(acc_ref, z_ref, norm_ref, o_ref, n_rows):
    tile = min(256, n_rows)
    for i in range(n_rows // tile):
        rows = slice(i * tile, (i + 1) * tile)
        o = acc_ref[rows, :]
        z = z_ref[0, rows, :]
        y = jnp.concatenate([_rms_rows(o[:, hh * GLA_DV:(hh + 1) * GLA_DV], norm_ref[...])
                             for hh in range(GLA_PAIR)], axis=1)
        o_ref[0, rows, :] = (y * (z * jax.nn.sigmoid(z))).astype(o_ref.dtype)


def _gla_kernel(q_ref, k_ref, v_ref, z_ref, gs_ref, qc_ref, kc_ref, vc_ref, zc_ref, gsc_ref, cos_ref, sin_ref,
                wg_ref, bgate_ref, norm_ref, sums_ref, masks_ref, o_ref, oc_ref, st_ref, acc_ref, accc_ref):
    t = q_ref.shape[1]
    tc = qc_ref.shape[1]
    n_lat, n_ctx = t // CHUNK, tc // CHUNK
    st_ref[...] = jnp.zeros_like(st_ref)
    acc_ref[...] = jnp.zeros_like(acc_ref)
    accc_ref[...] = jnp.zeros_like(accc_ref)
    shared = (wg_ref, bgate_ref, sums_ref, masks_ref, st_ref)

    def ctx_step(i, carry):
        _gla_chunk(i, False, qc_ref, kc_ref, vc_ref, gsc_ref, None, *shared, accc_ref)
        _gla_chunk(n_ctx - 1 - i, True, qc_ref, kc_ref, vc_ref, gsc_ref, None, *shared, accc_ref)
        return carry

    def lat_step(i, carry):
        _gla_chunk(i, False, q_ref, k_ref, v_ref, gs_ref, (cos_ref, sin_ref), *shared, acc_ref)
        _gla_chunk(n_lat - 1 - i, True, q_ref, k_ref, v_ref, gs_ref, (cos_ref, sin_ref), *shared, acc_ref)
        return carry

    lax.fori_loop(0, n_ctx, ctx_step, 0)
    lax.fori_loop(0, n_lat, lat_step, 0)
    _gla_finish(acc_ref, z_ref, norm_ref, o_ref, t)
    _gla_finish(accc_ref, zc_ref, norm_ref, oc_ref, tc)


def gla_branch(p_main, p_small, pc_main, pc_small, wg2, bg, norm):
    b, t, _ = p_main.shape
    tc = pc_main.shape[1]
    nj = GLA_HEADS // GLA_PAIR
    vw = GLA_PAIR * GLA_DV
    o_q = MY_OFF["gla_q"][0] // LANE
    o_k = MY_OFF["gla_k"][0] // LANE
    o_v = MY_OFF["gla_v"][0] // vw
    o_z = MY_OFF["gla_z"][0] // vw
    sums, masks = _gla_tables()
    cos, sin = _gla_rope_tables(t)
    wge, bge = _gla_gate_weights(wg2.astype(F32), bg.astype(F32))

    def seq(n, w, o):
        return pl.BlockSpec((1, n, w), lambda i, j: (i, 0, o + j))

    def whole(n):
        return pl.BlockSpec((1, n, N_SMALL), lambda i, j: (i, 0, 0))

    def const(shape):
        return pl.BlockSpec(shape, lambda i, j: (0,) * len(shape))

    return pl.pallas_call(
        _gla_kernel,
        grid=(b, nj),
        in_specs=[seq(t, LANE, o_q), seq(t, LANE, o_k), seq(t, vw, o_v), seq(t, vw, o_z), whole(t),
                  seq(tc, LANE, o_q), seq(tc, LANE, o_k), seq(tc, vw, o_v), seq(tc, vw, o_z), whole(tc),
                  const((t, LANE)), const((t, LANE)),
                  pl.BlockSpec((2, 1, N_SMALL, LANE), lambda i, j: (0, j, 0, 0)),
                  pl.BlockSpec((2, 1, 1, LANE), lambda i, j: (0, j, 0, 0)),
                  const((1, GLA_DV)), const((2, GLA_SUM_ROWS, CHUNK)),
                  const((2, GLA_LEVELS + 1, GLA_PAIR * CHUNK, CHUNK))],
        out_specs=[pl.BlockSpec((1, t, vw), lambda i, j: (i, 0, j)), pl.BlockSpec((1, tc, vw), lambda i, j: (i, 0, j))],
        out_shape=[jax.ShapeDtypeStruct((b, t, BR_W), BF16), jax.ShapeDtypeStruct((b, tc, BR_W), BF16)],
        scratch_shapes=[pltpu.VMEM((2, vw, LANE), F32), pltpu.VMEM((t, vw), F32), pltpu.VMEM((tc, vw), F32)],
        compiler_params=pltpu.CompilerParams(vmem_limit_bytes=VMEM_LIMIT),
        name="gla_scan",
    )(p_main, p_main, p_main, p_main, p_small, pc_main, pc_main, pc_main, pc_main, pc_small,
      jnp.asarray(cos), jnp.asarray(sin), wge, bge, norm.astype(F32).reshape(1, GLA_DV),
      jnp.asarray(sums, BF16), jnp.asarray(masks))


def _dot3(a, b):
    a1 = a.astype(BF16)
    a2 = (a - a1.astype(F32)).astype(BF16)
    b1 = b.astype(BF16)
    b2 = (b - b1.astype(F32)).astype(BF16)
    return (jnp.dot(a1, b1, preferred_element_type=F32) + jnp.dot(a1, b2, preferred_element_type=F32)
            + jnp.dot(a2, b1, preferred_element_type=F32))


def _unit_triangular_inverse(a):
    n = a.shape[0]
    eye = (lax.broadcasted_iota(jnp.int32, (n, n), 0) == lax.broadcasted_iota(jnp.int32, (n, n), 1)).astype(F32)
    inv = eye - a
    power = a
    for _ in range(int(math.log2(n)) - 1):
        power = _dot3(power, power)
        inv = inv + _dot3(inv, power)
    return inv


def _gdn_prepare(q_ref, k_ref, v_ref, gs_ref, wq_ref, wk_ref, wv_ref, sel_ref, coef_ref, qkv_s, gb_s):
    hp = lax.Precision.HIGHEST
    n = q_ref.shape[1]
    for i, (x_ref, w_ref) in enumerate(((q_ref, wq_ref), (k_ref, wk_ref), (v_ref, wv_ref))):
        y = _dwconv_rows(x_ref[0], w_ref, GDN_CONV)
        y = y * jax.nn.sigmoid(y)
        if i < 2:
            y = y * lax.rsqrt(jnp.sum(y * y, axis=-1, keepdims=True) + EPS)
        if i == 0:
            y = y * (GDN_DH ** -0.5)
        qkv_s[i, 0:n, :] = y
    gs = gs_ref[0]
    for d in range(2):
        a = jnp.dot(gs, sel_ref[0, d], precision=hp, preferred_element_type=F32)
        bb = jnp.dot(gs, sel_ref[0, 2 + d], precision=hp, preferred_element_type=F32)
        gb_s[d, 0:n, :] = -jnp.exp(coef_ref[0, d:d + 1, :]) * _softplus(a + coef_ref[0, 2 + d:3 + d, :])
        gb_s[2 + d, 0:n, :] = jax.nn.sigmoid(bb)


def _gdn_chunk(c, rev, qkv_s, gb_s, tri_ref, st_ref, acc_ref):
    d = 1 if rev else 0
    r0 = pl.multiple_of(c * CHUNK, CHUNK)
    rows = pl.ds(r0, CHUNK)
    q = qkv_s[0, rows, :]
    k = qkv_s[1, rows, :]
    v = qkv_s[2, rows, :]
    g = gb_s[d, rows, :]
    beta = gb_s[2 + d, rows, :]
    gam = _sum_rows(tri_ref[d], g)
    gam_end = gam[0:1] if rev else gam[CHUNK - 1:CHUNK]
    ti = lax.broadcasted_iota(jnp.int32, (CHUNK, CHUNK), 0)
    si = lax.broadcasted_iota(jnp.int32, (CHUNK, CHUNK), 1)
    earlier = (si > ti) if rev else (si < ti)
    diff = gam[:, 0:CHUNK] - gam.T[0:CHUNK, :]
    dec_strict = jnp.where(earlier, jnp.exp(jnp.where(earlier, diff, 0.0)), 0.0)
    dec_incl = jnp.where(si == ti, 1.0, dec_strict)
    kb = k * beta
    k16 = k.astype(BF16)
    a_kk = _nt_dot(kb.astype(BF16), k16) * dec_strict
    tinv = _unit_triangular_inverse(a_kk)
    rhs = jnp.concatenate([v * beta, kb * jnp.exp(gam)], axis=1)
    sol = _dot3(tinv, rhs)
    u = sol[:, 0:GDN_DH]
    w = sol[:, GDN_DH:]
    a_qk = _nt_dot(q.astype(BF16), k16) * dec_incl
    q_dec = (q * jnp.exp(gam)).astype(BF16)
    k_dec = (k * jnp.exp(gam_end - gam)).astype(BF16)
    st = st_ref[d]
    st16 = st.astype(BF16)
    e = u - jnp.dot(w.astype(BF16), st16, preferred_element_type=F32)
    e16 = e.astype(BF16)
    acc_ref[rows, :] += (jnp.dot(q_dec, st16, preferred_element_type=F32)
                         + jnp.dot(a_qk.astype(BF16), e16, preferred_element_type=F32))
    st_ref[d] = st * jnp.exp(gam_end) + _tn_dot(k_dec, e16)


def _gdn_finish(acc_ref, z_ref, norm_ref, o_ref, n_rows):
    tile = min(256, n_rows)
    for i in range(n_rows // tile):
        rows = slice(i * tile, (i + 1) * tile)
        z = z_ref[0, rows, :]
        o_ref[0, rows, :] = (_rms_rows(acc_ref[rows, :], norm_ref[...]) * (z * jax.nn.sigmoid(z))).astype(o_ref.dtype)


def _gdn_kernel(q_ref, k_ref, v_ref, z_ref, gs_ref, qc_ref, kc_ref, vc_ref, zc_ref, gsc_ref, wq_ref, wk_ref, wv_ref,
                sel_ref, coef_ref, norm_ref, tri_ref, o_ref, oc_ref, st_ref, qkv_s, gb_s, qkvc_s, gbc_s, acc_ref,
                accc_ref):
    t = q_ref.shape[1]
    tc = qc_ref.shape[1]
    n_lat, n_ctx = t // CHUNK, tc // CHUNK
    _gdn_prepare(qc_ref, kc_ref, vc_ref, gsc_ref, wq_ref, wk_ref, wv_ref, sel_ref, coef_ref, qkvc_s, gbc_s)
    _gdn_prepare(q_ref, k_ref, v_ref, gs_ref, wq_ref, wk_ref, wv_ref, sel_ref, coef_ref, qkv_s, gb_s)
    st_ref[...] = jnp.zeros_like(st_ref)
    acc_ref[...] = jnp.zeros_like(acc_ref)
    accc_ref[...] = jnp.zeros_like(accc_ref)

    def ctx_step(i, carry):
        _gdn_chunk(i, False, qkvc_s, gbc_s, tri_ref, st_ref, accc_ref)
        _gdn_chunk(n_ctx - 1 - i, True, qkvc_s, gbc_s, tri_ref, st_ref, accc_ref)
        return carry

    def lat_step(i, carry):
        _gdn_chunk(i, False, qkv_s, gb_s, tri_ref, st_ref, acc_ref)
        _gdn_chunk(n_lat - 1 - i, True, qkv_s, gb_s, tri_ref, st_ref, acc_ref)
        return carry

    lax.fori_loop(0, n_ctx, ctx_step, 0)
    lax.fori_loop(0, n_lat, lat_step, 0)
    _gdn_finish(acc_ref, z_ref, norm_ref, o_ref, t)
    _gdn_finish(accc_ref, zc_ref, norm_ref, oc_ref, tc)


def gdn_branch(p_main, p_small, pc_main, pc_small, conv_w, a_log, dt_bias, norm):
    b, t, _ = p_main.shape
    tc = pc_main.shape[1]
    nh = GDN_HEADS
    o_q = MY_OFF["gdn_qkv"][0] // LANE
    o_z = MY_OFF["gdn_z"][0] // LANE
    wt = conv_w.T.astype(F32)
    sel = np.zeros((nh, 4, N_SMALL, LANE), np.float32)
    for h in range(nh):
        for d in range(2):
            sel[h, d, MY_OFF["gdn_a"][0] + d * nh + h, :] = 1.0
            sel[h, 2 + d, MY_OFF["gdn_b"][0] + d * nh + h, :] = 1.0
    coef = jnp.concatenate([a_log.astype(F32).T, dt_bias.astype(F32).T], axis=1)
    coef = jnp.broadcast_to(coef[:, :, None], (nh, 4, LANE))
    ti = np.arange(CHUNK)[:, None]
    ri = np.arange(CHUNK)[None, :]
    tri = np.stack([ri <= ti, ri >= ti]).astype(np.float32)

    def seq(n, o):
        return pl.BlockSpec((1, n, LANE), lambda i, j: (i, 0, o + j))

    def whole(n):
        return pl.BlockSpec((1, n, N_SMALL), lambda i, j: (i, 0, 0))

    def wspec(o):
        return pl.BlockSpec((GDN_CONV, LANE), lambda i, j: (0, o + j))

    return pl.pallas_call(
        _gdn_kernel,
        grid=(b, nh),
        in_specs=[seq(t, o_q), seq(t, o_q + nh), seq(t, o_q + 2 * nh), seq(t, o_z), whole(t),
                  seq(tc, o_q), seq(tc, o_q + nh), seq(tc, o_q + 2 * nh), seq(tc, o_z), whole(tc),
                  wspec(0), wspec(nh), wspec(2 * nh),
                  pl.BlockSpec((1, 4, N_SMALL, LANE), lambda i, j: (j, 0, 0, 0)),
                  pl.BlockSpec((1, 4, LANE), lambda i, j: (j, 0, 0)),
                  pl.BlockSpec((1, GDN_DH), lambda i, j: (0, 0)),
                  pl.BlockSpec((2, CHUNK, CHUNK), lambda i, j: (0, 0, 0))],
        out_specs=[pl.BlockSpec((1, t, LANE), lambda i, j: (i, 0, j)), pl.BlockSpec((1, tc, LANE), lambda i, j: (i, 0, j))],
        out_shape=[jax.ShapeDtypeStruct((b, t, BR_W), BF16), jax.ShapeDtypeStruct((b, tc, BR_W), BF16)],
        scratch_shapes=[pltpu.VMEM((2, GDN_DH, GDN_DH), F32),
                        pltpu.VMEM((3, t, LANE), F32), pltpu.VMEM((4, t, LANE), F32),
                        pltpu.VMEM((3, tc, LANE), F32), pltpu.VMEM((4, tc, LANE), F32),
                        pltpu.VMEM((t, LANE), F32), pltpu.VMEM((tc, LANE), F32)],
        compiler_params=pltpu.CompilerParams(vmem_limit_bytes=VMEM_LIMIT),
        name="gdn_scan",
    )(p_main, p_main, p_main, p_main, p_small, pc_main, pc_main, pc_main, pc_main, pc_small,
      wt, wt, wt, jnp.asarray(sel), coef, norm.astype(F32).reshape(1, GDN_DH), jnp.asarray(tri, BF16))


HY_COLS = 256
HY_ROWS = 512


def dft_matrices(length):
    n = 2 * length
    r = jnp.arange(n, dtype=jnp.int32)[:, None]
    t = jnp.arange(length, dtype=jnp.int32)[None, :]
    k = jnp.where(r <= length, r, r - length)
    ang = ((k * t) % n).astype(F32) * (2.0 * math.pi / n)
    f = jnp.where(r <= length, jnp.cos(ang), -jnp.sin(ang))
    w = jnp.where((r == 0) | (r == length), 1.0, 2.0) / n
    return f.astype(BF16), (f * w).T.astype(BF16)


def _hy_embedding(length):
    t = np.linspace(0.0, 1.0, length)[:, None]
    bands = (HY_EMB - 1) // 2
    wpos = 2.0 * math.pi * np.arange(length)[:, None] / length
    f = np.linspace(1e-4, bands - 1, bands)[None]
    z = np.concatenate([t, np.cos(f * wpos), -np.sin(f * wpos)], axis=-1)
    z = np.pad(z, ((0, 0), (0, HY_FFN - HY_EMB)))
    deltas = np.abs(np.linspace(math.log(HY_TARGET) / HY_SLOW, math.log(HY_TARGET) / HY_FAST, BR_W))[None]
    return z.astype(np.float32), deltas.astype(np.float32)


def _hy_filter_kernel(z_ref, win_ref, bin_ref, wmid_ref, bmid_ref, freq_ref, wout_ref, delta_ref, o_ref):
    hp = lax.Precision.HIGHEST
    length = z_ref.shape[0]
    h = jnp.sin(freq_ref[0] * (jnp.dot(z_ref[...], win_ref[...], precision=hp, preferred_element_type=F32)
                               + bin_ref[...]))
    for i in range(HY_INNER):
        h = jnp.sin(freq_ref[i + 1] * (jnp.dot(h, wmid_ref[i], precision=hp, preferred_element_type=F32)
                                       + bmid_ref[i]))
    h = jnp.dot(h, wout_ref[...], precision=hp, preferred_element_type=F32)
    t = lax.broadcasted_iota(jnp.int32, (length, BR_W), 0).astype(F32) * (1.0 / (length - 1))
    decay = jnp.exp(-t * delta_ref[...])
    h_f = h[:, :BR_W] * decay
    h_b = h[:, BR_W:] * decay
    for i, a in enumerate((h_f + h_b, h_f - h_b)):
        hi = a.astype(BF16)
        lo = (a - hi.astype(F32)).astype(BF16)
        o_ref[:, (2 * i) * BR_W:(2 * i + 1) * BR_W] = hi
        o_ref[:, (2 * i + 1) * BR_W:(2 * i + 2) * BR_W] = lo


def _hy_spectrum_kernel(f_ref, a_ref, ks_ref, kd_ref, *, length):
    i = pl.program_id(0)
    tm = f_ref.shape[0]
    r = jnp.dot(f_ref[...], a_ref[...], preferred_element_type=F32)
    ks_ref[...] = r[:, :BR_W] + r[:, BR_W:2 * BR_W]
    kd = r[:, 2 * BR_W:3 * BR_W] + r[:, 3 * BR_W:]
    row = lax.broadcasted_iota(jnp.int32, kd.shape, 0) + i * tm
    kd_ref[...] = jnp.where(row == length, 0.0, kd)


def hyena_filter_spectrum(length, fwd, filt):
    w_in, b_in, w_mid, b_mid, freq, w_out = filt
    z, deltas = _hy_embedding(length)
    n = 2 * length
    parts = pl.pallas_call(
        _hy_filter_kernel,
        out_shape=jax.ShapeDtypeStruct((length, 4 * BR_W), BF16),
        compiler_params=pltpu.CompilerParams(vmem_limit_bytes=VMEM_LIMIT),
        name="hyena_filter",
    )(jnp.asarray(z), jnp.pad(w_in, ((0, HY_FFN - HY_EMB), (0, 0))), b_in.reshape(1, HY_FFN), w_mid,
      b_mid.reshape(HY_INNER, 1, HY_FFN), freq.reshape(HY_INNER + 1, 1, HY_FFN), w_out, jnp.asarray(deltas))
    tm = min(HY_ROWS, n)
    return pl.pallas_call(
        functools.partial(_hy_spectrum_kernel, length=length),
        grid=(n // tm,),
        in_specs=[pl.BlockSpec((tm, length), lambda i: (i, 0)),
                  pl.BlockSpec((length, 4 * BR_W), lambda i: (0, 0))],
        out_specs=[pl.BlockSpec((tm, BR_W), lambda i: (i, 0)), pl.BlockSpec((tm, BR_W), lambda i: (i, 0))],
        out_shape=[jax.ShapeDtypeStruct((n, BR_W), F32), jax.ShapeDtypeStruct((n, BR_W), F32)],
        compiler_params=pltpu.CompilerParams(vmem_limit_bytes=VMEM_LIMIT),
        name="hyena_spectrum",
    )(fwd, parts)


def _shift_rows(x, off):
    t = x.shape[0]
    rolled = pltpu.roll(x, (-off) % t, axis=0)
    row = lax.broadcasted_iota(jnp.int32, x.shape, 0)
    valid = (row >= -off) if off < 0 else (row < t - off)
    return jnp.where(valid, rolled, 0.0)


def _dwconv_rows(x, w_ref, k):
    out = x * w_ref[k // 2:k // 2 + 1, :]
    for j in range(k):
        if j != k // 2:
            out = out + _shift_rows(x, j - k // 2) * w_ref[j:j + 1, :]
    return out


def _hy_prep_kernel(x0_ref, x1_ref, v_ref, z_ref, w0_ref, w1_ref, wv_ref, b0_ref, b1_ref, bv_ref, vx_ref, x0g_ref):
    x0 = _dwconv_rows(x0_ref[0], w0_ref, HY_CONV) + b0_ref[...]
    x1 = _dwconv_rows(x1_ref[0], w1_ref, HY_CONV) + b1_ref[...]
    v = _dwconv_rows(v_ref[0], wv_ref, HY_CONV) + bv_ref[...]
    z = z_ref[0]
    vx_ref[0] = v * x1
    x0g_ref[0] = x0 * (z * jax.nn.sigmoid(z))


def hyena_prep(p_main, conv_w, conv_b):
    b, length, _ = p_main.shape
    cb = LANE
    nj = BR_W // cb
    o_xv = MY_OFF["hy_xv"][0] // cb
    o_z = MY_OFF["hy_z"][0] // cb
    wt = conv_w.T
    bt = conv_b.reshape(1, 3 * BR_W)
    dspec = lambda o: pl.BlockSpec((1, length, cb), lambda i, j: (i, 0, o + j))
    wspec = lambda o: pl.BlockSpec((HY_CONV, cb), lambda i, j: (0, o + j))
    bspec = lambda o: pl.BlockSpec((1, cb), lambda i, j: (0, o + j))
    ospec = pl.BlockSpec((1, length, cb), lambda i, j: (i, 0, j))
    return pl.pallas_call(
        _hy_prep_kernel,
        grid=(b, nj),
        in_specs=[dspec(o_xv), dspec(o_xv + nj), dspec(o_xv + 2 * nj), dspec(o_z),
                  wspec(0), wspec(nj), wspec(2 * nj), bspec(0), bspec(nj), bspec(2 * nj)],
        out_specs=[ospec, ospec],
        out_shape=[jax.ShapeDtypeStruct((b, length, BR_W), F32)] * 2,
        compiler_params=pltpu.CompilerParams(vmem_limit_bytes=VMEM_LIMIT),
        name="hyena_prep",
    )(p_main, p_main, p_main, p_main, wt, wt, wt, bt, bt, bt)


def _hy_fwd_kernel(vx_ref, f_ref, kr_ref, ki_ref, knyq_ref, y_ref, *, length):
    v = vx_ref[0].astype(BF16)
    tm = min(HY_ROWS, length)
    for kt in range(length // tm):
        lo = kt * tm
        a = jnp.dot(f_ref[lo:lo + tm, :], v, preferred_element_type=F32)
        bh = jnp.dot(f_ref[length + lo:length + lo + tm, :], v, preferred_element_type=F32)
        kr = kr_ref[lo:lo + tm, :]
        ki = ki_ref[lo:lo + tm, :]
        if kt == 0:
            row = lax.broadcasted_iota(jnp.int32, kr.shape, 0)
            kb = jnp.where(row == 0, knyq_ref[0:1, :], kr)
        else:
            kb = kr
        y_ref[0, lo:lo + tm, :] = (a * kr - bh * ki).astype(BF16)
        y_ref[0, length + lo:length + lo + tm, :] = (a * ki + bh * kb).astype(BF16)


def _hy_inv_kernel(y_ref, inv_ref, vx_ref, x0g_ref, skip_ref, o_ref, *, length):
    tm = min(HY_ROWS, length)
    y = y_ref[0]
    for tt in range(length // tm):
        lo = tt * tm
        conv = jnp.dot(inv_ref[lo:lo + tm, :], y, preferred_element_type=F32)
        o_ref[0, lo:lo + tm, :] = ((conv + skip_ref[...] * vx_ref[0, lo:lo + tm, :])
                                   * x0g_ref[0, lo:lo + tm, :]).astype(o_ref.dtype)


def hyena_branch(p_main, conv_w, conv_b, skip, fwd, inv, ks, kd):
    b, length, _ = p_main.shape
    n = 2 * length
    vx, x0g = hyena_prep(p_main, conv_w, conv_b)
    cb = HY_COLS
    nj = BR_W // cb
    resident = lambda shape: pl.BlockSpec(shape, lambda j, i: (0, 0), pipeline_mode=pl.Buffered(1))
    y = pl.pallas_call(
        functools.partial(_hy_fwd_kernel, length=length),
        grid=(nj, b),
        in_specs=[pl.BlockSpec((1, length, cb), lambda j, i: (i, 0, j)),
                  resident((n, length)),
                  pl.BlockSpec((length, cb), lambda j, i: (0, j)),
                  pl.BlockSpec((length, cb), lambda j, i: (1, j)),
                  pl.BlockSpec((8, cb), lambda j, i: (length // 8, j))],
        out_specs=pl.BlockSpec((1, n, cb), lambda j, i: (i, 0, j)),
        out_shape=jax.ShapeDtypeStruct((b, n, BR_W), BF16),
        compiler_params=pltpu.CompilerParams(vmem_limit_bytes=VMEM_LIMIT),
        name="hyena_dft",
    )(vx, fwd, ks, kd, ks)
    return pl.pallas_call(
        functools.partial(_hy_inv_kernel, length=length),
        grid=(nj, b),
        in_specs=[pl.BlockSpec((1, n, cb), lambda j, i: (i, 0, j)),
                  resident((length, n)),
                  pl.BlockSpec((1, length, cb), lambda j, i: (i, 0, j)),
                  pl.BlockSpec((1, length, cb), lambda j, i: (i, 0, j)),
                  pl.BlockSpec((1, cb), lambda j, i: (0, j))],
        out_specs=pl.BlockSpec((1, length, cb), lambda j, i: (i, 0, j)),
        out_shape=jax.ShapeDtypeStruct((b, length, BR_W), BF16),
        compiler_params=pltpu.CompilerParams(vmem_limit_bytes=VMEM_LIMIT),
        name="hyena_idft",
    )(y, inv, vx, x0g, skip.reshape(1, BR_W))


def mixers(p_main, p_small, pc_main, pc_small, na_rpb, gla_wg2, gla_bg, gla_norm, gdn_conv, gdn_a_log, gdn_dt_bias,
           gdn_norm, hy_conv, hy_conv_b, hy_filt, hy_skip, with_ctx, dft_lat, dft_ctx):
    t = p_main.shape[1]
    ya = na_branch(p_main, pc_main, na_rpb)
    yb, yb_c = gla_branch(p_main, p_small, pc_main, pc_small, gla_wg2, gla_bg, gla_norm)
    yc, yc_c = gdn_branch(p_main, p_small, pc_main, pc_small, gdn_conv, gdn_a_log, gdn_dt_bias, gdn_norm)
    fwd, inv = dft_lat
    ks, kd = hyena_filter_spectrum(t, fwd, hy_filt)
    yd = hyena_branch(p_main, hy_conv, hy_conv_b, hy_skip, fwd, inv, ks, kd)
    if not with_ctx:
        return (ya, yb, yc, yd), None
    ya_c = ctx_attention_branch(pc_main)
    fwd_c, inv_c = dft_ctx
    ks_c, kd_c = hyena_filter_spectrum(pc_main.shape[1], fwd_c, hy_filt)
    yd_c = hyena_branch(pc_main, hy_conv, hy_conv_b, hy_skip, fwd_c, inv_c, ks_c, kd_c)
    return (ya, yb, yc, yd), (ya_c, yb_c, yc_c, yd_c)


def kernel(x, c, ctx, c_ctx, w_mod, b_mod, g_pre, g_post, w_in, w_gate, b_gate, w_branch, w_out, na_rpb, gla_wg2, gla_bg, gla_norm, gdn_conv, gdn_a_log, gdn_dt_bias, gdn_norm, hy_conv, hy_conv_b, hy_w_in, hy_b_in, hy_w_mid, hy_b_mid, hy_freq, hy_w_out, hy_skip):
    b, t, d = x.shape
    tc = ctx.shape[1]
    h = x.reshape(b * t, d)
    hc = ctx.reshape(b * tc, d)
    c_all = jnp.concatenate([c, c_ctx[None], jnp.zeros((7, d), F32)], axis=0)
    dft_lat = dft_matrices(t)
    dft_ctx = dft_matrices(tc)
    for l in range(DEPTH):
        with_ctx = l < DEPTH - 1
        mod = modulation(c_all, w_mod[l], b_mod[l])
        shift, scale, gate = [mod[:, i * d:(i + 1) * d].reshape(-1, 1, d) for i in range(3)]
        u = prenorm_mod(h, g_pre[l], scale[:b], shift[:b], t)
        uc = prenorm_mod(hc, g_pre[l], scale[b:b + 1], shift[b:b + 1], b * tc)
        w_main, w_small = _permute_w_in(w_in[l])
        p_main = matmul(u, w_main, 1024, 1536, name="in_proj").reshape(b, t, N_MAIN)
        p_small = matmul(u, w_small, 1024, N_SMALL, name="in_proj_small").reshape(b, t, N_SMALL)
        pc_main = matmul(uc, w_main, 1024, 1536, name="in_proj_ctx").reshape(b, tc, N_MAIN)
        pc_small = matmul(uc, w_small, 1024, N_SMALL, name="in_proj_ctx_small").reshape(b, tc, N_SMALL)
        ys, ycs = mixers(p_main, p_small, pc_main, pc_small,
                         na_rpb[l], gla_wg2[l], gla_bg[l], gla_norm[l], gdn_conv[l], gdn_a_log[l], gdn_dt_bias[l],
                         gdn_norm[l], hy_conv[l], hy_conv_b[l],
                         (hy_w_in[l], hy_b_in[l], hy_w_mid[l], hy_b_mid[l], hy_freq[l], hy_w_out[l]),
                         hy_skip[l], with_ctx, dft_lat, dft_ctx)
        h_new = merge(u, ys, w_gate[l], b_gate[l], w_branch[l], w_out[l], h, gate[:b], g_post[l], t)
        if with_ctx:
            hc = merge(uc, ycs, w_gate[l], b_gate[l], w_branch[l], w_out[l], hc, gate[b:b + 1], g_post[l], b * tc)
        h = h_new
    return h.reshape(b, t, d)
```

```python
import functools
import math

import numpy as np
import jax
import jax.numpy as jnp
from jax import lax
from jax.experimental import pallas as pl
from jax.experimental.pallas import tpu as pltpu

D_MODEL = 2048
BATCH = 16
SEQ = 2048
DEPTH = 2
GRID_W = 64
CTX_LEN = 256
N_BRANCH = 4
BR_W = D_MODEL // N_BRANCH
NA_HEADS = 8
NA_DH = BR_W // NA_HEADS
WIN_R = 8
WIN_C = 16
QBLK_C = 16
KBLK_C = WIN_C + QBLK_C
N_CBLK = GRID_W // QBLK_C
GLA_HEADS = 4
GLA_DV = BR_W // GLA_HEADS
GLA_DK = GLA_DV // 2
GLA_RANK = 16
GLA_TAU = 16.0
CHUNK = 64
GDN_HEADS = 4
GDN_DH = BR_W // GDN_HEADS
GDN_CONV = 5
HY_CONV = 3
HY_EMB = 33
HY_FFN = 64
HY_INNER = 2
HY_TARGET = 1e-2
HY_FAST = 0.3
HY_SLOW = 1.5
ROPE_THETA = 10000.0
EPS = 1e-6
F32 = jnp.float32
BF16 = jnp.bfloat16

REF_SPLITS = (
    ("na_qkv", 3 * BR_W), ("na_z", BR_W),
    ("gla_q", GLA_HEADS * GLA_DK), ("gla_k", GLA_HEADS * GLA_DK), ("gla_v", BR_W), ("gla_z", BR_W),
    ("gla_g", 2 * GLA_RANK),
    ("gdn_qkv", 3 * BR_W), ("gdn_z", BR_W), ("gdn_a", 2 * GDN_HEADS), ("gdn_b", 2 * GDN_HEADS),
    ("hy_xv", 3 * BR_W), ("hy_z", BR_W),
)
MAIN_ORDER = ("na_qkv", "na_z", "gla_q", "gla_k", "gla_v", "gla_z", "gdn_qkv", "gdn_z", "hy_xv", "hy_z")
SMALL_ORDER = ("gla_g", "gdn_a", "gdn_b")
LANE = 128
N_MAIN = sum(w for n, w in REF_SPLITS if n in MAIN_ORDER)
N_SMALL = LANE
VMEM_LIMIT = 48 * 1024 * 1024


def _ref_offsets():
    out, o = {}, 0
    for name, w in REF_SPLITS:
        out[name] = (o, w)
        o += w
    return out


def _my_offsets():
    out, o = {}, 0
    for name in MAIN_ORDER:
        w = dict(REF_SPLITS)[name]
        out[name] = (o, w)
        o += w
    o = 0
    for name in SMALL_ORDER:
        w = dict(REF_SPLITS)[name]
        out[name] = (o, w)
        o += w
    return out


REF_OFF = _ref_offsets()
MY_OFF = _my_offsets()


def _permute_w_in(w_in_l):
    main = jnp.concatenate([w_in_l[:, REF_OFF[n][0]:REF_OFF[n][0] + REF_OFF[n][1]] for n in MAIN_ORDER], axis=1)
    small = jnp.concatenate([w_in_l[:, REF_OFF[n][0]:REF_OFF[n][0] + REF_OFF[n][1]] for n in SMALL_ORDER], axis=1)
    small = jnp.pad(small, ((0, 0), (0, N_SMALL - small.shape[1])))
    return main.astype(BF16), small.astype(BF16)


def _mod_kernel(c_ref, w_ref, b_ref, o_ref):
    c = c_ref[...]
    a = (c * jax.nn.sigmoid(c)).astype(BF16)
    o_ref[...] = jnp.dot(a, w_ref[...].astype(BF16), preferred_element_type=F32) + b_ref[...]


def modulation(c_all, w_mod_l, b_mod_l):
    r, d = c_all.shape
    n = w_mod_l.shape[1]
    tn = 768
    return pl.pallas_call(
        _mod_kernel,
        grid=(n // tn,),
        in_specs=[pl.BlockSpec((r, d), lambda j: (0, 0)),
                  pl.BlockSpec((d, tn), lambda j: (0, j)),
                  pl.BlockSpec((1, tn), lambda j: (0, j))],
        out_specs=pl.BlockSpec((r, tn), lambda j: (0, j)),
        out_shape=jax.ShapeDtypeStruct((r, n), F32),
        compiler_params=pltpu.CompilerParams(vmem_limit_bytes=VMEM_LIMIT),
        name="modulation",
    )(c_all, w_mod_l, b_mod_l.reshape(1, n))


def _prenorm_kernel(x_ref, g_ref, scale_ref, shift_ref, o_ref):
    x = x_ref[...]
    y = x * lax.rsqrt(jnp.mean(x * x, axis=-1, keepdims=True) + EPS)
    u = (y * g_ref[...]) * (1.0 + scale_ref[0]) + shift_ref[0]
    o_ref[...] = u.astype(o_ref.dtype)


def prenorm_mod(x2d, g_pre_l, scale, shift, rows_per_group, out_dtype=BF16):
    n, d = x2d.shape
    tm = 512
    tiles_per_group = rows_per_group // tm
    return pl.pallas_call(
        _prenorm_kernel,
        grid=(n // tm,),
        in_specs=[pl.BlockSpec((tm, d), lambda i: (i, 0)),
                  pl.BlockSpec((1, d), lambda i: (0, 0)),
                  pl.BlockSpec((1, 1, d), lambda i: (i // tiles_per_group, 0, 0)),
                  pl.BlockSpec((1, 1, d), lambda i: (i // tiles_per_group, 0, 0))],
        out_specs=pl.BlockSpec((tm, d), lambda i: (i, 0)),
        out_shape=jax.ShapeDtypeStruct((n, d), out_dtype),
        compiler_params=pltpu.CompilerParams(vmem_limit_bytes=VMEM_LIMIT),
        name="prenorm_mod",
    )(x2d, g_pre_l.reshape(1, d), scale, shift)


def _matmul_kernel(u_ref, w_ref, o_ref):
    o_ref[...] = jnp.dot(u_ref[...], w_ref[...], preferred_element_type=F32).astype(o_ref.dtype)


def matmul(u, w, tm, tn, out_dtype=F32, name="matmul"):
    n, k = u.shape
    m = w.shape[1]
    return pl.pallas_call(
        _matmul_kernel,
        grid=(m // tn, n // tm),
        in_specs=[pl.BlockSpec((tm, k), lambda j, i: (i, 0)),
                  pl.BlockSpec((k, tn), lambda j, i: (0, j))],
        out_specs=pl.BlockSpec((tm, tn), lambda j, i: (i, j)),
        out_shape=jax.ShapeDtypeStruct((n, m), out_dtype),
        compiler_params=pltpu.CompilerParams(vmem_limit_bytes=VMEM_LIMIT),
        name=name,
    )(u, w)


def _merge_kernel(u_ref, y0_ref, y1_ref, y2_ref, y3_ref, wg_ref, bg_ref, wb_ref, wo_ref, h_ref, gate_ref, gpost_ref,
                  o_ref, acc_ref):
    n = pl.program_id(1)
    u = u_ref[...]
    merged = None
    for i, y_ref in enumerate((y0_ref, y1_ref, y2_ref, y3_ref)):
        g = jnp.dot(u, wg_ref[i], preferred_element_type=F32) + bg_ref[i]
        yb = jnp.dot(y_ref[...], wb_ref[i], preferred_element_type=F32)
        term = jax.nn.sigmoid(g) * yb
        merged = term if merged is None else merged + term
    part = jnp.dot(merged.astype(BF16), wo_ref[...], preferred_element_type=F32)

    @pl.when(n == 0)
    def _():
        acc_ref[...] = part

    @pl.when(n > 0)
    def _():
        acc_ref[...] += part

    @pl.when(n == pl.num_programs(1) - 1)
    def _():
        y = acc_ref[...]
        yn = y * lax.rsqrt(jnp.mean(y * y, axis=-1, keepdims=True) + EPS) * gpost_ref[...]
        o_ref[...] = h_ref[...] + gate_ref[0] * yn


def merge(u, ys, w_gate_l, b_gate_l, w_branch_l, w_out_l, h2d, gate, g_post_l, rows_per_group):
    n, d = u.shape
    tm, tn = 512, 256
    tiles_per_group = rows_per_group // tm
    yspec = pl.BlockSpec((tm, BR_W), lambda i, j: (i, 0))
    return pl.pallas_call(
        _merge_kernel,
        grid=(n // tm, d // tn),
        in_specs=[pl.BlockSpec((tm, d), lambda i, j: (i, 0)),
                  yspec, yspec, yspec, yspec,
                  pl.BlockSpec((N_BRANCH, d, tn), lambda i, j: (0, 0, j)),
                  pl.BlockSpec((N_BRANCH, 1, tn), lambda i, j: (0, 0, j)),
                  pl.BlockSpec((N_BRANCH, BR_W, tn), lambda i, j: (0, 0, j)),
                  pl.BlockSpec((tn, d), lambda i, j: (j, 0)),
                  pl.BlockSpec((tm, d), lambda i, j: (i, 0)),
                  pl.BlockSpec((1, 1, d), lambda i, j: (i // tiles_per_group, 0, 0)),
                  pl.BlockSpec((1, d), lambda i, j: (0, 0))],
        out_specs=pl.BlockSpec((tm, d), lambda i, j: (i, 0)),
        out_shape=jax.ShapeDtypeStruct((n, d), F32),
        scratch_shapes=[pltpu.VMEM((tm, d), F32)],
        compiler_params=pltpu.CompilerParams(vmem_limit_bytes=VMEM_LIMIT),
        name="merge",
    )(u, *[y.reshape(n, BR_W) for y in ys], w_gate_l.astype(BF16), b_gate_l.reshape(N_BRANCH, 1, d),
      w_branch_l.astype(BF16),
      w_out_l.astype(BF16), h2d, gate, g_post_l.reshape(1, d))


NA_NEG = -1e30
NA_PAIR = LANE // NA_DH


def na_bias_table(rpb, rows):
    assert rows >= WIN_R
    heads = rpb.shape[0]
    c = np.arange(GRID_W)[:, None]
    kc = np.arange(GRID_W)[None, :]
    cstart = np.clip(c - WIN_C // 2, 0, GRID_W - WIN_C)
    valid = (kc >= cstart) & (kc < cstart + WIN_C)
    dc = np.clip(kc - c + WIN_C - 1, 0, 2 * WIN_C - 2)
    onehot = ((dc[None] == np.arange(2 * WIN_C - 1)[:, None, None]) & valid[None]).astype(np.float32)
    by_dr = jnp.einsum("hrd,dck->hrck", rpb.astype(F32), jnp.asarray(onehot), precision=lax.Precision.HIGHEST)
    by_dr = jnp.where(valid, by_dr, NA_NEG)
    tabs = [by_dr[:, WIN_R - 1 - off:2 * WIN_R - 1 - off].transpose(0, 2, 1, 3).reshape(heads, GRID_W, WIN_R * GRID_W)
            for off in range(WIN_R)]
    return jnp.stack(tabs, axis=1)


def _nt_dot(a, b):
    return lax.dot_general(a, b, (((1,), (1,)), ((), ())), preferred_element_type=F32)


def _na_kernel(q_ref, k_ref, v_ref, kc_ref, vc_ref, z_ref, bias_ref, o_ref, *, rows):
    lane = lax.broadcasted_iota(jnp.int32, (GRID_W, LANE), 1)
    head_mask = [(lane >= hh * NA_DH) & (lane < (hh + 1) * NA_DH) for hh in range(NA_PAIR)]
    kc = kc_ref[0].astype(BF16)
    vc = vc_ref[0].astype(BF16)
    win = WIN_R * GRID_W

    def body(r, carry):
        rs = jnp.clip(r - WIN_R // 2, 0, rows - WIN_R)
        off = r - rs
        q0 = pl.multiple_of(r * GRID_W, GRID_W)
        k0 = pl.multiple_of(rs * GRID_W, GRID_W)
        q = q_ref[0, pl.ds(q0, GRID_W), :] * (NA_DH ** -0.5)
        kw = k_ref[0, pl.ds(k0, win), :].astype(BF16)
        vw = v_ref[0, pl.ds(k0, win), :].astype(BF16)
        out = None
        for hh in range(NA_PAIR):
            qh = jnp.where(head_mask[hh], q, 0.0).astype(BF16)
            s = _nt_dot(qh, kw) + bias_ref[hh, off]
            sc = _nt_dot(qh, kc)
            m = jnp.maximum(jnp.max(s, axis=-1, keepdims=True), jnp.max(sc, axis=-1, keepdims=True))
            p = jnp.exp(s - m)
            pc = jnp.exp(sc - m)
            l = jnp.sum(p, axis=-1, keepdims=True) + jnp.sum(pc, axis=-1, keepdims=True)
            o = (jnp.dot(p.astype(BF16), vw, preferred_element_type=F32)
                 + jnp.dot(pc.astype(BF16), vc, preferred_element_type=F32)) / l
            out = o if out is None else jnp.where(head_mask[hh], o, out)
        z = z_ref[0, pl.ds(q0, GRID_W), :]
        o_ref[0, pl.ds(q0, GRID_W), :] = (out * (z * jax.nn.sigmoid(z))).astype(o_ref.dtype)
        return carry

    lax.fori_loop(0, rows, body, 0, unroll=2)


def na_branch(p_main, pc_main, rpb):
    b, t, _ = p_main.shape
    tc = pc_main.shape[1]
    rows = t // GRID_W
    nj = BR_W // LANE
    o_q = MY_OFF["na_qkv"][0] // LANE
    o_z = MY_OFF["na_z"][0] // LANE
    bias = na_bias_table(rpb, rows)
    lat = lambda o: pl.BlockSpec((1, t, LANE), lambda i, j: (i, 0, o + j))
    cx = lambda o: pl.BlockSpec((1, tc, LANE), lambda i, j: (i, 0, o + j))
    return pl.pallas_call(
        functools.partial(_na_kernel, rows=rows),
        grid=(b, nj),
        in_specs=[lat(o_q), lat(o_q + nj), lat(o_q + 2 * nj), cx(o_q + nj), cx(o_q + 2 * nj), lat(o_z),
                  pl.BlockSpec((NA_PAIR, WIN_R, GRID_W, WIN_R * GRID_W), lambda i, j: (j, 0, 0, 0))],
        out_specs=pl.BlockSpec((1, t, LANE), lambda i, j: (i, 0, j)),
        out_shape=jax.ShapeDtypeStruct((b, t, BR_W), BF16),
        compiler_params=pltpu.CompilerParams(vmem_limit_bytes=VMEM_LIMIT),
        name="na_attention",
    )(p_main, p_main, p_main, pc_main, pc_main, p_main, bias)


def _ctx_attn_kernel(q_ref, k_ref, v_ref, z_ref, o_ref):
    tc = q_ref.shape[1]
    lane = lax.broadcasted_iota(jnp.int32, (tc, LANE), 1)
    q = q_ref[0] * (NA_DH ** -0.5)
    k = k_ref[0].astype(BF16)
    v = v_ref[0].astype(BF16)
    out = None
    for hh in range(NA_PAIR):
        mask = (lane >= hh * NA_DH) & (lane < (hh + 1) * NA_DH)
        s = _nt_dot(jnp.where(mask, q, 0.0).astype(BF16), k)
        p = jnp.exp(s - jnp.max(s, axis=-1, keepdims=True))
        o = jnp.dot(p.astype(BF16), v, preferred_element_type=F32) / jnp.sum(p, axis=-1, keepdims=True)
        out = o if out is None else jnp.where(mask, o, out)
    z = z_ref[0]
    o_ref[0] = (out * (z * jax.nn.sigmoid(z))).astype(o_ref.dtype)


def ctx_attention_branch(pc_main):
    b, tc, _ = pc_main.shape
    nj = BR_W // LANE
    o_q = MY_OFF["na_qkv"][0] // LANE
    o_z = MY_OFF["na_z"][0] // LANE
    cx = lambda o: pl.BlockSpec((1, tc, LANE), lambda i, j: (i, 0, o + j))
    return pl.pallas_call(
        _ctx_attn_kernel,
        grid=(b, nj),
        in_specs=[cx(o_q), cx(o_q + nj), cx(o_q + 2 * nj), cx(o_z)],
        out_specs=pl.BlockSpec((1, tc, LANE), lambda i, j: (i, 0, j)),
        out_shape=jax.ShapeDtypeStruct((b, tc, BR_W), BF16),
        compiler_params=pltpu.CompilerParams(vmem_limit_bytes=VMEM_LIMIT),
        name="ctx_attention",
    )(pc_main, pc_main, pc_main, pc_main)


def _sum_rows(m_bf16, x):
    hi = x.astype(BF16)
    lo = (x - hi.astype(F32)).astype(BF16)
    return jnp.dot(m_bf16, hi, preferred_element_type=F32) + jnp.dot(m_bf16, lo, preferred_element_type=F32)


def _tn_dot(a, b):
    return lax.dot_general(a, b, (((0,), (0,)), ((), ())), preferred_element_type=F32)


def _softplus(x):
    return jnp.maximum(x, 0.0) + jnp.log1p(jnp.exp(-jnp.abs(x)))


def _rms_rows(o, g):
    return o * lax.rsqrt(jnp.mean(o * o, axis=-1, keepdims=True) + EPS) * g


GLA_LEVELS = 6
GLA_PAIR = LANE // GLA_DK
GLA_SUM_ROWS = (2 + GLA_LEVELS) * CHUNK


def _gla_tables():
    assert CHUNK == 1 << GLA_LEVELS
    t = np.arange(CHUNK)[:, None]
    r = np.arange(CHUNK)[None, :]
    sums, masks = [], []
    for rev in (False, True):
        before = (r >= t) if rev else (r <= t)
        after = (r < t) if rev else (r > t)
        blocks = [before, after]
        lev_masks = []
        for lv in range(1, GLA_LEVELS + 1):
            seg = CHUNK >> lv
            st, sr = t // seg, r // seg
            q_half = (st % 2 == 0) if rev else (st % 2 == 1)
            k_half = ~q_half
            blocks.append((st == sr) & ((before & q_half) | (after & k_half)))
            lev_masks.append(q_half & (sr == (st + 1 if rev else st - 1)))
        lev_masks.append(r == t)
        sums.append(np.concatenate(blocks, axis=0))
        masks.append(np.stack([np.concatenate([m, m], axis=0) for m in lev_masks]))
    return np.stack(sums).astype(np.float32), np.stack(masks).astype(np.float32)


def _gla_rope_tables(t_len):
    pos = np.arange(t_len)
    row = (pos // GRID_W).astype(np.float64)
    col = (pos % GRID_W).astype(np.float64)
    n = GLA_DK // 4
    freqs = ROPE_THETA ** (-np.arange(n, dtype=np.float64) / n)
    ang = np.concatenate([row[:, None] * freqs, col[:, None] * freqs], axis=-1)
    cos = np.concatenate([np.cos(ang), np.cos(ang)] * GLA_PAIR, axis=-1)
    sin = np.concatenate([-np.sin(ang), np.sin(ang)] * GLA_PAIR, axis=-1)
    return cos.astype(np.float32), sin.astype(np.float32)


def _gla_gate_weights(wg2, bg):
    half = GLA_DK // 2
    lane = np.arange(LANE)
    w_all, b_all = [], []
    for d in range(2):
        wd, bd = [], []
        for j in range(GLA_HEADS // GLA_PAIR):
            cols = half * (GLA_PAIR * j + lane // GLA_DK) + lane % half
            w = jnp.zeros((N_SMALL, LANE), F32).at[MY_OFF["gla_g"][0] + d * GLA_RANK:
                                                    MY_OFF["gla_g"][0] + (d + 1) * GLA_RANK].set(wg2[d][:, cols])
            wd.append(w)
            bd.append(bg[d][cols].reshape(1, LANE))
        w_all.append(jnp.stack(wd))
        b_all.append(jnp.stack(bd))
    return jnp.stack(w_all), jnp.stack(b_all)


def _gla_local(c, q_ref, k_ref, v_ref, gs_ref, rope, wg_ref, bgate_ref, sums_ref, masks_ref, qr_s, ke_s, tot_s,
               acc_ref):
    hp = lax.Precision.HIGHEST
    r0 = pl.multiple_of(c * CHUNK, CHUNK)
    rows = pl.ds(r0, CHUNK)
    lane = lax.broadcasted_iota(jnp.int32, (CHUNK, LANE), 1)
    q = q_ref[0, rows, :] * (GLA_DK ** -0.5)
    k = k_ref[0, rows, :]
    v16 = v_ref[0, rows, :].astype(BF16)
    if rope is not None:
        cos = rope[0][rows, :]
        sin = rope[1][rows, :]
        first_half = (lane % GLA_DK) < GLA_DK // 2

        def rot(x):
            partner = jnp.where(first_half, pltpu.roll(x, LANE - GLA_DK // 2, axis=1), pltpu.roll(x, GLA_DK // 2, axis=1))
            return x * cos + partner * sin
        q, k = rot(q), rot(k)
    k16 = k.astype(BF16)
    gs = gs_ref[0, rows, :]

    def stack_heads(a):
        return jnp.concatenate([jnp.where((lane >= hh * GLA_DK) & (lane < (hh + 1) * GLA_DK), a, 0.0)
                                for hh in range(GLA_PAIR)], axis=0).astype(BF16)

    q_heads = stack_heads(q)
    atts = []
    for d in range(2):
        x = jnp.dot(gs, wg_ref[d, 0], precision=hp, preferred_element_type=F32) + bgate_ref[d, 0]
        g = (jnp.minimum(x, 0.0) - jnp.log1p(jnp.exp(-jnp.abs(x)))) * (1.0 / GLA_TAU)
        sums = _sum_rows(sums_ref[d], g)
        run = sums[0:CHUNK]
        total = run[0:1] if d else run[CHUNK - 1:CHUNK]
        qr_s[d, rows, :] = (q * jnp.exp(run)).astype(BF16)
        ke_s[d, rows, :] = (k * jnp.exp(sums[CHUNK:2 * CHUNK])).astype(BF16)
        tot_s[d, pl.ds(c, 1), :] = jnp.exp(total)
        att = _nt_dot(q_heads, k16) * masks_ref[d, GLA_LEVELS]
        for lv in range(GLA_LEVELS):
            o = (2 + lv) * CHUNK
            part = jnp.exp(sums[o:o + CHUNK])
            att = att + _nt_dot(stack_heads(q * part), (k * part).astype(BF16)) * masks_ref[d, lv]
        atts.append(att)
    av = jnp.dot(jnp.concatenate(atts, axis=0).astype(BF16), v16, preferred_element_type=F32)
    acc_ref[rows, :] = sum(
        jnp.concatenate([av[(GLA_PAIR * d + hh) * CHUNK:(GLA_PAIR * d + hh + 1) * CHUNK, hh * GLA_DV:(hh + 1) * GLA_DV]
                         for hh in range(GLA_PAIR)], axis=1) for d in range(2))


def _gla_recur(chunks, v_ref, qr_s, ke_s, tot_s, st_refs, acc_ref):
    outs = []
    for d, c in enumerate(chunks):
        rows = pl.ds(pl.multiple_of(c * CHUNK, CHUNK), CHUNK)
        st = st_refs[d][...]
        outs.append((rows, _nt_dot(qr_s[d, rows, :], st.astype(BF16))))
        upd = _tn_dot(v_ref[0, rows, :].astype(BF16), ke_s[d, rows, :])
        srow = lax.broadcasted_iota(jnp.int32, upd.shape, 0) // GLA_DV
        scol = lax.broadcasted_iota(jnp.int32, upd.shape, 1) // GLA_DK
        st_refs[d][...] = jnp.where(srow == scol, st * tot_s[d, pl.ds(c, 1), :] + upd, 0.0)
    for rows, o in outs:
        acc_ref[rows, :] += o


def _gla_finish(acc_ref, z_ref, norm_ref, o_ref, n_rows):
    tile = min(256, n_rows)
    for i in range(n_rows // tile):
        rows = slice(i * tile, (i + 1) * tile)
        o = acc_ref[rows, :]
        z = z_ref[0, rows, :]
        y = jnp.concatenate([_rms_rows(o[:, hh * GLA_DV:(hh + 1) * GLA_DV], norm_ref[...])
                             for hh in range(GLA_PAIR)], axis=1)
        o_ref[0, rows, :] = (y * (z * jax.nn.sigmoid(z))).astype(o_ref.dtype)


def _gla_kernel(q_ref, k_ref, v_ref, z_ref, gs_ref, qc_ref, kc_ref, vc_ref, zc_ref, gsc_ref, cos_ref, sin_ref,
                wg_ref, bgate_ref, norm_ref, sums_ref, masks_ref, o_ref, oc_ref, stf_ref, stb_ref, qr_s, ke_s, tot_s,
                acc_ref):
    st_refs = (stf_ref, stb_ref)
    for r in st_refs:
        r[...] = jnp.zeros_like(r)
    for (qq, kk, vv, gg), rope, zz_ref, out_ref in (((qc_ref, kc_ref, vc_ref, gsc_ref), None, zc_ref, oc_ref),
                                                    ((q_ref, k_ref, v_ref, gs_ref), (cos_ref, sin_ref), z_ref, o_ref)):
        n = qq.shape[1]
        n_chunks = n // CHUNK

        def local_step(c, carry, qq=qq, kk=kk, vv=vv, gg=gg, rope=rope):
            _gla_local(c, qq, kk, vv, gg, rope, wg_ref, bgate_ref, sums_ref, masks_ref, qr_s, ke_s, tot_s, acc_ref)
            return carry

        def recur_step(i, carry, vv=vv, n_chunks=n_chunks):
            _gla_recur((i, n_chunks - 1 - i), vv, qr_s, ke_s, tot_s, st_refs, acc_ref)
            return carry

        lax.fori_loop(0, n_chunks, local_step, 0, unroll=2)
        lax.fori_loop(0, n_chunks, recur_step, 0)
        _gla_finish(acc_ref, zz_ref, norm_ref, out_ref, n)


def gla_branch(p_main, p_small, pc_main, pc_small, wg2, bg, norm):
    b, t, _ = p_main.shape
    tc = pc_main.shape[1]
    nj = GLA_HEADS // GLA_PAIR
    vw = GLA_PAIR * GLA_DV
    o_q = MY_OFF["gla_q"][0] // LANE
    o_k = MY_OFF["gla_k"][0] // LANE
    o_v = MY_OFF["gla_v"][0] // vw
    o_z = MY_OFF["gla_z"][0] // vw
    sums, masks = _gla_tables()
    cos, sin = _gla_rope_tables(t)
    wge, bge = _gla_gate_weights(wg2.astype(F32), bg.astype(F32))

    def seq(n, w, o):
        return pl.BlockSpec((1, n, w), lambda i, j: (i, 0, o + j))

    def whole(n):
        return pl.BlockSpec((1, n, N_SMALL), lambda i, j: (i, 0, 0))

    def const(shape):
        return pl.BlockSpec(shape, lambda i, j: (0,) * len(shape))

    return pl.pallas_call(
        _gla_kernel,
        grid=(b, nj),
        in_specs=[seq(t, LANE, o_q), seq(t, LANE, o_k), seq(t, vw, o_v), seq(t, vw, o_z), whole(t),
                  seq(tc, LANE, o_q), seq(tc, LANE, o_k), seq(tc, vw, o_v), seq(tc, vw, o_z), whole(tc),
                  const((t, LANE)), const((t, LANE)),
                  pl.BlockSpec((2, 1, N_SMALL, LANE), lambda i, j: (0, j, 0, 0)),
                  pl.BlockSpec((2, 1, 1, LANE), lambda i, j: (0, j, 0, 0)),
                  const((1, GLA_DV)), const((2, GLA_SUM_ROWS, CHUNK)),
                  const((2, GLA_LEVELS + 1, GLA_PAIR * CHUNK, CHUNK))],
        out_specs=[pl.BlockSpec((1, t, vw), lambda i, j: (i, 0, j)), pl.BlockSpec((1, tc, vw), lambda i, j: (i, 0, j))],
        out_shape=[jax.ShapeDtypeStruct((b, t, BR_W), BF16), jax.ShapeDtypeStruct((b, tc, BR_W), BF16)],
        scratch_shapes=[pltpu.VMEM((vw, LANE), F32), pltpu.VMEM((vw, LANE), F32),
                        pltpu.VMEM((2, t, LANE), BF16), pltpu.VMEM((2, t, LANE), BF16),
                        pltpu.VMEM((2, t // CHUNK, LANE), F32),
                        pltpu.VMEM((t, vw), F32)],
        compiler_params=pltpu.CompilerParams(vmem_limit_bytes=VMEM_LIMIT),
        name="gla_scan",
    )(p_main, p_main, p_main, p_main, p_small, pc_main, pc_main, pc_main, pc_main, pc_small,
      jnp.asarray(cos), jnp.asarray(sin), wge, bge, norm.astype(F32).reshape(1, GLA_DV),
      jnp.asarray(sums, BF16), jnp.asarray(masks))


GDN_PAIR = 2 * CHUNK
GDN_STACK = 2 * GDN_PAIR


def _neumann(a):
    r = -a
    p = a
    for _ in range(int(math.log2(CHUNK)) - 1):
        p16 = p.astype(BF16)
        p = jnp.dot(p16, p16, preferred_element_type=F32)
        r = r + p + jnp.dot(r.astype(BF16), p.astype(BF16), preferred_element_type=F32)
    return r


def _seg_scan(x, reverse):
    n = x.shape[0]
    pos = lax.broadcasted_iota(jnp.int32, x.shape, 0) % CHUNK
    s = 1
    while s < CHUNK:
        if reverse:
            x = x + jnp.where(pos < CHUNK - s, pltpu.roll(x, n - s, axis=0), 0.0)
        else:
            x = x + jnp.where(pos >= s, pltpu.roll(x, s, axis=0), 0.0)
        s *= 2
    return x


def _gdn_prepare(q_ref, k_ref, v_ref, gs_ref, wq_ref, wk_ref, wv_ref, sel_ref, coef_ref, qkv_s, gb_s):
    hp = lax.Precision.HIGHEST
    n = q_ref.shape[1]
    for i, (x_ref, w_ref) in enumerate(((q_ref, wq_ref), (k_ref, wk_ref), (v_ref, wv_ref))):
        y = _dwconv_rows(x_ref[0], w_ref, GDN_CONV)
        y = y * jax.nn.sigmoid(y)
        if i < 2:
            y = y * lax.rsqrt(jnp.sum(y * y, axis=-1, keepdims=True) + EPS)
        if i == 0:
            y = y * (GDN_DH ** -0.5)
        qkv_s[i, 0:n, :] = y
    gs = gs_ref[0]
    for d in range(2):
        a = jnp.dot(gs, sel_ref[0, d], precision=hp, preferred_element_type=F32)
        bb = jnp.dot(gs, sel_ref[0, 2 + d], precision=hp, preferred_element_type=F32)
        g = -jnp.exp(coef_ref[0, d:d + 1, :]) * _softplus(a + coef_ref[0, 2 + d:3 + d, :])
        gb_s[d, 0:n, :] = _seg_scan(g, reverse=(d == 1))
        gb_s[2 + d, 0:n, :] = jax.nn.sigmoid(bb)
        gb_s[4 + d, 0:n, :] = jnp.exp(_seg_scan(g, reverse=(d == 0)) - g)


def _gdn_local(p, qkv_s, gb_s, negm_ref, u_s, w_s, aqk_s, qd_s, kd_s):
    rows = pl.ds(pl.multiple_of(p * GDN_PAIR, GDN_PAIR), GDN_PAIR)
    q = qkv_s[0, rows, :]
    k = qkv_s[1, rows, :]
    v = qkv_s[2, rows, :]
    gam = [gb_s[d, rows, :] for d in range(2)]
    beta = [gb_s[2 + d, rows, :] for d in range(2)]
    egam = [jnp.exp(g) for g in gam]
    kb = [k * bt for bt in beta]
    k16 = k.astype(BF16)
    raw = _nt_dot(jnp.concatenate([kb[0], kb[1], q, q], axis=0).astype(BF16),
                  jnp.concatenate([k16, k16], axis=0))
    g2 = jnp.concatenate(gam, axis=0)
    g2t = g2.T
    diff = jnp.concatenate([g2, g2], axis=1) - jnp.concatenate([g2t, g2t], axis=0)
    dec = jnp.exp(jnp.minimum(diff, 0.0) + negm_ref[...])
    eye = (lax.broadcasted_iota(jnp.int32, dec.shape, 0) == lax.broadcasted_iota(jnp.int32, dec.shape, 1)).astype(F32)
    a_kk = raw[0:GDN_STACK] * dec
    a_qk = raw[GDN_STACK:] * (dec + eye)
    r = _neumann(a_kk)
    rhs = jnp.concatenate([jnp.concatenate([v * beta[d], kb[d] * egam[d]], axis=1) for d in range(2)], axis=0)
    sol = rhs + jnp.dot(r.astype(BF16), rhs.astype(BF16), preferred_element_type=F32)
    for d in range(2):
        blk = slice(d * GDN_PAIR, (d + 1) * GDN_PAIR)
        u_s[d, rows, :] = sol[blk, 0:GDN_DH]
        w_s[d, rows, :] = sol[blk, GDN_DH:].astype(BF16)
        aqk_s[d, rows, :] = a_qk[blk, blk].astype(BF16)
        qd_s[d, rows, :] = (q * egam[d]).astype(BF16)
        kd_s[d, rows, :] = (k * gb_s[4 + d, rows, :]).astype(BF16)


def _gdn_recur(chunks, gb_s, u_s, w_s, aqk_s, qd_s, kd_s, st_refs, acc_ref):
    outs = []
    for d, c in enumerate(chunks):
        rows = pl.ds(pl.multiple_of(c * CHUNK, CHUNK), CHUNK)
        st = st_refs[d][...]
        st16 = st.astype(BF16)
        ws = jnp.dot(jnp.concatenate([w_s[d, rows, :], qd_s[d, rows, :]], axis=0), st16, preferred_element_type=F32)
        e16 = (u_s[d, rows, :] - ws[0:CHUNK]).astype(BF16)
        outs.append((rows, ws[CHUNK:] + jnp.dot(aqk_s[d, rows, :], jnp.concatenate([e16, e16], axis=0),
                                                preferred_element_type=F32)))
        last = c * CHUNK + (0 if d else CHUNK - 1)
        st_refs[d][...] = st * jnp.exp(gb_s[d, pl.ds(last, 1), :]) + _tn_dot(kd_s[d, rows, :], e16)
    for rows, o in outs:
        acc_ref[rows, :] += o


def _gdn_finish(acc_ref, z_ref, norm_ref, o_ref, n_rows):
    tile = min(256, n_rows)
    for i in range(n_rows // tile):
        rows = slice(i * tile, (i + 1) * tile)
        z = z_ref[0, rows, :]
        o_ref[0, rows, :] = (_rms_rows(acc_ref[rows, :], norm_ref[...]) * (z * jax.nn.sigmoid(z))).astype(o_ref.dtype)


def _gdn_kernel(q_ref, k_ref, v_ref, z_ref, gs_ref, qc_ref, kc_ref, vc_ref, zc_ref, gsc_ref, wq_ref, wk_ref, wv_ref,
                sel_ref, coef_ref, norm_ref, negm_ref, o_ref, oc_ref, stf_ref, stb_ref, qkv_s, gb_s,
                u_s, w_s, aqk_s, qd_s, kd_s, acc_ref):
    st_refs = (stf_ref, stb_ref)
    for r in st_refs:
        r[...] = jnp.zeros_like(r)
    for refs, zz_ref, out_ref in (((qc_ref, kc_ref, vc_ref, gsc_ref), zc_ref, oc_ref),
                                  ((q_ref, k_ref, v_ref, gs_ref), z_ref, o_ref)):
        n = refs[0].shape[1]
        n_chunks = n // CHUNK
        _gdn_prepare(*refs, wq_ref, wk_ref, wv_ref, sel_ref, coef_ref, qkv_s, gb_s)
        acc_ref[0:n, :] = jnp.zeros((n, LANE), F32)

        def local_step(p, carry):
            _gdn_local(p, qkv_s, gb_s, negm_ref, u_s, w_s, aqk_s, qd_s, kd_s)
            return carry

        def recur_step(i, carry, n_chunks=n_chunks):
            _gdn_recur((i, n_chunks - 1 - i), gb_s, u_s, w_s, aqk_s, qd_s, kd_s, st_refs, acc_ref)
            return carry

        lax.fori_loop(0, n // GDN_PAIR, local_step, 0, unroll=2)
        lax.fori_loop(0, n_chunks, recur_step, 0)
        _gdn_finish(acc_ref, zz_ref, norm_ref, out_ref, n)


def _gdn_order_mask():
    ti = np.arange(GDN_STACK)[:, None]
    si = np.arange(GDN_STACK)[None, :]
    same = (ti // CHUNK) == (si // CHUNK)
    earlier = same & np.where(ti < GDN_PAIR, si < ti, si > ti)
    return np.where(earlier, 0.0, NA_NEG).astype(np.float32)


def gdn_branch(p_main, p_small, pc_main, pc_small, conv_w, a_log, dt_bias, norm):
    b, t, _ = p_main.shape
    tc = pc_main.shape[1]
    assert t % GDN_PAIR == 0 and tc % GDN_PAIR == 0 and tc <= t
    nh = GDN_HEADS
    o_q = MY_OFF["gdn_qkv"][0] // LANE
    o_z = MY_OFF["gdn_z"][0] // LANE
    wt = conv_w.T.astype(F32)
    sel = np.zeros((nh, 4, N_SMALL, LANE), np.float32)
    for h in range(nh):
        for d in range(2):
            sel[h, d, MY_OFF["gdn_a"][0] + d * nh + h, :] = 1.0
            sel[h, 2 + d, MY_OFF["gdn_b"][0] + d * nh + h, :] = 1.0
    coef = jnp.concatenate([a_log.astype(F32).T, dt_bias.astype(F32).T], axis=1)
    coef = jnp.broadcast_to(coef[:, :, None], (nh, 4, LANE))

    def seq(n, o):
        return pl.BlockSpec((1, n, LANE), lambda i, j: (i, 0, o + j))

    def whole(n):
        return pl.BlockSpec((1, n, N_SMALL), lambda i, j: (i, 0, 0))

    def wspec(o):
        return pl.BlockSpec((GDN_CONV, LANE), lambda i, j: (0, o + j))

    return pl.pallas_call(
        _gdn_kernel,
        grid=(b, nh),
        in_specs=[seq(t, o_q), seq(t, o_q + nh), seq(t, o_q + 2 * nh), seq(t, o_z), whole(t),
                  seq(tc, o_q), seq(tc, o_q + nh), seq(tc, o_q + 2 * nh), seq(tc, o_z), whole(tc),
                  wspec(0), wspec(nh), wspec(2 * nh),
                  pl.BlockSpec((1, 4, N_SMALL, LANE), lambda i, j: (j, 0, 0, 0)),
                  pl.BlockSpec((1, 4, LANE), lambda i, j: (j, 0, 0)),
                  pl.BlockSpec((1, GDN_DH), lambda i, j: (0, 0)),
                  pl.BlockSpec((GDN_STACK, GDN_STACK), lambda i, j: (0, 0))],
        out_specs=[pl.BlockSpec((1, t, LANE), lambda i, j: (i, 0, j)), pl.BlockSpec((1, tc, LANE), lambda i, j: (i, 0, j))],
        out_shape=[jax.ShapeDtypeStruct((b, t, BR_W), BF16), jax.ShapeDtypeStruct((b, tc, BR_W), BF16)],
        scratch_shapes=[pltpu.VMEM((GDN_DH, GDN_DH), F32), pltpu.VMEM((GDN_DH, GDN_DH), F32),
                        pltpu.VMEM((3, t, LANE), F32), pltpu.VMEM((6, t, LANE), F32),
                        pltpu.VMEM((2, t, LANE), F32),
                        pltpu.VMEM((2, t, LANE), BF16), pltpu.VMEM((2, t, LANE), BF16),
                        pltpu.VMEM((2, t, LANE), BF16), pltpu.VMEM((2, t, LANE), BF16),
                        pltpu.VMEM((t, LANE), F32)],
        compiler_params=pltpu.CompilerParams(vmem_limit_bytes=VMEM_LIMIT),
        name="gdn_scan",
    )(p_main, p_main, p_main, p_main, p_small, pc_main, pc_main, pc_main, pc_main, pc_small,
      wt, wt, wt, jnp.asarray(sel), coef, norm.astype(F32).reshape(1, GDN_DH), jnp.asarray(_gdn_order_mask()))


HY_COLS = 256
HY_ROWS = 512


def dft_matrices(length):
    n = 2 * length
    r = jnp.arange(n, dtype=jnp.int32)[:, None]
    t = jnp.arange(length, dtype=jnp.int32)[None, :]
    k = jnp.where(r <= length, r, r - length)
    ang = ((k * t) % n).astype(F32) * (2.0 * math.pi / n)
    f = jnp.where(r <= length, jnp.cos(ang), -jnp.sin(ang))
    w = jnp.where((r == 0) | (r == length), 1.0, 2.0) / n
    return f.astype(BF16), (f * w).T.astype(BF16)


def _hy_embedding(length):
    t = np.linspace(0.0, 1.0, length)[:, None]
    bands = (HY_EMB - 1) // 2
    wpos = 2.0 * math.pi * np.arange(length)[:, None] / length
    f = np.linspace(1e-4, bands - 1, bands)[None]
    z = np.concatenate([t, np.cos(f * wpos), -np.sin(f * wpos)], axis=-1)
    z = np.pad(z, ((0, 0), (0, HY_FFN - HY_EMB)))
    deltas = np.abs(np.linspace(math.log(HY_TARGET) / HY_SLOW, math.log(HY_TARGET) / HY_FAST, BR_W))[None]
    return z.astype(np.float32), deltas.astype(np.float32)


def _hy_filter_kernel(z_ref, win_ref, bin_ref, wmid_ref, bmid_ref, freq_ref, wout_ref, delta_ref, o_ref):
    hp = lax.Precision.HIGHEST
    length = z_ref.shape[0]
    h = jnp.sin(freq_ref[0] * (jnp.dot(z_ref[...], win_ref[...], precision=hp, preferred_element_type=F32)
                               + bin_ref[...]))
    for i in range(HY_INNER):
        h = jnp.sin(freq_ref[i + 1] * (jnp.dot(h, wmid_ref[i], precision=hp, preferred_element_type=F32)
                                       + bmid_ref[i]))
    h = jnp.dot(h, wout_ref[...], precision=hp, preferred_element_type=F32)
    t = lax.broadcasted_iota(jnp.int32, (length, BR_W), 0).astype(F32) * (1.0 / (length - 1))
    decay = jnp.exp(-t * delta_ref[...])
    h_f = h[:, :BR_W] * decay
    h_b = h[:, BR_W:] * decay
    for i, a in enumerate((h_f + h_b, h_f - h_b)):
        hi = a.astype(BF16)
        lo = (a - hi.astype(F32)).astype(BF16)
        o_ref[:, (2 * i) * BR_W:(2 * i + 1) * BR_W] = hi
        o_ref[:, (2 * i + 1) * BR_W:(2 * i + 2) * BR_W] = lo


def _hy_spectrum_kernel(f_ref, a_ref, ks_ref, kd_ref, *, length):
    i = pl.program_id(0)
    tm = f_ref.shape[0]
    r = jnp.dot(f_ref[...], a_ref[...], preferred_element_type=F32)
    ks_ref[...] = r[:, :BR_W] + r[:, BR_W:2 * BR_W]
    kd = r[:, 2 * BR_W:3 * BR_W] + r[:, 3 * BR_W:]
    row = lax.broadcasted_iota(jnp.int32, kd.shape, 0) + i * tm
    kd_ref[...] = jnp.where(row == length, 0.0, kd)


def hyena_filter_spectrum(length, fwd, filt):
    w_in, b_in, w_mid, b_mid, freq, w_out = filt
    z, deltas = _hy_embedding(length)
    n = 2 * length
    parts = pl.pallas_call(
        _hy_filter_kernel,
        out_shape=jax.ShapeDtypeStruct((length, 4 * BR_W), BF16),
        compiler_params=pltpu.CompilerParams(vmem_limit_bytes=VMEM_LIMIT),
        name="hyena_filter",
    )(jnp.asarray(z), jnp.pad(w_in, ((0, HY_FFN - HY_EMB), (0, 0))), b_in.reshape(1, HY_FFN), w_mid,
      b_mid.reshape(HY_INNER, 1, HY_FFN), freq.reshape(HY_INNER + 1, 1, HY_FFN), w_out, jnp.asarray(deltas))
    tm = min(HY_ROWS, n)
    return pl.pallas_call(
        functools.partial(_hy_spectrum_kernel, length=length),
        grid=(n // tm,),
        in_specs=[pl.BlockSpec((tm, length), lambda i: (i, 0)),
                  pl.BlockSpec((length, 4 * BR_W), lambda i: (0, 0))],
        out_specs=[pl.BlockSpec((tm, BR_W), lambda i: (i, 0)), pl.BlockSpec((tm, BR_W), lambda i: (i, 0))],
        out_shape=[jax.ShapeDtypeStruct((n, BR_W), F32), jax.ShapeDtypeStruct((n, BR_W), F32)],
        compiler_params=pltpu.CompilerParams(vmem_limit_bytes=VMEM_LIMIT),
        name="hyena_spectrum",
    )(fwd, parts)


def _shift_rows(x, off):
    t = x.shape[0]
    rolled = pltpu.roll(x, (-off) % t, axis=0)
    row = lax.broadcasted_iota(jnp.int32, x.shape, 0)
    valid = (row >= -off) if off < 0 else (row < t - off)
    return jnp.where(valid, rolled, 0.0)


def _dwconv_rows(x, w_ref, k):
    out = x * w_ref[k // 2:k // 2 + 1, :]
    for j in range(k):
        if j != k // 2:
            out = out + _shift_rows(x, j - k // 2) * w_ref[j:j + 1, :]
    return out


def _hy_prep_kernel(x0_ref, x1_ref, v_ref, z_ref, w0_ref, w1_ref, wv_ref, b0_ref, b1_ref, bv_ref, vx_ref, x0g_ref):
    x0 = _dwconv_rows(x0_ref[0], w0_ref, HY_CONV) + b0_ref[...]
    x1 = _dwconv_rows(x1_ref[0], w1_ref, HY_CONV) + b1_ref[...]
    v = _dwconv_rows(v_ref[0], wv_ref, HY_CONV) + bv_ref[...]
    z = z_ref[0]
    vx_ref[0] = v * x1
    x0g_ref[0] = x0 * (z * jax.nn.sigmoid(z))


def hyena_prep(p_main, conv_w, conv_b):
    b, length, _ = p_main.shape
    cb = LANE
    nj = BR_W // cb
    o_xv = MY_OFF["hy_xv"][0] // cb
    o_z = MY_OFF["hy_z"][0] // cb
    wt = conv_w.T
    bt = conv_b.reshape(1, 3 * BR_W)
    dspec = lambda o: pl.BlockSpec((1, length, cb), lambda i, j: (i, 0, o + j))
    wspec = lambda o: pl.BlockSpec((HY_CONV, cb), lambda i, j: (0, o + j))
    bspec = lambda o: pl.BlockSpec((1, cb), lambda i, j: (0, o + j))
    ospec = pl.BlockSpec((1, length, cb), lambda i, j: (i, 0, j))
    return pl.pallas_call(
        _hy_prep_kernel,
        grid=(b, nj),
        in_specs=[dspec(o_xv), dspec(o_xv + nj), dspec(o_xv + 2 * nj), dspec(o_z),
                  wspec(0), wspec(nj), wspec(2 * nj), bspec(0), bspec(nj), bspec(2 * nj)],
        out_specs=[ospec, ospec],
        out_shape=[jax.ShapeDtypeStruct((b, length, BR_W), F32)] * 2,
        compiler_params=pltpu.CompilerParams(vmem_limit_bytes=VMEM_LIMIT),
        name="hyena_prep",
    )(p_main, p_main, p_main, p_main, wt, wt, wt, bt, bt, bt)


def _hy_fwd_kernel(vx_ref, f_ref, kr_ref, ki_ref, knyq_ref, y_ref, *, length):
    v = vx_ref[0].astype(BF16)
    tm = min(HY_ROWS, length)
    for kt in range(length // tm):
        lo = kt * tm
        a = jnp.dot(f_ref[lo:lo + tm, :], v, preferred_element_type=F32)
        bh = jnp.dot(f_ref[length + lo:length + lo + tm, :], v, preferred_element_type=F32)
        kr = kr_ref[lo:lo + tm, :]
        ki = ki_ref[lo:lo + tm, :]
        if kt == 0:
            row = lax.broadcasted_iota(jnp.int32, kr.shape, 0)
            kb = jnp.where(row == 0, knyq_ref[0:1, :], kr)
        else:
            kb = kr
        y_ref[0, lo:lo + tm, :] = (a * kr - bh * ki).astype(BF16)
        y_ref[0, length + lo:length + lo + tm, :] = (a * ki + bh * kb).astype(BF16)


def _hy_inv_kernel(y_ref, inv_ref, vx_ref, x0g_ref, skip_ref, o_ref, *, length):
    tm = min(HY_ROWS, length)
    y = y_ref[0]
    for tt in range(length // tm):
        lo = tt * tm
        conv = jnp.dot(inv_ref[lo:lo + tm, :], y, preferred_element_type=F32)
        o_ref[0, lo:lo + tm, :] = ((conv + skip_ref[...] * vx_ref[0, lo:lo + tm, :])
                                   * x0g_ref[0, lo:lo + tm, :]).astype(o_ref.dtype)


def hyena_branch(p_main, conv_w, conv_b, skip, fwd, inv, ks, kd):
    b, length, _ = p_main.shape
    n = 2 * length
    vx, x0g = hyena_prep(p_main, conv_w, conv_b)
    cb = HY_COLS
    nj = BR_W // cb
    resident = lambda shape: pl.BlockSpec(shape, lambda j, i: (0, 0), pipeline_mode=pl.Buffered(1))
    y = pl.pallas_call(
        functools.partial(_hy_fwd_kernel, length=length),
        grid=(nj, b),
        in_specs=[pl.BlockSpec((1, length, cb), lambda j, i: (i, 0, j)),
                  resident((n, length)),
                  pl.BlockSpec((length, cb), lambda j, i: (0, j)),
                  pl.BlockSpec((length, cb), lambda j, i: (1, j)),
                  pl.BlockSpec((8, cb), lambda j, i: (length // 8, j))],
        out_specs=pl.BlockSpec((1, n, cb), lambda j, i: (i, 0, j)),
        out_shape=jax.ShapeDtypeStruct((b, n, BR_W), BF16),
        compiler_params=pltpu.CompilerParams(vmem_limit_bytes=VMEM_LIMIT),
        name="hyena_dft",
    )(vx, fwd, ks, kd, ks)
    return pl.pallas_call(
        functools.partial(_hy_inv_kernel, length=length),
        grid=(nj, b),
        in_specs=[pl.BlockSpec((1, n, cb), lambda j, i: (i, 0, j)),
                  resident((length, n)),
                  pl.BlockSpec((1, length, cb), lambda j, i: (i, 0, j)),
                  pl.BlockSpec((1, length, cb), lambda j, i: (i, 0, j)),
                  pl.BlockSpec((1, cb), lambda j, i: (0, j))],
        out_specs=pl.BlockSpec((1, length, cb), lambda j, i: (i, 0, j)),
        out_shape=jax.ShapeDtypeStruct((b, length, BR_W), BF16),
        compiler_params=pltpu.CompilerParams(vmem_limit_bytes=VMEM_LIMIT),
        name="hyena_idft",
    )(y, inv, vx, x0g, skip.reshape(1, BR_W))


def mixers(p_main, p_small, pc_main, pc_small, na_rpb, gla_wg2, gla_bg, gla_norm, gdn_conv, gdn_a_log, gdn_dt_bias,
           gdn_norm, hy_conv, hy_conv_b, hy_filt, hy_skip, with_ctx, dft_lat, dft_ctx):
    t = p_main.shape[1]
    ya = na_branch(p_main, pc_main, na_rpb)
    yb, yb_c = gla_branch(p_main, p_small, pc_main, pc_small, gla_wg2, gla_bg, gla_norm)
    yc, yc_c = gdn_branch(p_main, p_small, pc_main, pc_small, gdn_conv, gdn_a_log, gdn_dt_bias, gdn_norm)
    fwd, inv = dft_lat
    ks, kd = hyena_filter_spectrum(t, fwd, hy_filt)
    yd = hyena_branch(p_main, hy_conv, hy_conv_b, hy_skip, fwd, inv, ks, kd)
    if not with_ctx:
        return (ya, yb, yc, yd), None
    ya_c = ctx_attention_branch(pc_main)
    fwd_c, inv_c = dft_ctx
    ks_c, kd_c = hyena_filter_spectrum(pc_main.shape[1], fwd_c, hy_filt)
    yd_c = hyena_branch(pc_main, hy_conv, hy_conv_b, hy_skip, fwd_c, inv_c, ks_c, kd_c)
    return (ya, yb, yc, yd), (ya_c, yb_c, yc_c, yd_c)


def kernel(x, c, ctx, c_ctx, w_mod, b_mod, g_pre, g_post, w_in, w_gate, b_gate, w_branch, w_out, na_rpb, gla_wg2, gla_bg, gla_norm, gdn_conv, gdn_a_log, gdn_dt_bias, gdn_norm, hy_conv, hy_conv_b, hy_w_in, hy_b_in, hy_w_mid, hy_b_mid, hy_freq, hy_w_out, hy_skip):
    b, t, d = x.shape
    tc = ctx.shape[1]
    h = x.reshape(b * t, d)
    hc = ctx.reshape(b * tc, d)
    c_all = jnp.concatenate([c, c_ctx[None], jnp.zeros((7, d), F32)], axis=0)
    dft_lat = dft_matrices(t)
    dft_ctx = dft_matrices(tc)
    for l in range(DEPTH):
        with_ctx = l < DEPTH - 1
        mod = modulation(c_all, w_mod[l], b_mod[l])
        shift, scale, gate = [mod[:, i * d:(i + 1) * d].reshape(-1, 1, d) for i in range(3)]
        u = prenorm_mod(h, g_pre[l], scale[:b], shift[:b], t)
        uc = prenorm_mod(hc, g_pre[l], scale[b:b + 1], shift[b:b + 1], b * tc)
        w_main, w_small = _permute_w_in(w_in[l])
        p_main = matmul(u, w_main, 1024, 1536, name="in_proj").reshape(b, t, N_MAIN)
        p_small = matmul(u, w_small, 1024, N_SMALL, name="in_proj_small").reshape(b, t, N_SMALL)
        pc_main = matmul(uc, w_main, 1024, 1536, name="in_proj_ctx").reshape(b, tc, N_MAIN)
        pc_small = matmul(uc, w_small, 1024, N_SMALL, name="in_proj_ctx_small").reshape(b, tc, N_SMALL)
        ys, ycs = mixers(p_main, p_small, pc_main, pc_small,
                         na_rpb[l], gla_wg2[l], gla_bg[l], gla_norm[l], gdn_conv[l], gdn_a_log[l], gdn_dt_bias[l],
                         gdn_norm[l], hy_conv[l], hy_conv_b[l],
                         (hy_w_in[l], hy_b_in[l], hy_w_mid[l], hy_b_mid[l], hy_freq[l], hy_w_out[l]),
                         hy_skip[l], with_ctx, dft_lat, dft_ctx)
        h_new = merge(u, ys, w_gate[l], b_gate[l], w_branch[l], w_out[l], h, gate[:b], g_post[l], t)
        if with_ctx:
            hc = merge(uc, ycs, w_gate[l], b_gate[l], w_branch[l], w_out[l], hc, gate[b:b + 1], g_post[l], b * tc)
        h = h_new
    return h.reshape(b, t, d)
```

```python
import functools
import math

import numpy as np
import jax
import jax.numpy as jnp
from jax import lax
from jax.experimental import pallas as pl
from jax.experimental.pallas import tpu as pltpu

D_MODEL = 2048
BATCH = 16
SEQ = 2048
DEPTH = 2
GRID_W = 64
CTX_LEN = 256
N_BRANCH = 4
BR_W = D_MODEL // N_BRANCH
NA_HEADS = 8
NA_DH = BR_W // NA_HEADS
WIN_R = 8
WIN_C = 16
QBLK_C = 16
KBLK_C = WIN_C + QBLK_C
N_CBLK = GRID_W // QBLK_C
GLA_HEADS = 4
GLA_DV = BR_W // GLA_HEADS
GLA_DK = GLA_DV // 2
GLA_RANK = 16
GLA_TAU = 16.0
CHUNK = 64
GDN_HEADS = 4
GDN_DH = BR_W // GDN_HEADS
GDN_CONV = 5
HY_CONV = 3
HY_EMB = 33
HY_FFN = 64
HY_INNER = 2
HY_TARGET = 1e-2
HY_FAST = 0.3
HY_SLOW = 1.5
ROPE_THETA = 10000.0
EPS = 1e-6
F32 = jnp.float32
BF16 = jnp.bfloat16

REF_SPLITS = (
    ("na_qkv", 3 * BR_W), ("na_z", BR_W),
    ("gla_q", GLA_HEADS * GLA_DK), ("gla_k", GLA_HEADS * GLA_DK), ("gla_v", BR_W), ("gla_z", BR_W),
    ("gla_g", 2 * GLA_RANK),
    ("gdn_qkv", 3 * BR_W), ("gdn_z", BR_W), ("gdn_a", 2 * GDN_HEADS), ("gdn_b", 2 * GDN_HEADS),
    ("hy_xv", 3 * BR_W), ("hy_z", BR_W),
)
MAIN_ORDER = ("na_qkv", "na_z", "gla_q", "gla_k", "gla_v", "gla_z", "gdn_qkv", "gdn_z", "hy_xv", "hy_z")
SMALL_ORDER = ("gla_g", "gdn_a", "gdn_b")
LANE = 128
N_MAIN = sum(w for n, w in REF_SPLITS if n in MAIN_ORDER)
N_SMALL = LANE
VMEM_LIMIT = 48 * 1024 * 1024
MERGE_VMEM_LIMIT = 56 * 1024 * 1024


def _ref_offsets():
    out, o = {}, 0
    for name, w in REF_SPLITS:
        out[name] = (o, w)
        o += w
    return out


def _my_offsets():
    out, o = {}, 0
    for name in MAIN_ORDER:
        w = dict(REF_SPLITS)[name]
        out[name] = (o, w)
        o += w
    o = 0
    for name in SMALL_ORDER:
        w = dict(REF_SPLITS)[name]
        out[name] = (o, w)
        o += w
    return out


REF_OFF = _ref_offsets()
MY_OFF = _my_offsets()


def _permute_w_in(w_in_l):
    main = jnp.concatenate([w_in_l[:, REF_OFF[n][0]:REF_OFF[n][0] + REF_OFF[n][1]] for n in MAIN_ORDER], axis=1)
    small = jnp.concatenate([w_in_l[:, REF_OFF[n][0]:REF_OFF[n][0] + REF_OFF[n][1]] for n in SMALL_ORDER], axis=1)
    small = jnp.pad(small, ((0, 0), (0, N_SMALL - small.shape[1])))
    return main.astype(BF16), small.astype(BF16)


def _mod_kernel(c_ref, w_ref, b_ref, o_ref):
    c = c_ref[...]
    a = (c * jax.nn.sigmoid(c)).astype(BF16)
    o_ref[...] = jnp.dot(a, w_ref[...].astype(BF16), preferred_element_type=F32) + b_ref[...]


def modulation(c_all, w_mod_l, b_mod_l):
    r, d = c_all.shape
    n = w_mod_l.shape[1]
    tn = 768
    return pl.pallas_call(
        _mod_kernel,
        grid=(n // tn,),
        in_specs=[pl.BlockSpec((r, d), lambda j: (0, 0)),
                  pl.BlockSpec((d, tn), lambda j: (0, j)),
                  pl.BlockSpec((1, tn), lambda j: (0, j))],
        out_specs=pl.BlockSpec((r, tn), lambda j: (0, j)),
        out_shape=jax.ShapeDtypeStruct((r, n), F32),
        compiler_params=pltpu.CompilerParams(vmem_limit_bytes=VMEM_LIMIT),
        name="modulation",
    )(c_all, w_mod_l, b_mod_l.reshape(1, n))


def _prenorm_kernel(x_ref, g_ref, scale_ref, shift_ref, o_ref):
    x = x_ref[...]
    y = x * lax.rsqrt(jnp.mean(x * x, axis=-1, keepdims=True) + EPS)
    u = (y * g_ref[...]) * (1.0 + scale_ref[0]) + shift_ref[0]
    o_ref[...] = u.astype(o_ref.dtype)


def prenorm_mod(x2d, g_pre_l, scale, shift, rows_per_group, out_dtype=BF16):
    n, d = x2d.shape
    tm = 512
    tiles_per_group = rows_per_group // tm
    return pl.pallas_call(
        _prenorm_kernel,
        grid=(n // tm,),
        in_specs=[pl.BlockSpec((tm, d), lambda i: (i, 0)),
                  pl.BlockSpec((1, d), lambda i: (0, 0)),
                  pl.BlockSpec((1, 1, d), lambda i: (i // tiles_per_group, 0, 0)),
                  pl.BlockSpec((1, 1, d), lambda i: (i // tiles_per_group, 0, 0))],
        out_specs=pl.BlockSpec((tm, d), lambda i: (i, 0)),
        out_shape=jax.ShapeDtypeStruct((n, d), out_dtype),
        compiler_params=pltpu.CompilerParams(vmem_limit_bytes=VMEM_LIMIT),
        name="prenorm_mod",
    )(x2d, g_pre_l.reshape(1, d), scale, shift)


def _matmul_kernel(u_ref, w_ref, o_ref):
    o_ref[...] = jnp.dot(u_ref[...], w_ref[...], preferred_element_type=F32).astype(o_ref.dtype)


def matmul(u, w, tm, tn, out_dtype=F32, name="matmul"):
    n, k = u.shape
    m = w.shape[1]
    return pl.pallas_call(
        _matmul_kernel,
        grid=(m // tn, n // tm),
        in_specs=[pl.BlockSpec((tm, k), lambda j, i: (i, 0)),
                  pl.BlockSpec((k, tn), lambda j, i: (0, j))],
        out_specs=pl.BlockSpec((tm, tn), lambda j, i: (i, j)),
        out_shape=jax.ShapeDtypeStruct((n, m), out_dtype),
        compiler_params=pltpu.CompilerParams(vmem_limit_bytes=VMEM_LIMIT),
        name=name,
    )(u, w)


def _merge_kernel(u_ref, y0_ref, y1_ref, y2_ref, y3_ref, wg_ref, bg_ref, wb_ref, wo_ref, h_ref, gate_ref, gpost_ref,
                  o_ref, merged_ref):
    n = pl.program_id(1)
    tn = wg_ref.shape[2]
    u = u_ref[...]
    merged = None
    for i, y_ref in enumerate((y0_ref, y1_ref, y2_ref, y3_ref)):
        g = jnp.dot(u, wg_ref[i], preferred_element_type=F32) + bg_ref[i]
        yb = jnp.dot(y_ref[...], wb_ref[i], preferred_element_type=F32)
        term = jax.nn.sigmoid(g) * yb
        merged = term if merged is None else merged + term
    merged_ref[:, pl.ds(pl.multiple_of(n * tn, tn), tn)] = merged.astype(BF16)

    @pl.when(n == pl.num_programs(1) - 1)
    def _():
        y = jnp.dot(merged_ref[...], wo_ref[...], preferred_element_type=F32)
        yn = y * lax.rsqrt(jnp.mean(y * y, axis=-1, keepdims=True) + EPS) * gpost_ref[...]
        o_ref[...] = h_ref[...] + gate_ref[0] * yn


def merge(u, ys, w_gate_l, b_gate_l, w_branch_l, w_out_l, h2d, gate, g_post_l, rows_per_group):
    n, d = u.shape
    tm, tn = 512, 256
    tiles_per_group = rows_per_group // tm
    yspec = pl.BlockSpec((tm, BR_W), lambda i, j: (i, 0))
    return pl.pallas_call(
        _merge_kernel,
        grid=(n // tm, d // tn),
        in_specs=[pl.BlockSpec((tm, d), lambda i, j: (i, 0)),
                  yspec, yspec, yspec, yspec,
                  pl.BlockSpec((N_BRANCH, d, tn), lambda i, j: (0, 0, j)),
                  pl.BlockSpec((N_BRANCH, 1, tn), lambda i, j: (0, 0, j)),
                  pl.BlockSpec((N_BRANCH, BR_W, tn), lambda i, j: (0, 0, j)),
                  pl.BlockSpec((d, d), lambda i, j: (0, 0), pipeline_mode=pl.Buffered(1)),
                  pl.BlockSpec((tm, d), lambda i, j: (i, 0)),
                  pl.BlockSpec((1, 1, d), lambda i, j: (i // tiles_per_group, 0, 0)),
                  pl.BlockSpec((1, d), lambda i, j: (0, 0))],
        out_specs=pl.BlockSpec((tm, d), lambda i, j: (i, 0)),
        out_shape=jax.ShapeDtypeStruct((n, d), F32),
        scratch_shapes=[pltpu.VMEM((tm, d), BF16)],
        compiler_params=pltpu.CompilerParams(vmem_limit_bytes=MERGE_VMEM_LIMIT),
        name="merge",
    )(u, *[y.reshape(n, BR_W) for y in ys], w_gate_l.astype(BF16), b_gate_l.reshape(N_BRANCH, 1, d),
      w_branch_l.astype(BF16),
      w_out_l.astype(BF16), h2d, gate, g_post_l.reshape(1, d))


NA_NEG = -1e30
NA_PAIR = LANE // NA_DH


def na_bias_table(rpb, rows):
    assert rows >= WIN_R
    heads = rpb.shape[0]
    c = np.arange(GRID_W)[:, None]
    kc = np.arange(GRID_W)[None, :]
    cstart = np.clip(c - WIN_C // 2, 0, GRID_W - WIN_C)
    valid = (kc >= cstart) & (kc < cstart + WIN_C)
    dc = np.clip(kc - c + WIN_C - 1, 0, 2 * WIN_C - 2)
    onehot = ((dc[None] == np.arange(2 * WIN_C - 1)[:, None, None]) & valid[None]).astype(np.float32)
    by_dr = jnp.einsum("hrd,dck->hrck", rpb.astype(F32), jnp.asarray(onehot), precision=lax.Precision.HIGHEST)
    by_dr = jnp.where(valid, by_dr, NA_NEG)
    tabs = [by_dr[:, WIN_R - 1 - off:2 * WIN_R - 1 - off].transpose(0, 2, 1, 3).reshape(heads, GRID_W, WIN_R * GRID_W)
            for off in range(WIN_R)]
    return jnp.stack(tabs, axis=1)


def _nt_dot(a, b):
    return lax.dot_general(a, b, (((1,), (1,)), ((), ())), preferred_element_type=F32)


def _na_kernel(q_ref, k_ref, v_ref, kc_ref, vc_ref, z_ref, bias_ref, o_ref, *, rows):
    lane = lax.broadcasted_iota(jnp.int32, (GRID_W, LANE), 1)
    head_mask = [(lane >= hh * NA_DH) & (lane < (hh + 1) * NA_DH) for hh in range(NA_PAIR)]
    kc = kc_ref[0].astype(BF16)
    vc = vc_ref[0].astype(BF16)
    win = WIN_R * GRID_W

    def body(r, carry):
        rs = jnp.clip(r - WIN_R // 2, 0, rows - WIN_R)
        off = r - rs
        q0 = pl.multiple_of(r * GRID_W, GRID_W)
        k0 = pl.multiple_of(rs * GRID_W, GRID_W)
        q = q_ref[0, pl.ds(q0, GRID_W), :] * (NA_DH ** -0.5)
        kw = k_ref[0, pl.ds(k0, win), :].astype(BF16)
        vw = v_ref[0, pl.ds(k0, win), :].astype(BF16)
        out = None
        for hh in range(NA_PAIR):
            qh = jnp.where(head_mask[hh], q, 0.0).astype(BF16)
            s = _nt_dot(qh, kw) + bias_ref[hh, off]
            sc = _nt_dot(qh, kc)
            m = jnp.maximum(jnp.max(s, axis=-1, keepdims=True), jnp.max(sc, axis=-1, keepdims=True))
            p = jnp.exp(s - m)
            pc = jnp.exp(sc - m)
            l = jnp.sum(p, axis=-1, keepdims=True) + jnp.sum(pc, axis=-1, keepdims=True)
            o = (jnp.dot(p.astype(BF16), vw, preferred_element_type=F32)
                 + jnp.dot(pc.astype(BF16), vc, preferred_element_type=F32)) / l
            out = o if out is None else jnp.where(head_mask[hh], o, out)
        z = z_ref[0, pl.ds(q0, GRID_W), :]
        o_ref[0, pl.ds(q0, GRID_W), :] = (out * (z * jax.nn.sigmoid(z))).astype(o_ref.dtype)
        return carry

    lax.fori_loop(0, rows, body, 0, unroll=4)


def na_branch(p_main, pc_main, rpb):
    b, t, _ = p_main.shape
    tc = pc_main.shape[1]
    rows = t // GRID_W
    nj = BR_W // LANE
    o_q = MY_OFF["na_qkv"][0] // LANE
    o_z = MY_OFF["na_z"][0] // LANE
    bias = na_bias_table(rpb, rows)
    lat = lambda o: pl.BlockSpec((1, t, LANE), lambda i, j: (i, 0, o + j))
    cx = lambda o: pl.BlockSpec((1, tc, LANE), lambda i, j: (i, 0, o + j))
    return pl.pallas_call(
        functools.partial(_na_kernel, rows=rows),
        grid=(b, nj),
        in_specs=[lat(o_q), lat(o_q + nj), lat(o_q + 2 * nj), cx(o_q + nj), cx(o_q + 2 * nj), lat(o_z),
                  pl.BlockSpec((NA_PAIR, WIN_R, GRID_W, WIN_R * GRID_W), lambda i, j: (j, 0, 0, 0))],
        out_specs=pl.BlockSpec((1, t, LANE), lambda i, j: (i, 0, j)),
        out_shape=jax.ShapeDtypeStruct((b, t, BR_W), BF16),
        compiler_params=pltpu.CompilerParams(vmem_limit_bytes=VMEM_LIMIT),
        name="na_attention",
    )(p_main, p_main, p_main, pc_main, pc_main, p_main, bias)


def _ctx_attn_kernel(q_ref, k_ref, v_ref, z_ref, o_ref):
    tc = q_ref.shape[1]
    lane = lax.broadcasted_iota(jnp.int32, (tc, LANE), 1)
    q = q_ref[0] * (NA_DH ** -0.5)
    k = k_ref[0].astype(BF16)
    v = v_ref[0].astype(BF16)
    out = None
    for hh in range(NA_PAIR):
        mask = (lane >= hh * NA_DH) & (lane < (hh + 1) * NA_DH)
        s = _nt_dot(jnp.where(mask, q, 0.0).astype(BF16), k)
        p = jnp.exp(s - jnp.max(s, axis=-1, keepdims=True))
        o = jnp.dot(p.astype(BF16), v, preferred_element_type=F32) / jnp.sum(p, axis=-1, keepdims=True)
        out = o if out is None else jnp.where(mask, o, out)
    z = z_ref[0]
    o_ref[0] = (out * (z * jax.nn.sigmoid(z))).astype(o_ref.dtype)


def ctx_attention_branch(pc_main):
    b, tc, _ = pc_main.shape
    nj = BR_W // LANE
    o_q = MY_OFF["na_qkv"][0] // LANE
    o_z = MY_OFF["na_z"][0] // LANE
    cx = lambda o: pl.BlockSpec((1, tc, LANE), lambda i, j: (i, 0, o + j))
    return pl.pallas_call(
        _ctx_attn_kernel,
        grid=(b, nj),
        in_specs=[cx(o_q), cx(o_q + nj), cx(o_q + 2 * nj), cx(o_z)],
        out_specs=pl.BlockSpec((1, tc, LANE), lambda i, j: (i, 0, j)),
        out_shape=jax.ShapeDtypeStruct((b, tc, BR_W), BF16),
        compiler_params=pltpu.CompilerParams(vmem_limit_bytes=VMEM_LIMIT),
        name="ctx_attention",
    )(pc_main, pc_main, pc_main, pc_main)


def _sum_rows(m_bf16, x):
    hi = x.astype(BF16)
    lo = (x - hi.astype(F32)).astype(BF16)
    return jnp.dot(m_bf16, hi, preferred_element_type=F32) + jnp.dot(m_bf16, lo, preferred_element_type=F32)


def _tn_dot(a, b):
    return lax.dot_general(a, b, (((0,), (0,)), ((), ())), preferred_element_type=F32)


def _softplus(x):
    return jnp.maximum(x, 0.0) + jnp.log1p(jnp.exp(-jnp.abs(x)))


def _rms_rows(o, g):
    return o * lax.rsqrt(jnp.mean(o * o, axis=-1, keepdims=True) + EPS) * g


GLA_LEVELS = 6
GLA_PAIR = LANE // GLA_DK
GLA_SUM_ROWS = (2 + GLA_LEVELS) * CHUNK


def _gla_tables():
    assert CHUNK == 1 << GLA_LEVELS
    t = np.arange(CHUNK)[:, None]
    r = np.arange(CHUNK)[None, :]
    sums, masks = [], []
    for rev in (False, True):
        before = (r >= t) if rev else (r <= t)
        after = (r < t) if rev else (r > t)
        blocks = [before, after]
        lev_masks = []
        for lv in range(1, GLA_LEVELS + 1):
            seg = CHUNK >> lv
            st, sr = t // seg, r // seg
            q_half = (st % 2 == 0) if rev else (st % 2 == 1)
            k_half = ~q_half
            blocks.append((st == sr) & ((before & q_half) | (after & k_half)))
            lev_masks.append(q_half & (sr == (st + 1 if rev else st - 1)))
        lev_masks.append(r == t)
        sums.append(np.concatenate(blocks, axis=0))
        masks.append(np.stack([np.concatenate([m, m], axis=0) for m in lev_masks]))
    return np.stack(sums).astype(np.float32), np.stack(masks).astype(np.float32)


def _gla_rope_tables(t_len):
    pos = np.arange(t_len)
    row = (pos // GRID_W).astype(np.float64)
    col = (pos % GRID_W).astype(np.float64)
    n = GLA_DK // 4
    freqs = ROPE_THETA ** (-np.arange(n, dtype=np.float64) / n)
    ang = np.concatenate([row[:, None] * freqs, col[:, None] * freqs], axis=-1)
    cos = np.concatenate([np.cos(ang), np.cos(ang)] * GLA_PAIR, axis=-1)
    sin = np.concatenate([-np.sin(ang), np.sin(ang)] * GLA_PAIR, axis=-1)
    return cos.astype(np.float32), sin.astype(np.float32)


def _gla_gate_weights(wg2, bg):
    half = GLA_DK // 2
    lane = np.arange(LANE)
    w_all, b_all = [], []
    for d in range(2):
        wd, bd = [], []
        for j in range(GLA_HEADS // GLA_PAIR):
            cols = half * (GLA_PAIR * j + lane // GLA_DK) + lane % half
            w = jnp.zeros((N_SMALL, LANE), F32).at[MY_OFF["gla_g"][0] + d * GLA_RANK:
                                                    MY_OFF["gla_g"][0] + (d + 1) * GLA_RANK].set(wg2[d][:, cols])
            wd.append(w)
            bd.append(bg[d][cols].reshape(1, LANE))
        w_all.append(jnp.stack(wd))
        b_all.append(jnp.stack(bd))
    return jnp.stack(w_all), jnp.stack(b_all)


def _gla_local(c, q_ref, k_ref, v_ref, gs_ref, rope, wg_ref, bgate_ref, sums_ref, masks_ref, qr_s, ke_s, tot_s,
               acc_ref):
    hp = lax.Precision.HIGHEST
    r0 = pl.multiple_of(c * CHUNK, CHUNK)
    rows = pl.ds(r0, CHUNK)
    lane = lax.broadcasted_iota(jnp.int32, (CHUNK, LANE), 1)
    q = q_ref[0, rows, :] * (GLA_DK ** -0.5)
    k = k_ref[0, rows, :]
    v16 = v_ref[0, rows, :].astype(BF16)
    if rope is not None:
        cos = rope[0][rows, :]
        sin = rope[1][rows, :]
        first_half = (lane % GLA_DK) < GLA_DK // 2

        def rot(x):
            partner = jnp.where(first_half, pltpu.roll(x, LANE - GLA_DK // 2, axis=1), pltpu.roll(x, GLA_DK // 2, axis=1))
            return x * cos + partner * sin
        q, k = rot(q), rot(k)
    k16 = k.astype(BF16)
    gs = gs_ref[0, rows, :]

    def stack_heads(a):
        return jnp.concatenate([jnp.where((lane >= hh * GLA_DK) & (lane < (hh + 1) * GLA_DK), a, 0.0)
                                for hh in range(GLA_PAIR)], axis=0).astype(BF16)

    q_heads = stack_heads(q)
    atts = []
    for d in range(2):
        x = jnp.dot(gs, wg_ref[d, 0], precision=hp, preferred_element_type=F32) + bgate_ref[d, 0]
        g = (jnp.minimum(x, 0.0) - jnp.log1p(jnp.exp(-jnp.abs(x)))) * (1.0 / GLA_TAU)
        sums = _sum_rows(sums_ref[d], g)
        run = sums[0:CHUNK]
        total = run[0:1] if d else run[CHUNK - 1:CHUNK]
        qr_s[d, rows, :] = (q * jnp.exp(run)).astype(BF16)
        ke_s[d, rows, :] = (k * jnp.exp(sums[CHUNK:2 * CHUNK])).astype(BF16)
        tot_s[d, pl.ds(c, 1), :] = jnp.exp(total)
        att = _nt_dot(q_heads, k16) * masks_ref[d, GLA_LEVELS]
        for lv in range(GLA_LEVELS):
            o = (2 + lv) * CHUNK
            part = jnp.exp(sums[o:o + CHUNK])
            att = att + _nt_dot(stack_heads(q * part), (k * part).astype(BF16)) * masks_ref[d, lv]
        atts.append(att)
    av = jnp.dot(jnp.concatenate(atts, axis=0).astype(BF16), v16, preferred_element_type=F32)
    acc_ref[rows, :] = sum(
        jnp.concatenate([av[(GLA_PAIR * d + hh) * CHUNK:(GLA_PAIR * d + hh + 1) * CHUNK, hh * GLA_DV:(hh + 1) * GLA_DV]
                         for hh in range(GLA_PAIR)], axis=1) for d in range(2))


def _gla_recur(chunks, v_ref, qr_s, ke_s, tot_s, st_refs, acc_ref):
    outs = []
    for d, c in enumerate(chunks):
        rows = pl.ds(pl.multiple_of(c * CHUNK, CHUNK), CHUNK)
        st = st_refs[d][...]
        outs.append((rows, _nt_dot(qr_s[d, rows, :], st.astype(BF16))))
        upd = _tn_dot(v_ref[0, rows, :].astype(BF16), ke_s[d, rows, :])
        srow = lax.broadcasted_iota(jnp.int32, upd.shape, 0) // GLA_DV
        scol = lax.broadcasted_iota(jnp.int32, upd.shape, 1) // GLA_DK
        st_refs[d][...] = jnp.where(srow == scol, st * tot_s[d, pl.ds(c, 1), :] + upd, 0.0)
    for rows, o in outs:
        acc_ref[rows, :] += o


def _gla_finish(acc_ref, z_ref, norm_ref, o_ref, n_rows):
    tile = min(256, n_rows)
    for i in range(n_rows // tile):
        rows = slice(i * tile, (i + 1) * tile)
        o = acc_ref[rows, :]
        z = z_ref[0, rows, :]
        y = jnp.concatenate([_rms_rows(o[:, hh * GLA_DV:(hh + 1) * GLA_DV], norm_ref[...])
                             for hh in range(GLA_PAIR)], axis=1)
        o_ref[0, rows, :] = (y * (z * jax.nn.sigmoid(z))).astype(o_ref.dtype)


def _gla_kernel(q_ref, k_ref, v_ref, z_ref, gs_ref, qc_ref, kc_ref, vc_ref, zc_ref, gsc_ref, cos_ref, sin_ref,
                wg_ref, bgate_ref, norm_ref, sums_ref, masks_ref, o_ref, oc_ref, stf_ref, stb_ref, qr_s, ke_s, tot_s,
                acc_ref):
    st_refs = (stf_ref, stb_ref)
    for r in st_refs:
        r[...] = jnp.zeros_like(r)
    for (qq, kk, vv, gg), rope, zz_ref, out_ref in (((qc_ref, kc_ref, vc_ref, gsc_ref), None, zc_ref, oc_ref),
                                                    ((q_ref, k_ref, v_ref, gs_ref), (cos_ref, sin_ref), z_ref, o_ref)):
        n = qq.shape[1]
        n_chunks = n // CHUNK

        def local_step(c, carry, qq=qq, kk=kk, vv=vv, gg=gg, rope=rope):
            _gla_local(c, qq, kk, vv, gg, rope, wg_ref, bgate_ref, sums_ref, masks_ref, qr_s, ke_s, tot_s, acc_ref)
            return carry

        def recur_step(i, carry, vv=vv, n_chunks=n_chunks):
            _gla_recur((i, n_chunks - 1 - i), vv, qr_s, ke_s, tot_s, st_refs, acc_ref)
            return carry

        lax.fori_loop(0, n_chunks, local_step, 0, unroll=2)
        lax.fori_loop(0, n_chunks, recur_step, 0)
        _gla_finish(acc_ref, zz_ref, norm_ref, out_ref, n)


def gla_branch(p_main, p_small, pc_main, pc_small, wg2, bg, norm):
    b, t, _ = p_main.shape
    tc = pc_main.shape[1]
    nj = GLA_HEADS // GLA_PAIR
    vw = GLA_PAIR * GLA_DV
    o_q = MY_OFF["gla_q"][0] // LANE
    o_k = MY_OFF["gla_k"][0] // LANE
    o_v = MY_OFF["gla_v"][0] // vw
    o_z = MY_OFF["gla_z"][0] // vw
    sums, masks = _gla_tables()
    cos, sin = _gla_rope_tables(t)
    wge, bge = _gla_gate_weights(wg2.astype(F32), bg.astype(F32))

    def seq(n, w, o):
        return pl.BlockSpec((1, n, w), lambda i, j: (i, 0, o + j))

    def whole(n):
        return pl.BlockSpec((1, n, N_SMALL), lambda i, j: (i, 0, 0))

    def const(shape):
        return pl.BlockSpec(shape, lambda i, j: (0,) * len(shape))

    return pl.pallas_call(
        _gla_kernel,
        grid=(b, nj),
        in_specs=[seq(t, LANE, o_q), seq(t, LANE, o_k), seq(t, vw, o_v), seq(t, vw, o_z), whole(t),
                  seq(tc, LANE, o_q), seq(tc, LANE, o_k), seq(tc, vw, o_v), seq(tc, vw, o_z), whole(tc),
                  const((t, LANE)), const((t, LANE)),
                  pl.BlockSpec((2, 1, N_SMALL, LANE), lambda i, j: (0, j, 0, 0)),
                  pl.BlockSpec((2, 1, 1, LANE), lambda i, j: (0, j, 0, 0)),
                  const((1, GLA_DV)), const((2, GLA_SUM_ROWS, CHUNK)),
                  const((2, GLA_LEVELS + 1, GLA_PAIR * CHUNK, CHUNK))],
        out_specs=[pl.BlockSpec((1, t, vw), lambda i, j: (i, 0, j)), pl.BlockSpec((1, tc, vw), lambda i, j: (i, 0, j))],
        out_shape=[jax.ShapeDtypeStruct((b, t, BR_W), BF16), jax.ShapeDtypeStruct((b, tc, BR_W), BF16)],
        scratch_shapes=[pltpu.VMEM((vw, LANE), F32), pltpu.VMEM((vw, LANE), F32),
                        pltpu.VMEM((2, t, LANE), BF16), pltpu.VMEM((2, t, LANE), BF16),
                        pltpu.VMEM((2, t // CHUNK, LANE), F32),
                        pltpu.VMEM((t, vw), F32)],
        compiler_params=pltpu.CompilerParams(vmem_limit_bytes=VMEM_LIMIT),
        name="gla_scan",
    )(p_main, p_main, p_main, p_main, p_small, pc_main, pc_main, pc_main, pc_main, pc_small,
      jnp.asarray(cos), jnp.asarray(sin), wge, bge, norm.astype(F32).reshape(1, GLA_DV),
      jnp.asarray(sums, BF16), jnp.asarray(masks))


GDN_PAIR = 2 * CHUNK
GDN_STACK = 2 * GDN_PAIR


def _neumann(a):
    r = -a
    p = a
    for _ in range(int(math.log2(CHUNK)) - 1):
        p16 = p.astype(BF16)
        p = jnp.dot(p16, p16, preferred_element_type=F32)
        r = r + p + jnp.dot(r.astype(BF16), p.astype(BF16), preferred_element_type=F32)
    return r


def _seg_scan(x, reverse):
    n = x.shape[0]
    pos = lax.broadcasted_iota(jnp.int32, x.shape, 0) % CHUNK
    s = 1
    while s < CHUNK:
        if reverse:
            x = x + jnp.where(pos < CHUNK - s, pltpu.roll(x, n - s, axis=0), 0.0)
        else:
            x = x + jnp.where(pos >= s, pltpu.roll(x, s, axis=0), 0.0)
        s *= 2
    return x


def _gdn_prepare(q_ref, k_ref, v_ref, gs_ref, wq_ref, wk_ref, wv_ref, sel_ref, coef_ref, qkv_s, gb_s):
    hp = lax.Precision.HIGHEST
    n = q_ref.shape[1]
    for i, (x_ref, w_ref) in enumerate(((q_ref, wq_ref), (k_ref, wk_ref), (v_ref, wv_ref))):
        y = _dwconv_rows(x_ref[0], w_ref, GDN_CONV)
        y = y * jax.nn.sigmoid(y)
        if i < 2:
            y = y * lax.rsqrt(jnp.sum(y * y, axis=-1, keepdims=True) + EPS)
        if i == 0:
            y = y * (GDN_DH ** -0.5)
        qkv_s[i, 0:n, :] = y
    gs = gs_ref[0]
    for d in range(2):
        a = jnp.dot(gs, sel_ref[0, d], precision=hp, preferred_element_type=F32)
        bb = jnp.dot(gs, sel_ref[0, 2 + d], precision=hp, preferred_element_type=F32)
        g = -jnp.exp(coef_ref[0, d:d + 1, :]) * _softplus(a + coef_ref[0, 2 + d:3 + d, :])
        gb_s[d, 0:n, :] = _seg_scan(g, reverse=(d == 1))
        gb_s[2 + d, 0:n, :] = jax.nn.sigmoid(bb)
        gb_s[4 + d, 0:n, :] = jnp.exp(_seg_scan(g, reverse=(d == 0)) - g)


def _gdn_local(p, qkv_s, gb_s, negm_ref, x_s, b_s, qp_s, op_s):
    rows = pl.ds(pl.multiple_of(p * GDN_PAIR, GDN_PAIR), GDN_PAIR)
    q = qkv_s[0, rows, :]
    k = qkv_s[1, rows, :]
    v = qkv_s[2, rows, :]
    gam = [gb_s[d, rows, :] for d in range(2)]
    beta = [gb_s[2 + d, rows, :] for d in range(2)]
    egam = [jnp.exp(g) for g in gam]
    kb = [k * bt for bt in beta]
    k16 = k.astype(BF16)
    raw = _nt_dot(jnp.concatenate([kb[0], kb[1], q, q], axis=0).astype(BF16),
                  jnp.concatenate([k16, k16], axis=0))
    g2 = jnp.concatenate(gam, axis=0)
    g2t = g2.T
    diff = jnp.concatenate([g2, g2], axis=1) - jnp.concatenate([g2t, g2t], axis=0)
    dec = jnp.exp(jnp.minimum(diff, 0.0) + negm_ref[...])
    eye = (lax.broadcasted_iota(jnp.int32, dec.shape, 0) == lax.broadcasted_iota(jnp.int32, dec.shape, 1)).astype(F32)
    a_kk = raw[0:GDN_STACK] * dec
    a_qk = raw[GDN_STACK:] * (dec + eye)
    r = _neumann(a_kk)
    rhs = jnp.concatenate([jnp.concatenate([v * beta[d], kb[d] * egam[d]], axis=1) for d in range(2)], axis=0)
    sol = rhs + jnp.dot(r.astype(BF16), rhs.astype(BF16), preferred_element_type=F32)
    sol16 = sol.astype(BF16)
    ao = jnp.dot(a_qk.astype(BF16), sol16, preferred_element_type=F32)
    for d in range(2):
        blk = slice(d * GDN_PAIR, (d + 1) * GDN_PAIR)
        op_s[d, rows, :] = ao[blk, 0:GDN_DH]
        qp_s[d, rows, :] = (q * egam[d] - ao[blk, GDN_DH:]).astype(BF16)
        kd = (k * gb_s[4 + d, rows, :]).astype(BF16)
        for jj in range(2):
            sub = slice(jj * CHUNK, (jj + 1) * CHUNK)
            bx = _tn_dot(kd[sub], sol16[d * GDN_PAIR + jj * CHUNK:d * GDN_PAIR + (jj + 1) * CHUNK])
            b_s[d, 2 * p + jj] = bx[:, 0:GDN_DH]
            x_s[d, 2 * p + jj] = bx[:, GDN_DH:].astype(BF16)


def _gdn_recur(chunks, gb_s, x_s, b_s, qp_s, op_s, st_refs, acc_ref):
    outs = []
    for d, c in enumerate(chunks):
        rows = pl.ds(pl.multiple_of(c * CHUNK, CHUNK), CHUNK)
        st = st_refs[d][...]
        xs = jnp.dot(jnp.concatenate([x_s[d, c], qp_s[d, rows, :]], axis=0), st.astype(BF16),
                     preferred_element_type=F32)
        outs.append((rows, xs[GDN_DH:] + op_s[d, rows, :]))
        last = c * CHUNK + (0 if d else CHUNK - 1)
        st_refs[d][...] = st * jnp.exp(gb_s[d, pl.ds(last, 1), :]) + (b_s[d, c] - xs[0:GDN_DH])
    for rows, o in outs:
        acc_ref[rows, :] += o


def _gdn_finish(acc_ref, z_ref, norm_ref, o_ref, n_rows):
    tile = min(256, n_rows)
    for i in range(n_rows // tile):
        rows = slice(i * tile, (i + 1) * tile)
        z = z_ref[0, rows, :]
        o_ref[0, rows, :] = (_rms_rows(acc_ref[rows, :], norm_ref[...]) * (z * jax.nn.sigmoid(z))).astype(o_ref.dtype)


def _gdn_kernel(q_ref, k_ref, v_ref, z_ref, gs_ref, qc_ref, kc_ref, vc_ref, zc_ref, gsc_ref, wq_ref, wk_ref, wv_ref,
                sel_ref, coef_ref, norm_ref, negm_ref, o_ref, oc_ref, stf_ref, stb_ref, qkv_s, gb_s,
                x_s, b_s, qp_s, op_s, acc_ref):
    st_refs = (stf_ref, stb_ref)
    for r in st_refs:
        r[...] = jnp.zeros_like(r)
    for refs, zz_ref, out_ref in (((qc_ref, kc_ref, vc_ref, gsc_ref), zc_ref, oc_ref),
                                  ((q_ref, k_ref, v_ref, gs_ref), z_ref, o_ref)):
        n = refs[0].shape[1]
        n_chunks = n // CHUNK
        _gdn_prepare(*refs, wq_ref, wk_ref, wv_ref, sel_ref, coef_ref, qkv_s, gb_s)
        acc_ref[0:n, :] = jnp.zeros((n, LANE), F32)

        def local_step(p, carry):
            _gdn_local(p, qkv_s, gb_s, negm_ref, x_s, b_s, qp_s, op_s)
            return carry

        def recur_step(i, carry, n_chunks=n_chunks):
            _gdn_recur((i, n_chunks - 1 - i), gb_s, x_s, b_s, qp_s, op_s, st_refs, acc_ref)
            return carry

        lax.fori_loop(0, n // GDN_PAIR, local_step, 0, unroll=2)
        lax.fori_loop(0, n_chunks, recur_step, 0)
        _gdn_finish(acc_ref, zz_ref, norm_ref, out_ref, n)


def _gdn_order_mask():
    ti = np.arange(GDN_STACK)[:, None]
    si = np.arange(GDN_STACK)[None, :]
    same = (ti // CHUNK) == (si // CHUNK)
    earlier = same & np.where(ti < GDN_PAIR, si < ti, si > ti)
    return np.where(earlier, 0.0, NA_NEG).astype(np.float32)


def gdn_branch(p_main, p_small, pc_main, pc_small, conv_w, a_log, dt_bias, norm):
    b, t, _ = p_main.shape
    tc = pc_main.shape[1]
    assert t % GDN_PAIR == 0 and tc % GDN_PAIR == 0 and tc <= t
    nh = GDN_HEADS
    o_q = MY_OFF["gdn_qkv"][0] // LANE
    o_z = MY_OFF["gdn_z"][0] // LANE
    wt = conv_w.T.astype(F32)
    sel = np.zeros((nh, 4, N_SMALL, LANE), np.float32)
    for h in range(nh):
        for d in range(2):
            sel[h, d, MY_OFF["gdn_a"][0] + d * nh + h, :] = 1.0
            sel[h, 2 + d, MY_OFF["gdn_b"][0] + d * nh + h, :] = 1.0
    coef = jnp.concatenate([a_log.astype(F32).T, dt_bias.astype(F32).T], axis=1)
    coef = jnp.broadcast_to(coef[:, :, None], (nh, 4, LANE))

    def seq(n, o):
        return pl.BlockSpec((1, n, LANE), lambda i, j: (i, 0, o + j))

    def whole(n):
        return pl.BlockSpec((1, n, N_SMALL), lambda i, j: (i, 0, 0))

    def wspec(o):
        return pl.BlockSpec((GDN_CONV, LANE), lambda i, j: (0, o + j))

    return pl.pallas_call(
        _gdn_kernel,
        grid=(b, nh),
        in_specs=[seq(t, o_q), seq(t, o_q + nh), seq(t, o_q + 2 * nh), seq(t, o_z), whole(t),
                  seq(tc, o_q), seq(tc, o_q + nh), seq(tc, o_q + 2 * nh), seq(tc, o_z), whole(tc),
                  wspec(0), wspec(nh), wspec(2 * nh),
                  pl.BlockSpec((1, 4, N_SMALL, LANE), lambda i, j: (j, 0, 0, 0)),
                  pl.BlockSpec((1, 4, LANE), lambda i, j: (j, 0, 0)),
                  pl.BlockSpec((1, GDN_DH), lambda i, j: (0, 0)),
                  pl.BlockSpec((GDN_STACK, GDN_STACK), lambda i, j: (0, 0))],
        out_specs=[pl.BlockSpec((1, t, LANE), lambda i, j: (i, 0, j)), pl.BlockSpec((1, tc, LANE), lambda i, j: (i, 0, j))],
        out_shape=[jax.ShapeDtypeStruct((b, t, BR_W), BF16), jax.ShapeDtypeStruct((b, tc, BR_W), BF16)],
        scratch_shapes=[pltpu.VMEM((GDN_DH, GDN_DH), F32), pltpu.VMEM((GDN_DH, GDN_DH), F32),
                        pltpu.VMEM((3, t, LANE), F32), pltpu.VMEM((6, t, LANE), F32),
                        pltpu.VMEM((2, t // CHUNK, GDN_DH, GDN_DH), BF16),
                        pltpu.VMEM((2, t // CHUNK, GDN_DH, GDN_DH), F32),
                        pltpu.VMEM((2, t, LANE), BF16), pltpu.VMEM((2, t, LANE), F32),
                        pltpu.VMEM((t, LANE), F32)],
        compiler_params=pltpu.CompilerParams(vmem_limit_bytes=VMEM_LIMIT),
        name="gdn_scan",
    )(p_main, p_main, p_main, p_main, p_small, pc_main, pc_main, pc_main, pc_main, pc_small,
      wt, wt, wt, jnp.asarray(sel), coef, norm.astype(F32).reshape(1, GDN_DH), jnp.asarray(_gdn_order_mask()))


HY_COLS = 256
HY_ROWS = 512


def dft_matrices(length):
    n = 2 * length
    inner = 64
    assert length % inner == 0
    r = jnp.arange(n, dtype=jnp.int32)
    k = jnp.where(r <= length, r, r - length)[:, None]
    a1 = ((k * (inner * jnp.arange(length // inner, dtype=jnp.int32))[None, :]) % n).astype(F32) * (2.0 * math.pi / n)
    a0 = ((k * jnp.arange(inner, dtype=jnp.int32)[None, :]) % n).astype(F32) * (2.0 * math.pi / n)
    c1, s1 = jnp.cos(a1)[:, :, None], jnp.sin(a1)[:, :, None]
    c0, s0 = jnp.cos(a0)[:, None, :], jnp.sin(a0)[:, None, :]
    is_cos = (r <= length)[:, None, None]
    f = jnp.where(is_cos, c1 * c0 - s1 * s0, -(s1 * c0 + c1 * s0)).reshape(n, length)
    w = (jnp.where((r == 0) | (r == length), 1.0, 2.0) / n)[:, None]
    return f.astype(BF16), (f * w).T.astype(BF16)


def _hy_embedding(length):
    t = np.linspace(0.0, 1.0, length)[:, None]
    bands = (HY_EMB - 1) // 2
    wpos = 2.0 * math.pi * np.arange(length)[:, None] / length
    f = np.linspace(1e-4, bands - 1, bands)[None]
    z = np.concatenate([t, np.cos(f * wpos), -np.sin(f * wpos)], axis=-1)
    z = np.pad(z, ((0, 0), (0, HY_FFN - HY_EMB)))
    deltas = np.abs(np.linspace(math.log(HY_TARGET) / HY_SLOW, math.log(HY_TARGET) / HY_FAST, BR_W))[None]
    return z.astype(np.float32), deltas.astype(np.float32)


def _hy_filter_kernel(z_ref, win_ref, bin_ref, wmid_ref, bmid_ref, freq_ref, wout_ref, delta_ref, o_ref):
    hp = lax.Precision.HIGHEST
    length = z_ref.shape[0]
    h = jnp.sin(freq_ref[0] * (jnp.dot(z_ref[...], win_ref[...], precision=hp, preferred_element_type=F32)
                               + bin_ref[...]))
    for i in range(HY_INNER):
        h = jnp.sin(freq_ref[i + 1] * (jnp.dot(h, wmid_ref[i], precision=hp, preferred_element_type=F32)
                                       + bmid_ref[i]))
    h = jnp.dot(h, wout_ref[...], precision=hp, preferred_element_type=F32)
    t = lax.broadcasted_iota(jnp.int32, (length, BR_W), 0).astype(F32) * (1.0 / (length - 1))
    decay = jnp.exp(-t * delta_ref[...])
    h_f = h[:, :BR_W] * decay
    h_b = h[:, BR_W:] * decay
    for i, a in enumerate((h_f + h_b, h_f - h_b)):
        hi = a.astype(BF16)
        lo = (a - hi.astype(F32)).astype(BF16)
        o_ref[:, (2 * i) * BR_W:(2 * i + 1) * BR_W] = hi
        o_ref[:, (2 * i + 1) * BR_W:(2 * i + 2) * BR_W] = lo


def _hy_spectrum_kernel(f_ref, a_ref, ks_ref, kd_ref, *, length):
    i = pl.program_id(0)
    tm = f_ref.shape[0]
    r = jnp.dot(f_ref[...], a_ref[...], preferred_element_type=F32)
    ks_ref[...] = r[:, :BR_W] + r[:, BR_W:2 * BR_W]
    kd = r[:, 2 * BR_W:3 * BR_W] + r[:, 3 * BR_W:]
    row = lax.broadcasted_iota(jnp.int32, kd.shape, 0) + i * tm
    kd_ref[...] = jnp.where(row == length, 0.0, kd)


def hyena_filter_spectrum(length, fwd, filt):
    w_in, b_in, w_mid, b_mid, freq, w_out = filt
    z, deltas = _hy_embedding(length)
    n = 2 * length
    parts = pl.pallas_call(
        _hy_filter_kernel,
        out_shape=jax.ShapeDtypeStruct((length, 4 * BR_W), BF16),
        compiler_params=pltpu.CompilerParams(vmem_limit_bytes=VMEM_LIMIT),
        name="hyena_filter",
    )(jnp.asarray(z), jnp.pad(w_in, ((0, HY_FFN - HY_EMB), (0, 0))), b_in.reshape(1, HY_FFN), w_mid,
      b_mid.reshape(HY_INNER, 1, HY_FFN), freq.reshape(HY_INNER + 1, 1, HY_FFN), w_out, jnp.asarray(deltas))
    tm = min(HY_ROWS, n)
    return pl.pallas_call(
        functools.partial(_hy_spectrum_kernel, length=length),
        grid=(n // tm,),
        in_specs=[pl.BlockSpec((tm, length), lambda i: (i, 0)),
                  pl.BlockSpec((length, 4 * BR_W), lambda i: (0, 0))],
        out_specs=[pl.BlockSpec((tm, BR_W), lambda i: (i, 0)), pl.BlockSpec((tm, BR_W), lambda i: (i, 0))],
        out_shape=[jax.ShapeDtypeStruct((n, BR_W), F32), jax.ShapeDtypeStruct((n, BR_W), F32)],
        compiler_params=pltpu.CompilerParams(vmem_limit_bytes=VMEM_LIMIT),
        name="hyena_spectrum",
    )(fwd, parts)


def _shift_rows(x, off):
    t = x.shape[0]
    rolled = pltpu.roll(x, (-off) % t, axis=0)
    row = lax.broadcasted_iota(jnp.int32, x.shape, 0)
    valid = (row >= -off) if off < 0 else (row < t - off)
    return jnp.where(valid, rolled, 0.0)


def _dwconv_rows(x, w_ref, k):
    out = x * w_ref[k // 2:k // 2 + 1, :]
    for j in range(k):
        if j != k // 2:
            out = out + _shift_rows(x, j - k // 2) * w_ref[j:j + 1, :]
    return out


def _hy_prep_kernel(x0_ref, x1_ref, v_ref, z_ref, w0_ref, w1_ref, wv_ref, b0_ref, b1_ref, bv_ref, vx_ref, x0g_ref):
    x0 = _dwconv_rows(x0_ref[0], w0_ref, HY_CONV) + b0_ref[...]
    x1 = _dwconv_rows(x1_ref[0], w1_ref, HY_CONV) + b1_ref[...]
    v = _dwconv_rows(v_ref[0], wv_ref, HY_CONV) + bv_ref[...]
    z = z_ref[0]
    vx_ref[0] = v * x1
    x0g_ref[0] = x0 * (z * jax.nn.sigmoid(z))


def hyena_prep(p_main, conv_w, conv_b):
    b, length, _ = p_main.shape
    cb = LANE
    nj = BR_W // cb
    o_xv = MY_OFF["hy_xv"][0] // cb
    o_z = MY_OFF["hy_z"][0] // cb
    wt = conv_w.T
    bt = conv_b.reshape(1, 3 * BR_W)
    dspec = lambda o: pl.BlockSpec((1, length, cb), lambda i, j: (i, 0, o + j))
    wspec = lambda o: pl.BlockSpec((HY_CONV, cb), lambda i, j: (0, o + j))
    bspec = lambda o: pl.BlockSpec((1, cb), lambda i, j: (0, o + j))
    ospec = pl.BlockSpec((1, length, cb), lambda i, j: (i, 0, j))
    return pl.pallas_call(
        _hy_prep_kernel,
        grid=(b, nj),
        in_specs=[dspec(o_xv), dspec(o_xv + nj), dspec(o_xv + 2 * nj), dspec(o_z),
                  wspec(0), wspec(nj), wspec(2 * nj), bspec(0), bspec(nj), bspec(2 * nj)],
        out_specs=[ospec, ospec],
        out_shape=[jax.ShapeDtypeStruct((b, length, BR_W), F32)] * 2,
        compiler_params=pltpu.CompilerParams(vmem_limit_bytes=VMEM_LIMIT),
        name="hyena_prep",
    )(p_main, p_main, p_main, p_main, wt, wt, wt, bt, bt, bt)


def _hy_fwd_kernel(vx_ref, f_ref, kr_ref, ki_ref, knyq_ref, y_ref, *, length):
    v = vx_ref[0].astype(BF16)
    tm = min(HY_ROWS, length)
    for kt in range(length // tm):
        lo = kt * tm
        a = jnp.dot(f_ref[lo:lo + tm, :], v, preferred_element_type=F32)
        bh = jnp.dot(f_ref[length + lo:length + lo + tm, :], v, preferred_element_type=F32)
        kr = kr_ref[lo:lo + tm, :]
        ki = ki_ref[lo:lo + tm, :]
        if kt == 0:
            row = lax.broadcasted_iota(jnp.int32, kr.shape, 0)
            kb = jnp.where(row == 0, knyq_ref[0:1, :], kr)
        else:
            kb = kr
        y_ref[0, lo:lo + tm, :] = (a * kr - bh * ki).astype(BF16)
        y_ref[0, length + lo:length + lo + tm, :] = (a * ki + bh * kb).astype(BF16)


def _hy_inv_kernel(y_ref, inv_ref, vx_ref, x0g_ref, skip_ref, o_ref, *, length):
    tm = min(HY_ROWS, length)
    y = y_ref[0]
    for tt in range(length // tm):
        lo = tt * tm
        conv = jnp.dot(inv_ref[lo:lo + tm, :], y, preferred_element_type=F32)
        o_ref[0, lo:lo + tm, :] = ((conv + skip_ref[...] * vx_ref[0, lo:lo + tm, :])
                                   * x0g_ref[0, lo:lo + tm, :]).astype(o_ref.dtype)


def hyena_branch(p_main, conv_w, conv_b, skip, fwd, inv, ks, kd):
    b, length, _ = p_main.shape
    n = 2 * length
    vx, x0g = hyena_prep(p_main, conv_w, conv_b)
    cb = HY_COLS
    nj = BR_W // cb
    resident = lambda shape: pl.BlockSpec(shape, lambda j, i: (0, 0), pipeline_mode=pl.Buffered(1))
    y = pl.pallas_call(
        functools.partial(_hy_fwd_kernel, length=length),
        grid=(nj, b),
        in_specs=[pl.BlockSpec((1, length, cb), lambda j, i: (i, 0, j)),
                  resident((n, length)),
                  pl.BlockSpec((length, cb), lambda j, i: (0, j)),
                  pl.BlockSpec((length, cb), lambda j, i: (1, j)),
                  pl.BlockSpec((8, cb), lambda j, i: (length // 8, j))],
        out_specs=pl.BlockSpec((1, n, cb), lambda j, i: (i, 0, j)),
        out_shape=jax.ShapeDtypeStruct((b, n, BR_W), BF16),
        compiler_params=pltpu.CompilerParams(vmem_limit_bytes=VMEM_LIMIT),
        name="hyena_dft",
    )(vx, fwd, ks, kd, ks)
    return pl.pallas_call(
        functools.partial(_hy_inv_kernel, length=length),
        grid=(nj, b),
        in_specs=[pl.BlockSpec((1, n, cb), lambda j, i: (i, 0, j)),
                  resident((length, n)),
                  pl.BlockSpec((1, length, cb), lambda j, i: (i, 0, j)),
                  pl.BlockSpec((1, length, cb), lambda j, i: (i, 0, j)),
                  pl.BlockSpec((1, cb), lambda j, i: (0, j))],
        out_specs=pl.BlockSpec((1, length, cb), lambda j, i: (i, 0, j)),
        out_shape=jax.ShapeDtypeStruct((b, length, BR_W), BF16),
        compiler_params=pltpu.CompilerParams(vmem_limit_bytes=VMEM_LIMIT),
        name="hyena_idft",
    )(y, inv, vx, x0g, skip.reshape(1, BR_W))


def mixers(p_main, p_small, pc_main, pc_small, na_rpb, gla_wg2, gla_bg, gla_norm, gdn_conv, gdn_a_log, gdn_dt_bias,
           gdn_norm, hy_conv, hy_conv_b, hy_filt, hy_skip, with_ctx, dft_lat, dft_ctx):
    t = p_main.shape[1]
    ya = na_branch(p_main, pc_main, na_rpb)
    yb, yb_c = gla_branch(p_main, p_small, pc_main, pc_small, gla_wg2, gla_bg, gla_norm)
    yc, yc_c = gdn_branch(p_main, p_small, pc_main, pc_small, gdn_conv, gdn_a_log, gdn_dt_bias, gdn_norm)
    fwd, inv = dft_lat
    ks, kd = hyena_filter_spectrum(t, fwd, hy_filt)
    yd = hyena_branch(p_main, hy_conv, hy_conv_b, hy_skip, fwd, inv, ks, kd)
    if not with_ctx:
        return (ya, yb, yc, yd), None
    ya_c = ctx_attention_branch(pc_main)
    fwd_c, inv_c = dft_ctx
    ks_c, kd_c = hyena_filter_spectrum(pc_main.shape[1], fwd_c, hy_filt)
    yd_c = hyena_branch(pc_main, hy_conv, hy_conv_b, hy_skip, fwd_c, inv_c, ks_c, kd_c)
    return (ya, yb, yc, yd), (ya_c, yb_c, yc_c, yd_c)


def kernel(x, c, ctx, c_ctx, w_mod, b_mod, g_pre, g_post, w_in, w_gate, b_gate, w_branch, w_out, na_rpb, gla_wg2, gla_bg, gla_norm, gdn_conv, gdn_a_log, gdn_dt_bias, gdn_norm, hy_conv, hy_conv_b, hy_w_in, hy_b_in, hy_w_mid, hy_b_mid, hy_freq, hy_w_out, hy_skip):
    b, t, d = x.shape
    tc = ctx.shape[1]
    h = x.reshape(b * t, d)
    hc = ctx.reshape(b * tc, d)
    c_all = jnp.concatenate([c, c_ctx[None], jnp.zeros((7, d), F32)], axis=0)
    dft_lat = dft_matrices(t)
    dft_ctx = dft_matrices(tc)
    for l in range(DEPTH):
        with_ctx = l < DEPTH - 1
        mod = modulation(c_all, w_mod[l], b_mod[l])
        shift, scale, gate = [mod[:, i * d:(i + 1) * d].reshape(-1, 1, d) for i in range(3)]
        u = prenorm_mod(h, g_pre[l], scale[:b], shift[:b], t)
        uc = prenorm_mod(hc, g_pre[l], scale[b:b + 1], shift[b:b + 1], b * tc)
        w_main, w_small = _permute_w_in(w_in[l])
        p_main = matmul(u, w_main, 1024, 1536, name="in_proj").reshape(b, t, N_MAIN)
        p_small = matmul(u, w_small, 1024, N_SMALL, name="in_proj_small").reshape(b, t, N_SMALL)
        pc_main = matmul(uc, w_main, 1024, 1536, name="in_proj_ctx").reshape(b, tc, N_MAIN)
        pc_small = matmul(uc, w_small, 1024, N_SMALL, name="in_proj_ctx_small").reshape(b, tc, N_SMALL)
        ys, ycs = mixers(p_main, p_small, pc_main, pc_small,
                         na_rpb[l], gla_wg2[l], gla_bg[l], gla_norm[l], gdn_conv[l], gdn_a_log[l], gdn_dt_bias[l],
                         gdn_norm[l], hy_conv[l], hy_conv_b[l],
                         (hy_w_in[l], hy_b_in[l], hy_w_mid[l], hy_b_mid[l], hy_freq[l], hy_w_out[l]),
                         hy_skip[l], with_ctx, dft_lat, dft_ctx)
        h_new = merge(u, ys, w_gate[l], b_gate[l], w_branch[l], w_out[l], h, gate[:b], g_post[l], t)
        if with_ctx:
            hc = merge(uc, ycs, w_gate[l], b_gate[l], w_branch[l], w_out[l], hc, gate[b:b + 1], g_post[l], b * tc)
        h = h_new
    return h.reshape(b, t, d)
```

```python
import functools
import math

import numpy as np
import jax
import jax.numpy as jnp
from jax import lax
from jax.experimental import pallas as pl
from jax.experimental.pallas import tpu as pltpu

D_MODEL = 2048
BATCH = 16
SEQ = 2048
DEPTH = 2
GRID_W = 64
CTX_LEN = 256
N_BRANCH = 4
BR_W = D_MODEL // N_BRANCH
NA_HEADS = 8
NA_DH = BR_W // NA_HEADS
WIN_R = 8
WIN_C = 16
QBLK_C = 16
KBLK_C = WIN_C + QBLK_C
N_CBLK = GRID_W // QBLK_C
GLA_HEADS = 4
GLA_DV = BR_W // GLA_HEADS
GLA_DK = GLA_DV // 2
GLA_RANK = 16
GLA_TAU = 16.0
CHUNK = 64
GDN_HEADS = 4
GDN_DH = BR_W // GDN_HEADS
GDN_CONV = 5
HY_CONV = 3
HY_EMB = 33
HY_FFN = 64
HY_INNER = 2
HY_TARGET = 1e-2
HY_FAST = 0.3
HY_SLOW = 1.5
ROPE_THETA = 10000.0
EPS = 1e-6
F32 = jnp.float32
BF16 = jnp.bfloat16

REF_SPLITS = (
    ("na_qkv", 3 * BR_W), ("na_z", BR_W),
    ("gla_q", GLA_HEADS * GLA_DK), ("gla_k", GLA_HEADS * GLA_DK), ("gla_v", BR_W), ("gla_z", BR_W),
    ("gla_g", 2 * GLA_RANK),
    ("gdn_qkv", 3 * BR_W), ("gdn_z", BR_W), ("gdn_a", 2 * GDN_HEADS), ("gdn_b", 2 * GDN_HEADS),
    ("hy_xv", 3 * BR_W), ("hy_z", BR_W),
)
MAIN_ORDER = ("na_qkv", "na_z", "gla_q", "gla_k", "gla_v", "gla_z", "gdn_qkv", "gdn_z", "hy_xv", "hy_z")
SMALL_ORDER = ("gla_g", "gdn_a", "gdn_b")
LANE = 128
N_MAIN = sum(w for n, w in REF_SPLITS if n in MAIN_ORDER)
N_SMALL = LANE
VMEM_LIMIT = 48 * 1024 * 1024

def _ref_offsets():
    out, o = {}, 0
    for name, w in REF_SPLITS:
        out[name] = (o, w)
        o += w
    return out


def _my_offsets():
    out, o = {}, 0
    for name in MAIN_ORDER:
        w = dict(REF_SPLITS)[name]
        out[name] = (o, w)
        o += w
    o = 0
    for name in SMALL_ORDER:
        w = dict(REF_SPLITS)[name]
        out[name] = (o, w)
        o += w
    return out


REF_OFF = _ref_offsets()
MY_OFF = _my_offsets()


def _permute_w_in(w_in_l):
    main = jnp.concatenate([w_in_l[:, REF_OFF[n][0]:REF_OFF[n][0] + REF_OFF[n][1]] for n in MAIN_ORDER], axis=1)
    small = jnp.concatenate([w_in_l[:, REF_OFF[n][0]:REF_OFF[n][0] + REF_OFF[n][1]] for n in SMALL_ORDER], axis=1)
    small = jnp.pad(small, ((0, 0), (0, N_SMALL - small.shape[1])))
    return main.astype(BF16), small.astype(BF16)


def _mod_kernel(c_ref, w_ref, b_ref, o_ref):
    c = c_ref[...]
    a = (c * jax.nn.sigmoid(c)).astype(BF16)
    o_ref[...] = jnp.dot(a, w_ref[...].astype(BF16), preferred_element_type=F32) + b_ref[...]


def modulation(c_all, w_mod_l, b_mod_l):
    r, d = c_all.shape
    n = w_mod_l.shape[1]
    tn = 768
    return pl.pallas_call(
        _mod_kernel,
        grid=(n // tn,),
        in_specs=[pl.BlockSpec((r, d), lambda j: (0, 0)),
                  pl.BlockSpec((d, tn), lambda j: (0, j)),
                  pl.BlockSpec((1, tn), lambda j: (0, j))],
        out_specs=pl.BlockSpec((r, tn), lambda j: (0, j)),
        out_shape=jax.ShapeDtypeStruct((r, n), F32),
        compiler_params=pltpu.CompilerParams(vmem_limit_bytes=VMEM_LIMIT),
        name="modulation",
    )(c_all, w_mod_l, b_mod_l.reshape(1, n))


def _prenorm_kernel(x_ref, g_ref, scale_ref, shift_ref, o_ref):
    x = x_ref[...]
    y = x * lax.rsqrt(jnp.mean(x * x, axis=-1, keepdims=True) + EPS)
    u = (y * g_ref[...]) * (1.0 + scale_ref[0]) + shift_ref[0]
    o_ref[...] = u.astype(o_ref.dtype)


def prenorm_mod(x2d, g_pre_l, scale, shift, rows_per_group, out_dtype=BF16):
    n, d = x2d.shape
    tm = 512
    tiles_per_group = rows_per_group // tm
    return pl.pallas_call(
        _prenorm_kernel,
        grid=(n // tm,),
        in_specs=[pl.BlockSpec((tm, d), lambda i: (i, 0)),
                  pl.BlockSpec((1, d), lambda i: (0, 0)),
                  pl.BlockSpec((1, 1, d), lambda i: (i // tiles_per_group, 0, 0)),
                  pl.BlockSpec((1, 1, d), lambda i: (i // tiles_per_group, 0, 0))],
        out_specs=pl.BlockSpec((tm, d), lambda i: (i, 0)),
        out_shape=jax.ShapeDtypeStruct((n, d), out_dtype),
        compiler_params=pltpu.CompilerParams(vmem_limit_bytes=VMEM_LIMIT),
        name="prenorm_mod",
    )(x2d, g_pre_l.reshape(1, d), scale, shift)


def _matmul_kernel(u_ref, w_ref, o_ref):
    o_ref[...] = jnp.dot(u_ref[...], w_ref[...], preferred_element_type=F32).astype(o_ref.dtype)


def matmul(u, w, tm, tn, out_dtype=F32, name="matmul"):
    n, k = u.shape
    m = w.shape[1]
    return pl.pallas_call(
        _matmul_kernel,
        grid=(m // tn, n // tm),
        in_specs=[pl.BlockSpec((tm, k), lambda j, i: (i, 0)),
                  pl.BlockSpec((k, tn), lambda j, i: (0, j))],
        out_specs=pl.BlockSpec((tm, tn), lambda j, i: (i, j)),
        out_shape=jax.ShapeDtypeStruct((n, m), out_dtype),
        compiler_params=pltpu.CompilerParams(vmem_limit_bytes=VMEM_LIMIT),
        name=name,
    )(u, w)


def _gate_merge_kernel(u_ref, y0_ref, y1_ref, y2_ref, y3_ref, wg_ref, bg_ref, wb_ref, o_ref):
    u = u_ref[...]
    merged = None
    for i, y_ref in enumerate((y0_ref, y1_ref, y2_ref, y3_ref)):
        g = jnp.dot(u, wg_ref[i], preferred_element_type=F32) + bg_ref[i]
        yb = jnp.dot(y_ref[...], wb_ref[i], preferred_element_type=F32)
        term = jax.nn.sigmoid(g) * yb
        merged = term if merged is None else merged + term
    o_ref[...] = merged.astype(o_ref.dtype)


def _out_proj_kernel(m_ref, wo_ref, h_ref, gate_ref, gpost_ref, o_ref):
    y = jnp.dot(m_ref[...], wo_ref[...], preferred_element_type=F32)
    yn = y * lax.rsqrt(jnp.mean(y * y, axis=-1, keepdims=True) + EPS) * gpost_ref[...]
    o_ref[...] = h_ref[...] + gate_ref[0] * yn


def merge(u, ys, w_gate_l, b_gate_l, w_branch_l, w_out_l, h2d, gate, g_post_l, rows_per_group):
    n, d = u.shape
    tm, tn = 1024, 256
    yspec = pl.BlockSpec((tm, BR_W), lambda i, j: (i, 0))
    merged = pl.pallas_call(
        _gate_merge_kernel,
        grid=(n // tm, d // tn),
        in_specs=[pl.BlockSpec((tm, d), lambda i, j: (i, 0)),
                  yspec, yspec, yspec, yspec,
                  pl.BlockSpec((N_BRANCH, d, tn), lambda i, j: (0, 0, j)),
                  pl.BlockSpec((N_BRANCH, 1, tn), lambda i, j: (0, 0, j)),
                  pl.BlockSpec((N_BRANCH, BR_W, tn), lambda i, j: (0, 0, j))],
        out_specs=pl.BlockSpec((tm, tn), lambda i, j: (i, j)),
        out_shape=jax.ShapeDtypeStruct((n, d), BF16),
        compiler_params=pltpu.CompilerParams(vmem_limit_bytes=VMEM_LIMIT),
        name="gate_merge",
    )(u, *[y.reshape(n, BR_W) for y in ys], w_gate_l.astype(BF16), b_gate_l.reshape(N_BRANCH, 1, d),
      w_branch_l.astype(BF16))
    tm = 512
    tiles_per_group = rows_per_group // tm
    return pl.pallas_call(
        _out_proj_kernel,
        grid=(n // tm,),
        in_specs=[pl.BlockSpec((tm, d), lambda i: (i, 0)),
                  pl.BlockSpec((d, d), lambda i: (0, 0), pipeline_mode=pl.Buffered(1)),
                  pl.BlockSpec((tm, d), lambda i: (i, 0)),
                  pl.BlockSpec((1, 1, d), lambda i: (i // tiles_per_group, 0, 0)),
                  pl.BlockSpec((1, d), lambda i: (0, 0))],
        out_specs=pl.BlockSpec((tm, d), lambda i: (i, 0)),
        out_shape=jax.ShapeDtypeStruct((n, d), F32),
        compiler_params=pltpu.CompilerParams(vmem_limit_bytes=VMEM_LIMIT),
        name="out_proj",
    )(merged, w_out_l.astype(BF16), h2d, gate, g_post_l.reshape(1, d))


NA_NEG = -1e30
NA_PAIR = LANE // NA_DH


NA_QROWS = 2
NA_KROWS = WIN_R + 2


def _na_pair_types(rows):
    assert rows % NA_QROWS == 0 and rows >= NA_KROWS
    layouts, index = [], []
    for r0 in range(0, rows, NA_QROWS):
        rs = [int(np.clip(r0 + i - WIN_R // 2, 0, rows - WIN_R)) for i in range(NA_QROWS)]
        u0 = min(rs[0], rows - NA_KROWS)
        lay = (r0 - u0,) + tuple(x - u0 for x in rs)
        if lay not in layouts:
            layouts.append(lay)
        index.append(layouts.index(lay))
    return layouts, np.asarray(index, np.int32)


def na_bias_table(rpb, rows):
    heads = rpb.shape[0]
    c = np.arange(GRID_W)[:, None]
    kc = np.arange(GRID_W)[None, :]
    cstart = np.clip(c - WIN_C // 2, 0, GRID_W - WIN_C)
    valid = (kc >= cstart) & (kc < cstart + WIN_C)
    dc = np.clip(kc - c + WIN_C - 1, 0, 2 * WIN_C - 2)
    onehot = ((dc[None] == np.arange(2 * WIN_C - 1)[:, None, None]) & valid[None]).astype(np.float32)
    by_dr = jnp.einsum("hrd,dck->hrck", rpb.astype(F32), jnp.asarray(onehot), precision=lax.Precision.HIGHEST)
    by_dr = jnp.where(valid, by_dr, NA_NEG)
    outside = jnp.full((heads, GRID_W, GRID_W), NA_NEG, F32)
    layouts, index = _na_pair_types(rows)
    tabs = []
    for lay in layouts:
        a, starts = lay[0], lay[1:]
        strips = []
        for i, st in enumerate(starts):
            tiles = [by_dr[:, j - a - i + WIN_R - 1] if st <= j < st + WIN_R else outside for j in range(NA_KROWS)]
            strips.append(jnp.concatenate(tiles, axis=-1))
        tabs.append(jnp.concatenate(strips, axis=1))
    return jnp.stack(tabs, axis=1), index


def _nt_dot(a, b):
    return lax.dot_general(a, b, (((1,), (1,)), ((), ())), preferred_element_type=F32)


def _na_kernel(lay_ref, q_ref, k_ref, v_ref, kc_ref, vc_ref, z_ref, bias_ref, o_ref, *, rows):
    nq = NA_QROWS * GRID_W
    nk = NA_KROWS * GRID_W
    lane = lax.broadcasted_iota(jnp.int32, (nq, LANE), 1)
    head_mask = [(lane >= hh * NA_DH) & (lane < (hh + 1) * NA_DH) for hh in range(NA_PAIR)]
    kc = kc_ref[0].astype(BF16)
    vc = vc_ref[0].astype(BF16)

    def body(p, carry):
        r0 = p * NA_QROWS
        u0 = jnp.minimum(jnp.clip(r0 - WIN_R // 2, 0, rows - WIN_R), rows - NA_KROWS)
        q0 = pl.multiple_of(r0 * GRID_W, nq)
        k0 = pl.multiple_of(u0 * GRID_W, GRID_W)
        lay = lay_ref[p]
        q = q_ref[0, pl.ds(q0, nq), :] * (NA_DH ** -0.5)
        kw = k_ref[0, pl.ds(k0, nk), :].astype(BF16)
        vw = v_ref[0, pl.ds(k0, nk), :].astype(BF16)
        qs = jnp.concatenate([jnp.where(m, q, 0.0) for m in head_mask], axis=0).astype(BF16)
        s = _nt_dot(qs, kw) + jnp.concatenate([bias_ref[hh, lay] for hh in range(NA_PAIR)], axis=0)
        sc = _nt_dot(qs, kc)
        m = jnp.maximum(jnp.max(s, axis=-1, keepdims=True), jnp.max(sc, axis=-1, keepdims=True))
        pw = jnp.exp(s - m)
        pc = jnp.exp(sc - m)
        l = jnp.sum(pw, axis=-1, keepdims=True) + jnp.sum(pc, axis=-1, keepdims=True)
        o = (jnp.dot(pw.astype(BF16), vw, preferred_element_type=F32)
             + jnp.dot(pc.astype(BF16), vc, preferred_element_type=F32)) / l
        out = o[0:nq]
        for hh in range(1, NA_PAIR):
            out = jnp.where(head_mask[hh], o[hh * nq:(hh + 1) * nq], out)
        z = z_ref[0, pl.ds(q0, nq), :]
        o_ref[0, pl.ds(q0, nq), :] = (out * (z * jax.nn.sigmoid(z))).astype(o_ref.dtype)
        return carry

    lax.fori_loop(0, rows // NA_QROWS, body, 0)


def na_branch(p_main, pc_main, rpb):
    b, t, _ = p_main.shape
    tc = pc_main.shape[1]
    rows = t // GRID_W
    nj = BR_W // LANE
    o_q = MY_OFF["na_qkv"][0] // LANE
    o_z = MY_OFF["na_z"][0] // LANE
    bias, layout_of_pair = na_bias_table(rpb, rows)
    lat = lambda o: pl.BlockSpec((1, t, LANE), lambda i, j: (i, 0, o + j))
    cx = lambda o: pl.BlockSpec((1, tc, LANE), lambda i, j: (i, 0, o + j))
    return pl.pallas_call(
        functools.partial(_na_kernel, rows=rows),
        grid=(b, nj),
        in_specs=[pl.BlockSpec(memory_space=pltpu.SMEM),
                  lat(o_q), lat(o_q + nj), lat(o_q + 2 * nj), cx(o_q + nj), cx(o_q + 2 * nj), lat(o_z),
                  pl.BlockSpec((NA_PAIR,) + bias.shape[1:], lambda i, j: (j, 0, 0, 0))],
        out_specs=pl.BlockSpec((1, t, LANE), lambda i, j: (i, 0, j)),
        out_shape=jax.ShapeDtypeStruct((b, t, BR_W), BF16),
        compiler_params=pltpu.CompilerParams(vmem_limit_bytes=VMEM_LIMIT),
        name="na_attention",
    )(jnp.asarray(layout_of_pair), p_main, p_main, p_main, pc_main, pc_main, p_main, bias)


def _ctx_attn_kernel(q_ref, k_ref, v_ref, z_ref, o_ref):
    tc = q_ref.shape[1]
    lane = lax.broadcasted_iota(jnp.int32, (tc, LANE), 1)
    q = q_ref[0] * (NA_DH ** -0.5)
    k = k_ref[0].astype(BF16)
    v = v_ref[0].astype(BF16)
    out = None
    for hh in range(NA_PAIR):
        mask = (lane >= hh * NA_DH) & (lane < (hh + 1) * NA_DH)
        s = _nt_dot(jnp.where(mask, q, 0.0).astype(BF16), k)
        p = jnp.exp(s - jnp.max(s, axis=-1, keepdims=True))
        o = jnp.dot(p.astype(BF16), v, preferred_element_type=F32) / jnp.sum(p, axis=-1, keepdims=True)
        out = o if out is None else jnp.where(mask, o, out)
    z = z_ref[0]
    o_ref[0] = (out * (z * jax.nn.sigmoid(z))).astype(o_ref.dtype)


def ctx_attention_branch(pc_main):
    b, tc, _ = pc_main.shape
    nj = BR_W // LANE
    o_q = MY_OFF["na_qkv"][0] // LANE
    o_z = MY_OFF["na_z"][0] // LANE
    cx = lambda o: pl.BlockSpec((1, tc, LANE), lambda i, j: (i, 0, o + j))
    return pl.pallas_call(
        _ctx_attn_kernel,
        grid=(b, nj),
        in_specs=[cx(o_q), cx(o_q + nj), cx(o_q + 2 * nj), cx(o_z)],
        out_specs=pl.BlockSpec((1, tc, LANE), lambda i, j: (i, 0, j)),
        out_shape=jax.ShapeDtypeStruct((b, tc, BR_W), BF16),
        compiler_params=pltpu.CompilerParams(vmem_limit_bytes=VMEM_LIMIT),
        name="ctx_attention",
    )(pc_main, pc_main, pc_main, pc_main)


def _hi_lo(x):
    hi = x.astype(BF16)
    return hi, (x - hi.astype(F32)).astype(BF16)


def _dot_hl(a, b):
    a1, a2 = _hi_lo(a)
    b1, b2 = _hi_lo(b)
    return (jnp.dot(a1, b1, preferred_element_type=F32) + jnp.dot(a1, b2, preferred_element_type=F32)
            + jnp.dot(a2, b1, preferred_element_type=F32))


def _tn_dot(a, b):
    return lax.dot_general(a, b, (((0,), (0,)), ((), ())), preferred_element_type=F32)


def _softplus(x):
    return jnp.maximum(x, 0.0) + jnp.log1p(jnp.exp(-jnp.abs(x)))


def _rms_rows(o, g):
    return o * lax.rsqrt(jnp.mean(o * o, axis=-1, keepdims=True) + EPS) * g


GLA_LEVELS = 6
GLA_PAIR = LANE // GLA_DK
GLA_SUM_ROWS = (2 + GLA_LEVELS) * CHUNK


def _gla_tables():
    assert CHUNK == 1 << GLA_LEVELS
    t = np.arange(CHUNK)[:, None]
    r = np.arange(CHUNK)[None, :]
    sums, masks = [], []
    for rev in (False, True):
        before = (r >= t) if rev else (r <= t)
        after = (r < t) if rev else (r > t)
        blocks = [before, after]
        lev_masks = []
        for lv in range(1, GLA_LEVELS + 1):
            seg = CHUNK >> lv
            st, sr = t // seg, r // seg
            q_half = (st % 2 == 0) if rev else (st % 2 == 1)
            k_half = ~q_half
            blocks.append((st == sr) & ((before & q_half) | (after & k_half)))
            lev_masks.append(q_half & (sr == (st + 1 if rev else st - 1)))
        lev_masks.append(r == t)
        sums.append(np.concatenate(blocks, axis=0))
        masks.append(np.stack([np.concatenate([m, m], axis=0) for m in lev_masks]))
    return np.stack(sums).astype(np.float32), np.stack(masks).astype(np.float32)


def _gla_rope_tables(t_len):
    pos = np.arange(t_len)
    row = (pos // GRID_W).astype(np.float64)
    col = (pos % GRID_W).astype(np.float64)
    n = GLA_DK // 4
    freqs = ROPE_THETA ** (-np.arange(n, dtype=np.float64) / n)
    ang = np.concatenate([row[:, None] * freqs, col[:, None] * freqs], axis=-1)
    cos = np.concatenate([np.cos(ang), np.cos(ang)] * GLA_PAIR, axis=-1)
    sin = np.concatenate([-np.sin(ang), np.sin(ang)] * GLA_PAIR, axis=-1)
    return cos.astype(np.float32), sin.astype(np.float32)


def _gla_gate_weights(wg2, bg):
    half = GLA_DK // 2
    lane = np.arange(LANE)
    w_all, b_all = [], []
    for d in range(2):
        wd, bd = [], []
        for j in range(GLA_HEADS // GLA_PAIR):
            cols = half * (GLA_PAIR * j + lane // GLA_DK) + lane % half
            w = jnp.zeros((N_SMALL, LANE), F32).at[MY_OFF["gla_g"][0] + d * GLA_RANK:
                                                    MY_OFF["gla_g"][0] + (d + 1) * GLA_RANK].set(wg2[d][:, cols])
            wd.append(w)
            bd.append(bg[d][cols].reshape(1, LANE))
        w_all.append(jnp.stack(wd))
        b_all.append(jnp.stack(bd))
    return jnp.stack(w_all), jnp.stack(b_all)


def _gla_local(c, q_ref, k_ref, v_ref, gs_ref, rope, wg_ref, bgate_ref, sums_ref, masks_ref, qr_s, ke_s, tot_s,
               acc_ref):
    r0 = pl.multiple_of(c * CHUNK, CHUNK)
    rows = pl.ds(r0, CHUNK)
    lane = lax.broadcasted_iota(jnp.int32, (CHUNK, LANE), 1)
    q = q_ref[0, rows, :] * (GLA_DK ** -0.5)
    k = k_ref[0, rows, :]
    v16 = v_ref[0, rows, :].astype(BF16)
    if rope is not None:
        cos = rope[0][rows, :]
        sin = rope[1][rows, :]
        first_half = (lane % GLA_DK) < GLA_DK // 2

        def rot(x):
            partner = jnp.where(first_half, pltpu.roll(x, LANE - GLA_DK // 2, axis=1), pltpu.roll(x, GLA_DK // 2, axis=1))
            return x * cos + partner * sin
        q, k = rot(q), rot(k)
    k16 = k.astype(BF16)
    gs = gs_ref[0, rows, :]

    def stack_heads(a):
        return jnp.concatenate([jnp.where((lane >= hh * GLA_DK) & (lane < (hh + 1) * GLA_DK), a, 0.0)
                                for hh in range(GLA_PAIR)], axis=0).astype(BF16)

    q_heads = stack_heads(q)
    atts = []
    for d in range(2):
        x = _dot_hl(gs, wg_ref[d, 0]) + bgate_ref[d, 0]
        g = (jnp.minimum(x, 0.0) - jnp.log1p(jnp.exp(-jnp.abs(x)))) * (1.0 / GLA_TAU)
        g_hi, g_lo = _hi_lo(g)
        seg = sums_ref[d]
        sums = jnp.concatenate(
            [jnp.dot(seg[0:2 * CHUNK], g_hi, preferred_element_type=F32)
             + jnp.dot(seg[0:2 * CHUNK], g_lo, preferred_element_type=F32),
             jnp.dot(seg[2 * CHUNK:], g_hi, preferred_element_type=F32)], axis=0)
        run = sums[0:CHUNK]
        total = run[0:1] if d else run[CHUNK - 1:CHUNK]
        qr_s[d, rows, :] = (q * jnp.exp(run)).astype(BF16)
        ke_s[d, rows, :] = (k * jnp.exp(sums[CHUNK:2 * CHUNK])).astype(BF16)
        tot_s[d, pl.ds(c, 1), :] = jnp.exp(total)
        att = _nt_dot(q_heads, k16) * masks_ref[d, GLA_LEVELS]
        for lv in range(GLA_LEVELS):
            o = (2 + lv) * CHUNK
            part = jnp.exp(sums[o:o + CHUNK])
            att = att + _nt_dot(stack_heads(q * part), (k * part).astype(BF16)) * masks_ref[d, lv]
        atts.append(att)
    av = jnp.dot(jnp.concatenate(atts, axis=0).astype(BF16), v16, preferred_element_type=F32)
    acc_ref[rows, :] = sum(
        jnp.concatenate([av[(GLA_PAIR * d + hh) * CHUNK:(GLA_PAIR * d + hh + 1) * CHUNK, hh * GLA_DV:(hh + 1) * GLA_DV]
                         for hh in range(GLA_PAIR)], axis=1) for d in range(2))


def _gla_recur(chunks, v_ref, qr_s, ke_s, tot_s, st_refs, acc_ref):
    outs = []
    for d, c in enumerate(chunks):
        rows = pl.ds(pl.multiple_of(c * CHUNK, CHUNK), CHUNK)
        st = st_refs[d][...]
        outs.append((rows, _nt_dot(qr_s[d, rows, :], st.astype(BF16))))
        upd = _tn_dot(v_ref[0, rows, :].astype(BF16), ke_s[d, rows, :])
        srow = lax.broadcasted_iota(jnp.int32, upd.shape, 0) // GLA_DV
        scol = lax.broadcasted_iota(jnp.int32, upd.shape, 1) // GLA_DK
        st_refs[d][...] = jnp.where(srow == scol, st * tot_s[d, pl.ds(c, 1), :] + upd, 0.0)
    for rows, o in outs:
        acc_ref[rows, :] += o


def _gla_finish(acc_ref, z_ref, norm_ref, o_ref, n_rows):
    tile = min(256, n_rows)
    for i in range(n_rows // tile):
        rows = slice(i * tile, (i + 1) * tile)
        o = acc_ref[rows, :]
        z = z_ref[0, rows, :]
        y = jnp.concatenate([_rms_rows(o[:, hh * GLA_DV:(hh + 1) * GLA_DV], norm_ref[...])
                             for hh in range(GLA_PAIR)], axis=1)
        o_ref[0, rows, :] = (y * (z * jax.nn.sigmoid(z))).astype(o_ref.dtype)


def _gla_kernel(q_ref, k_ref, v_ref, z_ref, gs_ref, qc_ref, kc_ref, vc_ref, zc_ref, gsc_ref, cos_ref, sin_ref,
                wg_ref, bgate_ref, norm_ref, sums_ref, masks_ref, o_ref, oc_ref, stf_ref, stb_ref, qr_s, ke_s, tot_s,
                acc_ref):
    st_refs = (stf_ref, stb_ref)
    for r in st_refs:
        r[...] = jnp.zeros_like(r)
    for (qq, kk, vv, gg), rope, zz_ref, out_ref in (((qc_ref, kc_ref, vc_ref, gsc_ref), None, zc_ref, oc_ref),
                                                    ((q_ref, k_ref, v_ref, gs_ref), (cos_ref, sin_ref), z_ref, o_ref)):
        n = qq.shape[1]
        n_chunks = n // CHUNK

        def local_step(c, carry, qq=qq, kk=kk, vv=vv, gg=gg, rope=rope):
            _gla_local(c, qq, kk, vv, gg, rope, wg_ref, bgate_ref, sums_ref, masks_ref, qr_s, ke_s, tot_s, acc_ref)
            return carry

        def recur_step(i, carry, vv=vv, n_chunks=n_chunks):
            _gla_recur((i, n_chunks - 1 - i), vv, qr_s, ke_s, tot_s, st_refs, acc_ref)
            return carry

        lax.fori_loop(0, n_chunks, local_step, 0, unroll=2)
        lax.fori_loop(0, n_chunks, recur_step, 0)
        _gla_finish(acc_ref, zz_ref, norm_ref, out_ref, n)


def gla_branch(p_main, p_small, pc_main, pc_small, wg2, bg, norm):
    b, t, _ = p_main.shape
    tc = pc_main.shape[1]
    nj = GLA_HEADS // GLA_PAIR
    vw = GLA_PAIR * GLA_DV
    o_q = MY_OFF["gla_q"][0] // LANE
    o_k = MY_OFF["gla_k"][0] // LANE
    o_v = MY_OFF["gla_v"][0] // vw
    o_z = MY_OFF["gla_z"][0] // vw
    sums, masks = _gla_tables()
    cos, sin = _gla_rope_tables(t)
    wge, bge = _gla_gate_weights(wg2.astype(F32), bg.astype(F32))

    def seq(n, w, o):
        return pl.BlockSpec((1, n, w), lambda i, j: (i, 0, o + j))

    def whole(n):
        return pl.BlockSpec((1, n, N_SMALL), lambda i, j: (i, 0, 0))

    def const(shape):
        return pl.BlockSpec(shape, lambda i, j: (0,) * len(shape))

    return pl.pallas_call(
        _gla_kernel,
        grid=(b, nj),
        in_specs=[seq(t, LANE, o_q), seq(t, LANE, o_k), seq(t, vw, o_v), seq(t, vw, o_z), whole(t),
                  seq(tc, LANE, o_q), seq(tc, LANE, o_k), seq(tc, vw, o_v), seq(tc, vw, o_z), whole(tc),
                  const((t, LANE)), const((t, LANE)),
                  pl.BlockSpec((2, 1, N_SMALL, LANE), lambda i, j: (0, j, 0, 0)),
                  pl.BlockSpec((2, 1, 1, LANE), lambda i, j: (0, j, 0, 0)),
                  const((1, GLA_DV)), const((2, GLA_SUM_ROWS, CHUNK)),
                  const((2, GLA_LEVELS + 1, GLA_PAIR * CHUNK, CHUNK))],
        out_specs=[pl.BlockSpec((1, t, vw), lambda i, j: (i, 0, j)), pl.BlockSpec((1, tc, vw), lambda i, j: (i, 0, j))],
        out_shape=[jax.ShapeDtypeStruct((b, t, BR_W), BF16), jax.ShapeDtypeStruct((b, tc, BR_W), BF16)],
        scratch_shapes=[pltpu.VMEM((vw, LANE), F32), pltpu.VMEM((vw, LANE), F32),
                        pltpu.VMEM((2, t, LANE), BF16), pltpu.VMEM((2, t, LANE), BF16),
                        pltpu.VMEM((2, t // CHUNK, LANE), F32),
                        pltpu.VMEM((t, vw), F32)],
        compiler_params=pltpu.CompilerParams(vmem_limit_bytes=VMEM_LIMIT),
        name="gla_scan",
    )(p_main, p_main, p_main, p_main, p_small, pc_main, pc_main, pc_main, pc_main, pc_small,
      jnp.asarray(cos), jnp.asarray(sin), wge, bge, norm.astype(F32).reshape(1, GLA_DV),
      jnp.asarray(sums, BF16), jnp.asarray(masks))


GDN_PAIR = 2 * CHUNK
GDN_STACK = 2 * GDN_PAIR


def _neumann(a):
    r = -a
    p = a
    for _ in range(int(math.log2(CHUNK)) - 1):
        p16 = p.astype(BF16)
        p = jnp.dot(p16, p16, preferred_element_type=F32)
        r = r + p + jnp.dot(r.astype(BF16), p.astype(BF16), preferred_element_type=F32)
    return r


def _seg_scan(x, reverse):
    n = x.shape[0]
    pos = lax.broadcasted_iota(jnp.int32, x.shape, 0) % CHUNK
    s = 1
    while s < CHUNK:
        if reverse:
            x = x + jnp.where(pos < CHUNK - s, pltpu.roll(x, n - s, axis=0), 0.0)
        else:
            x = x + jnp.where(pos >= s, pltpu.roll(x, s, axis=0), 0.0)
        s *= 2
    return x


def _gdn_prepare(q_ref, k_ref, v_ref, gs_ref, wq_ref, wk_ref, wv_ref, sel_ref, coef_ref, qkv_s, gb_s):
    n = q_ref.shape[1]
    for i, (x_ref, w_ref) in enumerate(((q_ref, wq_ref), (k_ref, wk_ref), (v_ref, wv_ref))):
        y = _dwconv_rows(x_ref[0], w_ref, GDN_CONV)
        y = y * jax.nn.sigmoid(y)
        if i < 2:
            y = y * lax.rsqrt(jnp.sum(y * y, axis=-1, keepdims=True) + EPS)
        if i == 0:
            y = y * (GDN_DH ** -0.5)
        qkv_s[i, 0:n, :] = y
    gs_hi, gs_lo = _hi_lo(gs_ref[0])

    def pick(i):
        return (jnp.dot(gs_hi, sel_ref[0, i], preferred_element_type=F32)
                + jnp.dot(gs_lo, sel_ref[0, i], preferred_element_type=F32))

    for d in range(2):
        a = pick(d)
        bb = pick(2 + d)
        g = -jnp.exp(coef_ref[0, d:d + 1, :]) * _softplus(a + coef_ref[0, 2 + d:3 + d, :])
        run = _seg_scan(g, reverse=(d == 1))
        gb_s[d, 0:n, :] = run
        gb_s[2 + d, 0:n, :] = jax.nn.sigmoid(bb)
        last = 0 if d else CHUNK - 1
        total = run.reshape(n // CHUNK, CHUNK, LANE)[:, last:last + 1, :]
        total = jnp.broadcast_to(total, (n // CHUNK, CHUNK, LANE)).reshape(n, LANE)
        gb_s[4 + d, 0:n, :] = jnp.exp(total - run)


def _gdn_local(p, qkv_s, gb_s, negm_ref, x_s, b_s, qp_s, op_s):
    rows = pl.ds(pl.multiple_of(p * GDN_PAIR, GDN_PAIR), GDN_PAIR)
    q = qkv_s[0, rows, :]
    k = qkv_s[1, rows, :]
    v = qkv_s[2, rows, :]
    gam = [gb_s[d, rows, :] for d in range(2)]
    beta = [gb_s[2 + d, rows, :] for d in range(2)]
    egam = [jnp.exp(g) for g in gam]
    kb = [k * bt for bt in beta]
    k16 = k.astype(BF16)
    raw = _nt_dot(jnp.concatenate([kb[0], kb[1], q, q], axis=0).astype(BF16),
                  jnp.concatenate([k16, k16], axis=0))
    g2 = jnp.concatenate(gam, axis=0)
    g2t = g2.T
    diff = jnp.concatenate([g2, g2], axis=1) - jnp.concatenate([g2t, g2t], axis=0)
    dec = jnp.exp(jnp.minimum(diff, 0.0) + negm_ref[...])
    eye = (lax.broadcasted_iota(jnp.int32, dec.shape, 0) == lax.broadcasted_iota(jnp.int32, dec.shape, 1)).astype(F32)
    a_kk = raw[0:GDN_STACK] * dec
    a_qk = raw[GDN_STACK:] * (dec + eye)
    r = _neumann(a_kk)
    rhs = jnp.concatenate([jnp.concatenate([v * beta[d], kb[d] * egam[d]], axis=1) for d in range(2)], axis=0)
    sol = rhs + jnp.dot(r.astype(BF16), rhs.astype(BF16), preferred_element_type=F32)
    sol16 = sol.astype(BF16)
    ao = jnp.dot(a_qk.astype(BF16), sol16, preferred_element_type=F32)
    for d in range(2):
        blk = slice(d * GDN_PAIR, (d + 1) * GDN_PAIR)
        op_s[d, rows, :] = ao[blk, 0:GDN_DH]
        qp_s[d, rows, :] = (q * egam[d] - ao[blk, GDN_DH:]).astype(BF16)
        kd = (k * gb_s[4 + d, rows, :]).astype(BF16)
        for jj in range(2):
            sub = slice(jj * CHUNK, (jj + 1) * CHUNK)
            bx = _tn_dot(kd[sub], sol16[d * GDN_PAIR + jj * CHUNK:d * GDN_PAIR + (jj + 1) * CHUNK])
            b_s[d, 2 * p + jj] = bx[:, 0:GDN_DH]
            x_s[d, 2 * p + jj] = bx[:, GDN_DH:].astype(BF16)


def _gdn_recur(chunks, gb_s, x_s, b_s, qp_s, op_s, st_refs, acc_ref):
    outs = []
    for d, c in enumerate(chunks):
        rows = pl.ds(pl.multiple_of(c * CHUNK, CHUNK), CHUNK)
        st = st_refs[d][...]
        xs = jnp.dot(jnp.concatenate([x_s[d, c], qp_s[d, rows, :]], axis=0), st.astype(BF16),
                     preferred_element_type=F32)
        outs.append((rows, xs[GDN_DH:] + op_s[d, rows, :]))
        last = c * CHUNK + (0 if d else CHUNK - 1)
        st_refs[d][...] = st * jnp.exp(gb_s[d, pl.ds(last, 1), :]) + (b_s[d, c] - xs[0:GDN_DH])
    for rows, o in outs:
        acc_ref[rows, :] += o


def _gdn_finish(acc_ref, z_ref, norm_ref, o_ref, n_rows):
    tile = min(256, n_rows)
    for i in range(n_rows // tile):
        rows = slice(i * tile, (i + 1) * tile)
        z = z_ref[0, rows, :]
        o_ref[0, rows, :] = (_rms_rows(acc_ref[rows, :], norm_ref[...]) * (z * jax.nn.sigmoid(z))).astype(o_ref.dtype)


def _gdn_kernel(q_ref, k_ref, v_ref, z_ref, gs_ref, qc_ref, kc_ref, vc_ref, zc_ref, gsc_ref, wq_ref, wk_ref, wv_ref,
                sel_ref, coef_ref, norm_ref, negm_ref, o_ref, oc_ref, stf_ref, stb_ref, qkv_s, gb_s,
                x_s, b_s, qp_s, op_s, acc_ref):
    st_refs = (stf_ref, stb_ref)
    for r in st_refs:
        r[...] = jnp.zeros_like(r)
    for refs, zz_ref, out_ref in (((qc_ref, kc_ref, vc_ref, gsc_ref), zc_ref, oc_ref),
                                  ((q_ref, k_ref, v_ref, gs_ref), z_ref, o_ref)):
        n = refs[0].shape[1]
        n_chunks = n // CHUNK
        _gdn_prepare(*refs, wq_ref, wk_ref, wv_ref, sel_ref, coef_ref, qkv_s, gb_s)
        acc_ref[0:n, :] = jnp.zeros((n, LANE), F32)

        def local_step(p, carry):
            _gdn_local(p, qkv_s, gb_s, negm_ref, x_s, b_s, qp_s, op_s)
            return carry

        def recur_step(i, carry, n_chunks=n_chunks):
            _gdn_recur((i, n_chunks - 1 - i), gb_s, x_s, b_s, qp_s, op_s, st_refs, acc_ref)
            return carry

        lax.fori_loop(0, n // GDN_PAIR, local_step, 0, unroll=2)
        lax.fori_loop(0, n_chunks, recur_step, 0)
        _gdn_finish(acc_ref, zz_ref, norm_ref, out_ref, n)


def _gdn_order_mask():
    ti = np.arange(GDN_STACK)[:, None]
    si = np.arange(GDN_STACK)[None, :]
    same = (ti // CHUNK) == (si // CHUNK)
    earlier = same & np.where(ti < GDN_PAIR, si < ti, si > ti)
    return np.where(earlier, 0.0, NA_NEG).astype(np.float32)


def gdn_branch(p_main, p_small, pc_main, pc_small, conv_w, a_log, dt_bias, norm):
    b, t, _ = p_main.shape
    tc = pc_main.shape[1]
    assert t % GDN_PAIR == 0 and tc % GDN_PAIR == 0 and tc <= t
    nh = GDN_HEADS
    o_q = MY_OFF["gdn_qkv"][0] // LANE
    o_z = MY_OFF["gdn_z"][0] // LANE
    wt = conv_w.T.astype(F32)
    sel = np.zeros((nh, 4, N_SMALL, LANE), np.float32)
    for h in range(nh):
        for d in range(2):
            sel[h, d, MY_OFF["gdn_a"][0] + d * nh + h, :] = 1.0
            sel[h, 2 + d, MY_OFF["gdn_b"][0] + d * nh + h, :] = 1.0
    coef = jnp.concatenate([a_log.astype(F32).T, dt_bias.astype(F32).T], axis=1)
    coef = jnp.broadcast_to(coef[:, :, None], (nh, 4, LANE))

    def seq(n, o):
        return pl.BlockSpec((1, n, LANE), lambda i, j: (i, 0, o + j))

    def whole(n):
        return pl.BlockSpec((1, n, N_SMALL), lambda i, j: (i, 0, 0))

    def wspec(o):
        return pl.BlockSpec((GDN_CONV, LANE), lambda i, j: (0, o + j))

    return pl.pallas_call(
        _gdn_kernel,
        grid=(b, nh),
        in_specs=[seq(t, o_q), seq(t, o_q + nh), seq(t, o_q + 2 * nh), seq(t, o_z), whole(t),
                  seq(tc, o_q), seq(tc, o_q + nh), seq(tc, o_q + 2 * nh), seq(tc, o_z), whole(tc),
                  wspec(0), wspec(nh), wspec(2 * nh),
                  pl.BlockSpec((1, 4, N_SMALL, LANE), lambda i, j: (j, 0, 0, 0)),
                  pl.BlockSpec((1, 4, LANE), lambda i, j: (j, 0, 0)),
                  pl.BlockSpec((1, GDN_DH), lambda i, j: (0, 0)),
                  pl.BlockSpec((GDN_STACK, GDN_STACK), lambda i, j: (0, 0))],
        out_specs=[pl.BlockSpec((1, t, LANE), lambda i, j: (i, 0, j)), pl.BlockSpec((1, tc, LANE), lambda i, j: (i, 0, j))],
        out_shape=[jax.ShapeDtypeStruct((b, t, BR_W), BF16), jax.ShapeDtypeStruct((b, tc, BR_W), BF16)],
        scratch_shapes=[pltpu.VMEM((GDN_DH, GDN_DH), F32), pltpu.VMEM((GDN_DH, GDN_DH), F32),
                        pltpu.VMEM((3, t, LANE), F32), pltpu.VMEM((6, t, LANE), F32),
                        pltpu.VMEM((2, t // CHUNK, GDN_DH, GDN_DH), BF16),
                        pltpu.VMEM((2, t // CHUNK, GDN_DH, GDN_DH), F32),
                        pltpu.VMEM((2, t, LANE), BF16), pltpu.VMEM((2, t, LANE), F32),
                        pltpu.VMEM((t, LANE), F32)],
        compiler_params=pltpu.CompilerParams(vmem_limit_bytes=VMEM_LIMIT),
        name="gdn_scan",
    )(p_main, p_main, p_main, p_main, p_small, pc_main, pc_main, pc_main, pc_main, pc_small,
      wt, wt, wt, jnp.asarray(sel, BF16), coef, norm.astype(F32).reshape(1, GDN_DH), jnp.asarray(_gdn_order_mask()))


HY_COLS = 256
HY_ROWS = 512


def dft_matrices(length):
    n = 2 * length
    inner = 64
    assert length % inner == 0
    r = jnp.arange(n, dtype=jnp.int32)
    k = jnp.where(r <= length, r, r - length)[:, None]
    a1 = ((k * (inner * jnp.arange(length // inner, dtype=jnp.int32))[None, :]) % n).astype(F32) * (2.0 * math.pi / n)
    a0 = ((k * jnp.arange(inner, dtype=jnp.int32)[None, :]) % n).astype(F32) * (2.0 * math.pi / n)
    c1, s1 = jnp.cos(a1)[:, :, None], jnp.sin(a1)[:, :, None]
    c0, s0 = jnp.cos(a0)[:, None, :], jnp.sin(a0)[:, None, :]
    is_cos = (r <= length)[:, None, None]
    f = jnp.where(is_cos, c1 * c0 - s1 * s0, -(s1 * c0 + c1 * s0)).reshape(n, length)
    w = (jnp.where((r == 0) | (r == length), 1.0, 2.0) / n)[:, None]
    return f.astype(BF16), (f * w).T.astype(BF16)


def _hy_embedding(length):
    t = np.linspace(0.0, 1.0, length)[:, None]
    bands = (HY_EMB - 1) // 2
    wpos = 2.0 * math.pi * np.arange(length)[:, None] / length
    f = np.linspace(1e-4, bands - 1, bands)[None]
    z = np.concatenate([t, np.cos(f * wpos), -np.sin(f * wpos)], axis=-1)
    z = np.pad(z, ((0, 0), (0, HY_FFN - HY_EMB)))
    deltas = np.abs(np.linspace(math.log(HY_TARGET) / HY_SLOW, math.log(HY_TARGET) / HY_FAST, BR_W))[None]
    return z.astype(np.float32), deltas.astype(np.float32)


def _hy_filter_kernel(z_ref, win_ref, bin_ref, wmid_ref, bmid_ref, freq_ref, wout_ref, delta_ref, o_ref):
    hp = lax.Precision.HIGHEST
    length = z_ref.shape[0]
    h = jnp.sin(freq_ref[0] * (jnp.dot(z_ref[...], win_ref[...], precision=hp, preferred_element_type=F32)
                               + bin_ref[...]))
    for i in range(HY_INNER):
        h = jnp.sin(freq_ref[i + 1] * (jnp.dot(h, wmid_ref[i], precision=hp, preferred_element_type=F32)
                                       + bmid_ref[i]))
    h = jnp.dot(h, wout_ref[...], precision=hp, preferred_element_type=F32)
    t = lax.broadcasted_iota(jnp.int32, (length, BR_W), 0).astype(F32) * (1.0 / (length - 1))
    decay = jnp.exp(-t * delta_ref[...])
    h_f = h[:, :BR_W] * decay
    h_b = h[:, BR_W:] * decay
    for i, a in enumerate((h_f + h_b, h_f - h_b)):
        hi = a.astype(BF16)
        lo = (a - hi.astype(F32)).astype(BF16)
        o_ref[:, (2 * i) * BR_W:(2 * i + 1) * BR_W] = hi
        o_ref[:, (2 * i + 1) * BR_W:(2 * i + 2) * BR_W] = lo


def _hy_spectrum_kernel(f_ref, a_ref, ks_ref, kd_ref, *, length):
    i = pl.program_id(0)
    tm = f_ref.shape[0]
    r = jnp.dot(f_ref[...], a_ref[...], preferred_element_type=F32)
    ks_ref[...] = r[:, :BR_W] + r[:, BR_W:2 * BR_W]
    kd = r[:, 2 * BR_W:3 * BR_W] + r[:, 3 * BR_W:]
    row = lax.broadcasted_iota(jnp.int32, kd.shape, 0) + i * tm
    kd_ref[...] = jnp.where(row == length, 0.0, kd)


def hyena_filter_spectrum(length, fwd, filt):
    w_in, b_in, w_mid, b_mid, freq, w_out = filt
    z, deltas = _hy_embedding(length)
    n = 2 * length
    parts = pl.pallas_call(
        _hy_filter_kernel,
        out_shape=jax.ShapeDtypeStruct((length, 4 * BR_W), BF16),
        compiler_params=pltpu.CompilerParams(vmem_limit_bytes=VMEM_LIMIT),
        name="hyena_filter",
    )(jnp.asarray(z), jnp.pad(w_in, ((0, HY_FFN - HY_EMB), (0, 0))), b_in.reshape(1, HY_FFN), w_mid,
      b_mid.reshape(HY_INNER, 1, HY_FFN), freq.reshape(HY_INNER + 1, 1, HY_FFN), w_out, jnp.asarray(deltas))
    tm = min(HY_ROWS, n)
    return pl.pallas_call(
        functools.partial(_hy_spectrum_kernel, length=length),
        grid=(n // tm,),
        in_specs=[pl.BlockSpec((tm, length), lambda i: (i, 0)),
                  pl.BlockSpec((length, 4 * BR_W), lambda i: (0, 0))],
        out_specs=[pl.BlockSpec((tm, BR_W), lambda i: (i, 0)), pl.BlockSpec((tm, BR_W), lambda i: (i, 0))],
        out_shape=[jax.ShapeDtypeStruct((n, BR_W), F32), jax.ShapeDtypeStruct((n, BR_W), F32)],
        compiler_params=pltpu.CompilerParams(vmem_limit_bytes=VMEM_LIMIT),
        name="hyena_spectrum",
    )(fwd, parts)


def _shift_rows(x, off):
    t = x.shape[0]
    edge = 8
    assert 0 < abs(off) < edge and t % edge == 0
    rolled = pltpu.roll(x, (-off) % t, axis=0)
    row = lax.broadcasted_iota(jnp.int32, (edge, x.shape[1]), 0)
    if off < 0:
        return jnp.concatenate([jnp.where(row >= -off, rolled[0:edge], 0.0), rolled[edge:]], axis=0)
    return jnp.concatenate([rolled[0:t - edge], jnp.where(row < edge - off, rolled[t - edge:], 0.0)], axis=0)


def _dwconv_rows(x, w_ref, k):
    out = x * w_ref[k // 2:k // 2 + 1, :]
    for j in range(k):
        if j != k // 2:
            out = out + _shift_rows(x, j - k // 2) * w_ref[j:j + 1, :]
    return out


def _hy_prep_kernel(x0_ref, x1_ref, v_ref, z_ref, w0_ref, w1_ref, wv_ref, b0_ref, b1_ref, bv_ref, vx_ref, x0g_ref):
    x0 = _dwconv_rows(x0_ref[0], w0_ref, HY_CONV) + b0_ref[...]
    x1 = _dwconv_rows(x1_ref[0], w1_ref, HY_CONV) + b1_ref[...]
    v = _dwconv_rows(v_ref[0], wv_ref, HY_CONV) + bv_ref[...]
    z = z_ref[0]
    vx_ref[0] = v * x1
    x0g_ref[0] = x0 * (z * jax.nn.sigmoid(z))


def hyena_prep(p_main, conv_w, conv_b):
    b, length, _ = p_main.shape
    cb = LANE
    nj = BR_W // cb
    o_xv = MY_OFF["hy_xv"][0] // cb
    o_z = MY_OFF["hy_z"][0] // cb
    wt = conv_w.T
    bt = conv_b.reshape(1, 3 * BR_W)
    dspec = lambda o: pl.BlockSpec((1, length, cb), lambda i, j: (i, 0, o + j))
    wspec = lambda o: pl.BlockSpec((HY_CONV, cb), lambda i, j: (0, o + j))
    bspec = lambda o: pl.BlockSpec((1, cb), lambda i, j: (0, o + j))
    ospec = pl.BlockSpec((1, length, cb), lambda i, j: (i, 0, j))
    return pl.pallas_call(
        _hy_prep_kernel,
        grid=(b, nj),
        in_specs=[dspec(o_xv), dspec(o_xv + nj), dspec(o_xv + 2 * nj), dspec(o_z),
                  wspec(0), wspec(nj), wspec(2 * nj), bspec(0), bspec(nj), bspec(2 * nj)],
        out_specs=[ospec, ospec],
        out_shape=[jax.ShapeDtypeStruct((b, length, BR_W), F32)] * 2,
        compiler_params=pltpu.CompilerParams(vmem_limit_bytes=VMEM_LIMIT),
        name="hyena_prep",
    )(p_main, p_main, p_main, p_main, wt, wt, wt, bt, bt, bt)


def _hy_fwd_kernel(vx_ref, f_ref, kr_ref, ki_ref, knyq_ref, y_ref, *, length):
    v = vx_ref[0].astype(BF16)
    tm = min(HY_ROWS, length)
    for kt in range(length // tm):
        lo = kt * tm
        a = jnp.dot(f_ref[lo:lo + tm, :], v, preferred_element_type=F32)
        bh = jnp.dot(f_ref[length + lo:length + lo + tm, :], v, preferred_element_type=F32)
        kr = kr_ref[lo:lo + tm, :]
        ki = ki_ref[lo:lo + tm, :]
        if kt == 0:
            row = lax.broadcasted_iota(jnp.int32, kr.shape, 0)
            kb = jnp.where(row == 0, knyq_ref[0:1, :], kr)
        else:
            kb = kr
        y_ref[0, lo:lo + tm, :] = (a * kr - bh * ki).astype(BF16)
        y_ref[0, length + lo:length + lo + tm, :] = (a * ki + bh * kb).astype(BF16)


def _hy_inv_kernel(y_ref, inv_ref, vx_ref, x0g_ref, skip_ref, o_ref, *, length):
    tm = min(HY_ROWS, length)
    y = y_ref[0]
    for tt in range(length // tm):
        lo = tt * tm
        conv = jnp.dot(inv_ref[lo:lo + tm, :], y, preferred_element_type=F32)
        o_ref[0, lo:lo + tm, :] = ((conv + skip_ref[...] * vx_ref[0, lo:lo + tm, :])
                                   * x0g_ref[0, lo:lo + tm, :]).astype(o_ref.dtype)


def hyena_branch(p_main, conv_w, conv_b, skip, fwd, inv, ks, kd):
    b, length, _ = p_main.shape
    n = 2 * length
    vx, x0g = hyena_prep(p_main, conv_w, conv_b)
    cb = HY_COLS
    nj = BR_W // cb
    resident = lambda shape: pl.BlockSpec(shape, lambda j, i: (0, 0), pipeline_mode=pl.Buffered(1))
    y = pl.pallas_call(
        functools.partial(_hy_fwd_kernel, length=length),
        grid=(nj, b),
        in_specs=[pl.BlockSpec((1, length, cb), lambda j, i: (i, 0, j)),
                  resident((n, length)),
                  pl.BlockSpec((length, cb), lambda j, i: (0, j)),
                  pl.BlockSpec((length, cb), lambda j, i: (1, j)),
                  pl.BlockSpec((8, cb), lambda j, i: (length // 8, j))],
        out_specs=pl.BlockSpec((1, n, cb), lambda j, i: (i, 0, j)),
        out_shape=jax.ShapeDtypeStruct((b, n, BR_W), BF16),
        compiler_params=pltpu.CompilerParams(vmem_limit_bytes=VMEM_LIMIT),
        name="hyena_dft",
    )(vx, fwd, ks, kd, ks)
    return pl.pallas_call(
        functools.partial(_hy_inv_kernel, length=length),
        grid=(nj, b),
        in_specs=[pl.BlockSpec((1, n, cb), lambda j, i: (i, 0, j)),
                  resident((length, n)),
                  pl.BlockSpec((1, length, cb), lambda j, i: (i, 0, j)),
                  pl.BlockSpec((1, length, cb), lambda j, i: (i, 0, j)),
                  pl.BlockSpec((1, cb), lambda j, i: (0, j))],
        out_specs=pl.BlockSpec((1, length, cb), lambda j, i: (i, 0, j)),
        out_shape=jax.ShapeDtypeStruct((b, length, BR_W), BF16),
        compiler_params=pltpu.CompilerParams(vmem_limit_bytes=VMEM_LIMIT),
        name="hyena_idft",
    )(y, inv, vx, x0g, skip.reshape(1, BR_W))


def mixers(p_main, p_small, pc_main, pc_small, na_rpb, gla_wg2, gla_bg, gla_norm, gdn_conv, gdn_a_log, gdn_dt_bias,
           gdn_norm, hy_conv, hy_conv_b, hy_filt, hy_skip, with_ctx, dft_lat, dft_ctx):
    t = p_main.shape[1]
    ya = na_branch(p_main, pc_main, na_rpb)
    yb, yb_c = gla_branch(p_main, p_small, pc_main, pc_small, gla_wg2, gla_bg, gla_norm)
    yc, yc_c = gdn_branch(p_main, p_small, pc_main, pc_small, gdn_conv, gdn_a_log, gdn_dt_bias, gdn_norm)
    fwd, inv = dft_lat
    ks, kd = hyena_filter_spectrum(t, fwd, hy_filt)
    yd = hyena_branch(p_main, hy_conv, hy_conv_b, hy_skip, fwd, inv, ks, kd)
    if not with_ctx:
        return (ya, yb, yc, yd), None
    ya_c = ctx_attention_branch(pc_main)
    fwd_c, inv_c = dft_ctx
    ks_c, kd_c = hyena_filter_spectrum(pc_main.shape[1], fwd_c, hy_filt)
    yd_c = hyena_branch(pc_main, hy_conv, hy_conv_b, hy_skip, fwd_c, inv_c, ks_c, kd_c)
    return (ya, yb, yc, yd), (ya_c, yb_c, yc_c, yd_c)


def kernel(x, c, ctx, c_ctx, w_mod, b_mod, g_pre, g_post, w_in, w_gate, b_gate, w_branch, w_out, na_rpb, gla_wg2, gla_bg, gla_norm, gdn_conv, gdn_a_log, gdn_dt_bias, gdn_norm, hy_conv, hy_conv_b, hy_w_in, hy_b_in, hy_w_mid, hy_b_mid, hy_freq, hy_w_out, hy_skip):
    b, t, d = x.shape
    tc = ctx.shape[1]
    h = x.reshape(b * t, d)
    hc = ctx.reshape(b * tc, d)
    c_all = jnp.concatenate([c, c_ctx[None], jnp.zeros((7, d), F32)], axis=0)
    dft_lat = dft_matrices(t)
    dft_ctx = dft_matrices(tc)
    for l in range(DEPTH):
        with_ctx = l < DEPTH - 1
        mod = modulation(c_all, w_mod[l], b_mod[l])
        shift, scale, gate = [mod[:, i * d:(i + 1) * d].reshape(-1, 1, d) for i in range(3)]
        u = prenorm_mod(h, g_pre[l], scale[:b], shift[:b], t)
        uc = prenorm_mod(hc, g_pre[l], scale[b:b + 1], shift[b:b + 1], b * tc)
        w_main, w_small = _permute_w_in(w_in[l])
        p_main = matmul(u, w_main, 1024, 1536, name="in_proj").reshape(b, t, N_MAIN)
        p_small = matmul(u, w_small, 1024, N_SMALL, name="in_proj_small").reshape(b, t, N_SMALL)
        pc_main = matmul(uc, w_main, 1024, 1536, name="in_proj_ctx").reshape(b, tc, N_MAIN)
        pc_small = matmul(uc, w_small, 1024, N_SMALL, name="in_proj_ctx_small").reshape(b, tc, N_SMALL)
        ys, ycs = mixers(p_main, p_small, pc_main, pc_small,
                         na_rpb[l], gla_wg2[l], gla_bg[l], gla_norm[l], gdn_conv[l], gdn_a_log[l], gdn_dt_bias[l],
                         gdn_norm[l], hy_conv[l], hy_conv_b[l],
                         (hy_w_in[l], hy_b_in[l], hy_w_mid[l], hy_b_mid[l], hy_freq[l], hy_w_out[l]),
                         hy_skip[l], with_ctx, dft_lat, dft_ctx)
        h_new = merge(u, ys, w_gate[l], b_gate[l], w_branch[l], w_out[l], h, gate[:b], g_post[l], t)
        if with_ctx:
            hc = merge(uc, ycs, w_gate[l], b_gate[l], w_branch[l], w_out[l], hc, gate[b:b + 1], g_post[l], b * tc)
        h = h_new
    return h.reshape(b, t, d)
```

```python
import functools
import math

import numpy as np
import jax
import jax.numpy as jnp
from jax import lax
from jax.experimental import pallas as pl
from jax.experimental.pallas import tpu as pltpu

D_MODEL = 2048
BATCH = 16
SEQ = 2048
DEPTH = 2
GRID_W = 64
CTX_LEN = 256
N_BRANCH = 4
BR_W = D_MODEL // N_BRANCH
NA_HEADS = 8
NA_DH = BR_W // NA_HEADS
WIN_R = 8
WIN_C = 16
QBLK_C = 16
KBLK_C = WIN_C + QBLK_C
N_CBLK = GRID_W // QBLK_C
GLA_HEADS = 4
GLA_DV = BR_W // GLA_HEADS
GLA_DK = GLA_DV // 2
GLA_RANK = 16
GLA_TAU = 16.0
CHUNK = 64
GDN_HEADS = 4
GDN_DH = BR_W // GDN_HEADS
GDN_CONV = 5
HY_CONV = 3
HY_EMB = 33
HY_FFN = 64
HY_INNER = 2
HY_TARGET = 1e-2
HY_FAST = 0.3
HY_SLOW = 1.5
ROPE_THETA = 10000.0
EPS = 1e-6
F32 = jnp.float32
BF16 = jnp.bfloat16

REF_SPLITS = (
    ("na_qkv", 3 * BR_W), ("na_z", BR_W),
    ("gla_q", GLA_HEADS * GLA_DK), ("gla_k", GLA_HEADS * GLA_DK), ("gla_v", BR_W), ("gla_z", BR_W),
    ("gla_g", 2 * GLA_RANK),
    ("gdn_qkv", 3 * BR_W), ("gdn_z", BR_W), ("gdn_a", 2 * GDN_HEADS), ("gdn_b", 2 * GDN_HEADS),
    ("hy_xv", 3 * BR_W), ("hy_z", BR_W),
)
MAIN_ORDER = ("na_qkv", "na_z", "gla_q", "gla_k", "gla_v", "gla_z", "gdn_qkv", "gdn_z", "hy_xv", "hy_z")
SMALL_ORDER = ("gla_g", "gdn_a", "gdn_b")
LANE = 128
N_MAIN = sum(w for n, w in REF_SPLITS if n in MAIN_ORDER)
N_SMALL = LANE
VMEM_LIMIT = 48 * 1024 * 1024

def _ref_offsets():
    out, o = {}, 0
    for name, w in REF_SPLITS:
        out[name] = (o, w)
        o += w
    return out


def _my_offsets():
    out, o = {}, 0
    for name in MAIN_ORDER:
        w = dict(REF_SPLITS)[name]
        out[name] = (o, w)
        o += w
    o = 0
    for name in SMALL_ORDER:
        w = dict(REF_SPLITS)[name]
        out[name] = (o, w)
        o += w
    return out


REF_OFF = _ref_offsets()
MY_OFF = _my_offsets()


def _permute_w_in(w_in_l):
    main = jnp.concatenate([w_in_l[:, REF_OFF[n][0]:REF_OFF[n][0] + REF_OFF[n][1]] for n in MAIN_ORDER], axis=1)
    small = jnp.concatenate([w_in_l[:, REF_OFF[n][0]:REF_OFF[n][0] + REF_OFF[n][1]] for n in SMALL_ORDER], axis=1)
    small = jnp.pad(small, ((0, 0), (0, N_SMALL - small.shape[1])))
    return main.astype(BF16), small.astype(BF16)


def _mod_kernel(c_ref, w_ref, b_ref, o_ref):
    c = c_ref[...]
    a = (c * jax.nn.sigmoid(c)).astype(BF16)
    o_ref[...] = jnp.dot(a, w_ref[...].astype(BF16), preferred_element_type=F32) + b_ref[...]


def modulation(c_all, w_mod_l, b_mod_l):
    r, d = c_all.shape
    n = w_mod_l.shape[1]
    tn = 768
    return pl.pallas_call(
        _mod_kernel,
        grid=(n // tn,),
        in_specs=[pl.BlockSpec((r, d), lambda j: (0, 0)),
                  pl.BlockSpec((d, tn), lambda j: (0, j)),
                  pl.BlockSpec((1, tn), lambda j: (0, j))],
        out_specs=pl.BlockSpec((r, tn), lambda j: (0, j)),
        out_shape=jax.ShapeDtypeStruct((r, n), F32),
        compiler_params=pltpu.CompilerParams(vmem_limit_bytes=VMEM_LIMIT),
        name="modulation",
    )(c_all, w_mod_l, b_mod_l.reshape(1, n))


def _prenorm_kernel(x_ref, g_ref, scale_ref, shift_ref, o_ref):
    x = x_ref[...]
    y = x * lax.rsqrt(jnp.mean(x * x, axis=-1, keepdims=True) + EPS)
    u = (y * g_ref[...]) * (1.0 + scale_ref[0]) + shift_ref[0]
    o_ref[...] = u.astype(o_ref.dtype)


def prenorm_mod(x2d, g_pre_l, scale, shift, rows_per_group, out_dtype=BF16):
    n, d = x2d.shape
    tm = 512
    tiles_per_group = rows_per_group // tm
    return pl.pallas_call(
        _prenorm_kernel,
        grid=(n // tm,),
        in_specs=[pl.BlockSpec((tm, d), lambda i: (i, 0)),
                  pl.BlockSpec((1, d), lambda i: (0, 0)),
                  pl.BlockSpec((1, 1, d), lambda i: (i // tiles_per_group, 0, 0)),
                  pl.BlockSpec((1, 1, d), lambda i: (i // tiles_per_group, 0, 0))],
        out_specs=pl.BlockSpec((tm, d), lambda i: (i, 0)),
        out_shape=jax.ShapeDtypeStruct((n, d), out_dtype),
        compiler_params=pltpu.CompilerParams(vmem_limit_bytes=VMEM_LIMIT),
        name="prenorm_mod",
    )(x2d, g_pre_l.reshape(1, d), scale, shift)


def _matmul_kernel(u_ref, w_ref, o_ref):
    o_ref[...] = jnp.dot(u_ref[...], w_ref[...], preferred_element_type=F32).astype(o_ref.dtype)


def matmul(u, w, tm, tn, out_dtype=F32, name="matmul"):
    n, k = u.shape
    m = w.shape[1]
    return pl.pallas_call(
        _matmul_kernel,
        grid=(m // tn, n // tm),
        in_specs=[pl.BlockSpec((tm, k), lambda j, i: (i, 0)),
                  pl.BlockSpec((k, tn), lambda j, i: (0, j))],
        out_specs=pl.BlockSpec((tm, tn), lambda j, i: (i, j)),
        out_shape=jax.ShapeDtypeStruct((n, m), out_dtype),
        compiler_params=pltpu.CompilerParams(vmem_limit_bytes=VMEM_LIMIT),
        name=name,
    )(u, w)


def _gate_merge_kernel(u_ref, y0_ref, y1_ref, y2_ref, y3_ref, wg_ref, bg_ref, wb_ref, o_ref):
    u = u_ref[...]
    merged = None
    for i, y_ref in enumerate((y0_ref, y1_ref, y2_ref, y3_ref)):
        g = jnp.dot(u, wg_ref[i], preferred_element_type=F32) + bg_ref[i]
        yb = jnp.dot(y_ref[...], wb_ref[i], preferred_element_type=F32)
        term = jax.nn.sigmoid(g) * yb
        merged = term if merged is None else merged + term
    o_ref[...] = merged.astype(o_ref.dtype)


def _out_proj_kernel(m_ref, wo_ref, h_ref, gate_ref, gpost_ref, o_ref):
    y = jnp.dot(m_ref[...], wo_ref[...], preferred_element_type=F32)
    yn = y * lax.rsqrt(jnp.mean(y * y, axis=-1, keepdims=True) + EPS) * gpost_ref[...]
    o_ref[...] = h_ref[...] + gate_ref[0] * yn


def merge(u, ys, w_gate_l, b_gate_l, w_branch_l, w_out_l, h2d, gate, g_post_l, rows_per_group):
    n, d = u.shape
    tm, tn = 1024, 256
    yspec = pl.BlockSpec((tm, BR_W), lambda i, j: (i, 0))
    merged = pl.pallas_call(
        _gate_merge_kernel,
        grid=(n // tm, d // tn),
        in_specs=[pl.BlockSpec((tm, d), lambda i, j: (i, 0)),
                  yspec, yspec, yspec, yspec,
                  pl.BlockSpec((N_BRANCH, d, tn), lambda i, j: (0, 0, j)),
                  pl.BlockSpec((N_BRANCH, 1, tn), lambda i, j: (0, 0, j)),
                  pl.BlockSpec((N_BRANCH, BR_W, tn), lambda i, j: (0, 0, j))],
        out_specs=pl.BlockSpec((tm, tn), lambda i, j: (i, j)),
        out_shape=jax.ShapeDtypeStruct((n, d), BF16),
        compiler_params=pltpu.CompilerParams(vmem_limit_bytes=VMEM_LIMIT),
        name="gate_merge",
    )(u, *[y.reshape(n, BR_W) for y in ys], w_gate_l.astype(BF16), b_gate_l.reshape(N_BRANCH, 1, d),
      w_branch_l.astype(BF16))
    tm = 512
    tiles_per_group = rows_per_group // tm
    return pl.pallas_call(
        _out_proj_kernel,
        grid=(n // tm,),
        in_specs=[pl.BlockSpec((tm, d), lambda i: (i, 0)),
                  pl.BlockSpec((d, d), lambda i: (0, 0), pipeline_mode=pl.Buffered(1)),
                  pl.BlockSpec((tm, d), lambda i: (i, 0)),
                  pl.BlockSpec((1, 1, d), lambda i: (i // tiles_per_group, 0, 0)),
                  pl.BlockSpec((1, d), lambda i: (0, 0))],
        out_specs=pl.BlockSpec((tm, d), lambda i: (i, 0)),
        out_shape=jax.ShapeDtypeStruct((n, d), F32),
        compiler_params=pltpu.CompilerParams(vmem_limit_bytes=VMEM_LIMIT),
        name="out_proj",
    )(merged, w_out_l.astype(BF16), h2d, gate, g_post_l.reshape(1, d))


NA_NEG = -1e30
NA_PAIR = LANE // NA_DH


NA_QROWS = 2
NA_KROWS = WIN_R + 2


def _na_pair_types(rows):
    assert rows % NA_QROWS == 0 and rows >= NA_KROWS
    layouts, index = [], []
    for r0 in range(0, rows, NA_QROWS):
        rs = [int(np.clip(r0 + i - WIN_R // 2, 0, rows - WIN_R)) for i in range(NA_QROWS)]
        u0 = min(rs[0], rows - NA_KROWS)
        lay = (r0 - u0,) + tuple(x - u0 for x in rs)
        if lay not in layouts:
            layouts.append(lay)
        index.append(layouts.index(lay))
    return layouts, np.asarray(index, np.int32)


def na_bias_table(rpb, rows):
    heads = rpb.shape[0]
    c = np.arange(GRID_W)[:, None]
    kc = np.arange(GRID_W)[None, :]
    cstart = np.clip(c - WIN_C // 2, 0, GRID_W - WIN_C)
    valid = (kc >= cstart) & (kc < cstart + WIN_C)
    dc = np.clip(kc - c + WIN_C - 1, 0, 2 * WIN_C - 2)
    onehot = ((dc[None] == np.arange(2 * WIN_C - 1)[:, None, None]) & valid[None]).astype(np.float32)
    by_dr = jnp.einsum("hrd,dck->hrck", rpb.astype(F32), jnp.asarray(onehot), precision=lax.Precision.HIGHEST)
    by_dr = jnp.where(valid, by_dr, NA_NEG)
    outside = jnp.full((heads, GRID_W, GRID_W), NA_NEG, F32)
    layouts, index = _na_pair_types(rows)
    tabs = []
    for lay in layouts:
        a, starts = lay[0], lay[1:]
        strips = []
        for i, st in enumerate(starts):
            tiles = [by_dr[:, j - a - i + WIN_R - 1] if st <= j < st + WIN_R else outside for j in range(NA_KROWS)]
            strips.append(jnp.concatenate(tiles, axis=-1))
        tabs.append(jnp.concatenate(strips, axis=1))
    return jnp.stack(tabs, axis=1), index


def _nt_dot(a, b):
    return lax.dot_general(a, b, (((1,), (1,)), ((), ())), preferred_element_type=F32)


def _na_kernel(lay_ref, q_ref, k_ref, v_ref, kc_ref, vc_ref, z_ref, bias_ref, o_ref, *, rows):
    nq = NA_QROWS * GRID_W
    nk = NA_KROWS * GRID_W
    lane = lax.broadcasted_iota(jnp.int32, (nq, LANE), 1)
    head_mask = [(lane >= hh * NA_DH) & (lane < (hh + 1) * NA_DH) for hh in range(NA_PAIR)]
    kc = kc_ref[0].astype(BF16)
    vc = vc_ref[0].astype(BF16)

    def body(i, carry):
        work = []
        for p in (2 * i, 2 * i + 1):
            r0 = p * NA_QROWS
            u0 = jnp.minimum(jnp.clip(r0 - WIN_R // 2, 0, rows - WIN_R), rows - NA_KROWS)
            q0 = pl.multiple_of(r0 * GRID_W, nq)
            k0 = pl.multiple_of(u0 * GRID_W, GRID_W)
            q = q_ref[0, pl.ds(q0, nq), :] * (NA_DH ** -0.5)
            qs = jnp.concatenate([jnp.where(m, q, 0.0) for m in head_mask], axis=0).astype(BF16)
            work.append(dict(q0=q0, k0=k0, lay=lay_ref[p], qs=qs))
        for w in work:
            w["s"] = _nt_dot(w["qs"], k_ref[0, pl.ds(w["k0"], nk), :].astype(BF16))
        for w in work:
            w["sc"] = _nt_dot(w["qs"], kc)
        for w in work:
            s = w["s"] + jnp.concatenate([bias_ref[hh, w["lay"]] for hh in range(NA_PAIR)], axis=0)
            m = jnp.maximum(jnp.max(s, axis=-1, keepdims=True), jnp.max(w["sc"], axis=-1, keepdims=True))
            pw = jnp.exp(s - m)
            pc = jnp.exp(w["sc"] - m)
            w["l"] = jnp.sum(pw, axis=-1, keepdims=True) + jnp.sum(pc, axis=-1, keepdims=True)
            w["pw"] = pw.astype(BF16)
            w["pc"] = pc.astype(BF16)
        for w in work:
            w["o"] = jnp.dot(w["pw"], v_ref[0, pl.ds(w["k0"], nk), :].astype(BF16), preferred_element_type=F32)
        for w in work:
            o = (w["o"] + jnp.dot(w["pc"], vc, preferred_element_type=F32)) / w["l"]
            out = o[0:nq]
            for hh in range(1, NA_PAIR):
                out = jnp.where(head_mask[hh], o[hh * nq:(hh + 1) * nq], out)
            z = z_ref[0, pl.ds(w["q0"], nq), :]
            o_ref[0, pl.ds(w["q0"], nq), :] = (out * (z * jax.nn.sigmoid(z))).astype(o_ref.dtype)
        return carry

    assert (rows // NA_QROWS) % 2 == 0
    lax.fori_loop(0, rows // NA_QROWS // 2, body, 0)


def na_branch(p_main, pc_main, rpb):
    b, t, _ = p_main.shape
    tc = pc_main.shape[1]
    rows = t // GRID_W
    nj = BR_W // LANE
    o_q = MY_OFF["na_qkv"][0] // LANE
    o_z = MY_OFF["na_z"][0] // LANE
    bias, layout_of_pair = na_bias_table(rpb, rows)
    lat = lambda o: pl.BlockSpec((1, t, LANE), lambda i, j: (i, 0, o + j))
    cx = lambda o: pl.BlockSpec((1, tc, LANE), lambda i, j: (i, 0, o + j))
    return pl.pallas_call(
        functools.partial(_na_kernel, rows=rows),
        grid=(b, nj),
        in_specs=[pl.BlockSpec(memory_space=pltpu.SMEM),
                  lat(o_q), lat(o_q + nj), lat(o_q + 2 * nj), cx(o_q + nj), cx(o_q + 2 * nj), lat(o_z),
                  pl.BlockSpec((NA_PAIR,) + bias.shape[1:], lambda i, j: (j, 0, 0, 0))],
        out_specs=pl.BlockSpec((1, t, LANE), lambda i, j: (i, 0, j)),
        out_shape=jax.ShapeDtypeStruct((b, t, BR_W), BF16),
        compiler_params=pltpu.CompilerParams(vmem_limit_bytes=VMEM_LIMIT),
        name="na_attention",
    )(jnp.asarray(layout_of_pair), p_main, p_main, p_main, pc_main, pc_main, p_main, bias)


def _ctx_attn_kernel(q_ref, k_ref, v_ref, z_ref, o_ref):
    tc = q_ref.shape[1]
    lane = lax.broadcasted_iota(jnp.int32, (tc, LANE), 1)
    q = q_ref[0] * (NA_DH ** -0.5)
    k = k_ref[0].astype(BF16)
    v = v_ref[0].astype(BF16)
    out = None
    for hh in range(NA_PAIR):
        mask = (lane >= hh * NA_DH) & (lane < (hh + 1) * NA_DH)
        s = _nt_dot(jnp.where(mask, q, 0.0).astype(BF16), k)
        p = jnp.exp(s - jnp.max(s, axis=-1, keepdims=True))
        o = jnp.dot(p.astype(BF16), v, preferred_element_type=F32) / jnp.sum(p, axis=-1, keepdims=True)
        out = o if out is None else jnp.where(mask, o, out)
    z = z_ref[0]
    o_ref[0] = (out * (z * jax.nn.sigmoid(z))).astype(o_ref.dtype)


def ctx_attention_branch(pc_main):
    b, tc, _ = pc_main.shape
    nj = BR_W // LANE
    o_q = MY_OFF["na_qkv"][0] // LANE
    o_z = MY_OFF["na_z"][0] // LANE
    cx = lambda o: pl.BlockSpec((1, tc, LANE), lambda i, j: (i, 0, o + j))
    return pl.pallas_call(
        _ctx_attn_kernel,
        grid=(b, nj),
        in_specs=[cx(o_q), cx(o_q + nj), cx(o_q + 2 * nj), cx(o_z)],
        out_specs=pl.BlockSpec((1, tc, LANE), lambda i, j: (i, 0, j)),
        out_shape=jax.ShapeDtypeStruct((b, tc, BR_W), BF16),
        compiler_params=pltpu.CompilerParams(vmem_limit_bytes=VMEM_LIMIT),
        name="ctx_attention",
    )(pc_main, pc_main, pc_main, pc_main)


def _hi_lo(x):
    hi = x.astype(BF16)
    return hi, (x - hi.astype(F32)).astype(BF16)


def _dot_hl(a, b):
    a1, a2 = _hi_lo(a)
    b1, b2 = _hi_lo(b)
    return (jnp.dot(a1, b1, preferred_element_type=F32) + jnp.dot(a1, b2, preferred_element_type=F32)
            + jnp.dot(a2, b1, preferred_element_type=F32))


def _tn_dot(a, b):
    return lax.dot_general(a, b, (((0,), (0,)), ((), ())), preferred_element_type=F32)


def _softplus(x):
    return jnp.maximum(x, 0.0) + jnp.log1p(jnp.exp(-jnp.abs(x)))


def _rms_rows(o, g):
    return o * lax.rsqrt(jnp.mean(o * o, axis=-1, keepdims=True) + EPS) * g


GLA_LEVELS = 6
GLA_PAIR = LANE // GLA_DK
GLA_SUM_ROWS = (2 + GLA_LEVELS) * CHUNK


def _gla_tables():
    assert CHUNK == 1 << GLA_LEVELS
    t = np.arange(CHUNK)[:, None]
    r = np.arange(CHUNK)[None, :]
    sums, masks = [], []
    for rev in (False, True):
        before = (r >= t) if rev else (r <= t)
        after = (r < t) if rev else (r > t)
        blocks = [before, after]
        lev_masks = []
        for lv in range(1, GLA_LEVELS + 1):
            seg = CHUNK >> lv
            st, sr = t // seg, r // seg
            q_half = (st % 2 == 0) if rev else (st % 2 == 1)
            k_half = ~q_half
            blocks.append((st == sr) & ((before & q_half) | (after & k_half)))
            lev_masks.append(q_half & (sr == (st + 1 if rev else st - 1)))
        lev_masks.append(r == t)
        sums.append(np.concatenate(blocks, axis=0))
        masks.append(np.stack([np.concatenate([m, m], axis=0) for m in lev_masks]))
    return np.stack(sums).astype(np.float32), np.stack(masks).astype(np.float32)


def _gla_rope_tables(t_len):
    pos = np.arange(t_len)
    row = (pos // GRID_W).astype(np.float64)
    col = (pos % GRID_W).astype(np.float64)
    n = GLA_DK // 4
    freqs = ROPE_THETA ** (-np.arange(n, dtype=np.float64) / n)
    ang = np.concatenate([row[:, None] * freqs, col[:, None] * freqs], axis=-1)
    cos = np.concatenate([np.cos(ang), np.cos(ang)] * GLA_PAIR, axis=-1)
    sin = np.concatenate([-np.sin(ang), np.sin(ang)] * GLA_PAIR, axis=-1)
    return cos.astype(np.float32), sin.astype(np.float32)


def _gla_gate_weights(wg2, bg):
    half = GLA_DK // 2
    lane = np.arange(LANE)
    w_all, b_all = [], []
    for d in range(2):
        wd, bd = [], []
        for j in range(GLA_HEADS // GLA_PAIR):
            cols = half * (GLA_PAIR * j + lane // GLA_DK) + lane % half
            w = jnp.zeros((N_SMALL, LANE), F32).at[MY_OFF["gla_g"][0] + d * GLA_RANK:
                                                    MY_OFF["gla_g"][0] + (d + 1) * GLA_RANK].set(wg2[d][:, cols])
            wd.append(w)
            bd.append(bg[d][cols].reshape(1, LANE))
        w_all.append(jnp.stack(wd))
        b_all.append(jnp.stack(bd))
    return jnp.stack(w_all), jnp.stack(b_all)


def _gla_local(chunks, q_ref, k_ref, v_ref, gs_ref, rope, wg_ref, bgate_ref, sums_ref, masks_ref, qr_s, ke_s, tot_s,
               acc_ref):
    lane = lax.broadcasted_iota(jnp.int32, (CHUNK, LANE), 1)
    first_half = (lane % GLA_DK) < GLA_DK // 2

    def stack_heads(a):
        return jnp.concatenate([jnp.where((lane >= hh * GLA_DK) & (lane < (hh + 1) * GLA_DK), a, 0.0)
                                for hh in range(GLA_PAIR)], axis=0).astype(BF16)

    per_chunk = []
    for c in chunks:
        rows = pl.ds(pl.multiple_of(c * CHUNK, CHUNK), CHUNK)
        q = q_ref[0, rows, :] * (GLA_DK ** -0.5)
        k = k_ref[0, rows, :]
        if rope is not None:
            cos = rope[0][rows, :]
            sin = rope[1][rows, :]

            def rot(x, cos=cos, sin=sin):
                partner = jnp.where(first_half, pltpu.roll(x, LANE - GLA_DK // 2, axis=1),
                                    pltpu.roll(x, GLA_DK // 2, axis=1))
                return x * cos + partner * sin
            q, k = rot(q), rot(k)
        per_chunk.append(dict(c=c, rows=rows, q=q, k=k, k16=k.astype(BF16), q_heads=stack_heads(q),
                              v16=v_ref[0, rows, :].astype(BF16), gs=_hi_lo(gs_ref[0, rows, :])))
    items = [(w, d) for w in per_chunk for d in range(2)]
    wg = [_hi_lo(wg_ref[d, 0]) for d in range(2)]
    xs = [jnp.dot(w["gs"][0], wg[d][0], preferred_element_type=F32) for w, d in items]
    xs = [x + jnp.dot(w["gs"][0], wg[d][1], preferred_element_type=F32) for x, (w, d) in zip(xs, items)]
    xs = [x + jnp.dot(w["gs"][1], wg[d][0], preferred_element_type=F32) + bgate_ref[d, 0] for x, (w, d) in zip(xs, items)]
    gs = [_hi_lo((jnp.minimum(x, 0.0) - jnp.log1p(jnp.exp(-jnp.abs(x)))) * (1.0 / GLA_TAU)) for x in xs]
    main = [jnp.dot(sums_ref[d, 0:2 * CHUNK], g[0], preferred_element_type=F32) for g, (w, d) in zip(gs, items)]
    main = [m + jnp.dot(sums_ref[d, 0:2 * CHUNK], g[1], preferred_element_type=F32)
            for m, g, (w, d) in zip(main, gs, items)]
    levs = [jnp.dot(sums_ref[d, 2 * CHUNK:], g[0], preferred_element_type=F32) for g, (w, d) in zip(gs, items)]
    for m, (w, d) in zip(main, items):
        run = m[0:CHUNK]
        total = run[0:1] if d else run[CHUNK - 1:CHUNK]
        qr_s[d, w["rows"], :] = (w["q"] * jnp.exp(run)).astype(BF16)
        ke_s[d, w["rows"], :] = (w["k"] * jnp.exp(m[CHUNK:])).astype(BF16)
        tot_s[d, pl.ds(w["c"], 1), :] = jnp.exp(total)
    atts = [_nt_dot(w["q_heads"], w["k16"]) * masks_ref[d, GLA_LEVELS] for w, d in items]
    for lv in range(GLA_LEVELS):
        parts = [jnp.exp(lev[lv * CHUNK:(lv + 1) * CHUNK]) for lev in levs]
        atts = [att + _nt_dot(stack_heads(w["q"] * part), (w["k"] * part).astype(BF16)) * masks_ref[d, lv]
                for att, part, (w, d) in zip(atts, parts, items)]
    for n, w in enumerate(per_chunk):
        av = jnp.dot(jnp.concatenate(atts[2 * n:2 * n + 2], axis=0).astype(BF16), w["v16"],
                     preferred_element_type=F32)
        acc_ref[w["rows"], :] = sum(
            jnp.concatenate([av[(GLA_PAIR * d + hh) * CHUNK:(GLA_PAIR * d + hh + 1) * CHUNK,
                                hh * GLA_DV:(hh + 1) * GLA_DV] for hh in range(GLA_PAIR)], axis=1) for d in range(2))


def _gla_recur(chunks, v_ref, qr_s, ke_s, tot_s, st_refs, acc_ref):
    rows = [pl.ds(pl.multiple_of(c * CHUNK, CHUNK), CHUNK) for c in chunks]
    sts = [st_refs[d][...] for d in range(2)]
    inters = [_nt_dot(qr_s[d, rows[d], :], sts[d].astype(BF16)) for d in range(2)]
    upds = [_tn_dot(v_ref[0, rows[d], :].astype(BF16), ke_s[d, rows[d], :]) for d in range(2)]
    srow = lax.broadcasted_iota(jnp.int32, upds[0].shape, 0) // GLA_DV
    scol = lax.broadcasted_iota(jnp.int32, upds[0].shape, 1) // GLA_DK
    for d in range(2):
        st_refs[d][...] = jnp.where(srow == scol, sts[d] * tot_s[d, pl.ds(chunks[d], 1), :] + upds[d], 0.0)
    for d in range(2):
        acc_ref[rows[d], :] += inters[d]


def _gla_finish(acc_ref, z_ref, norm_ref, o_ref, n_rows):
    tile = min(256, n_rows)
    for i in range(n_rows // tile):
        rows = slice(i * tile, (i + 1) * tile)
        o = acc_ref[rows, :]
        z = z_ref[0, rows, :]
        y = jnp.concatenate([_rms_rows(o[:, hh * GLA_DV:(hh + 1) * GLA_DV], norm_ref[...])
                             for hh in range(GLA_PAIR)], axis=1)
        o_ref[0, rows, :] = (y * (z * jax.nn.sigmoid(z))).astype(o_ref.dtype)


def _gla_kernel(q_ref, k_ref, v_ref, z_ref, gs_ref, qc_ref, kc_ref, vc_ref, zc_ref, gsc_ref, cos_ref, sin_ref,
                wg_ref, bgate_ref, norm_ref, sums_ref, masks_ref, o_ref, oc_ref, stf_ref, stb_ref, qr_s, ke_s, tot_s,
                acc_ref):
    st_refs = (stf_ref, stb_ref)
    for r in st_refs:
        r[...] = jnp.zeros_like(r)
    for (qq, kk, vv, gg), rope, zz_ref, out_ref in (((qc_ref, kc_ref, vc_ref, gsc_ref), None, zc_ref, oc_ref),
                                                    ((q_ref, k_ref, v_ref, gs_ref), (cos_ref, sin_ref), z_ref, o_ref)):
        n = qq.shape[1]
        n_chunks = n // CHUNK

        def local_step(i, carry, qq=qq, kk=kk, vv=vv, gg=gg, rope=rope):
            _gla_local((2 * i, 2 * i + 1), qq, kk, vv, gg, rope, wg_ref, bgate_ref, sums_ref, masks_ref, qr_s, ke_s,
                       tot_s, acc_ref)
            return carry

        def recur_step(i, carry, vv=vv, n_chunks=n_chunks):
            _gla_recur((i, n_chunks - 1 - i), vv, qr_s, ke_s, tot_s, st_refs, acc_ref)
            return carry

        lax.fori_loop(0, n_chunks // 2, local_step, 0)
        lax.fori_loop(0, n_chunks, recur_step, 0)
        _gla_finish(acc_ref, zz_ref, norm_ref, out_ref, n)


def gla_branch(p_main, p_small, pc_main, pc_small, wg2, bg, norm):
    b, t, _ = p_main.shape
    tc = pc_main.shape[1]
    nj = GLA_HEADS // GLA_PAIR
    vw = GLA_PAIR * GLA_DV
    o_q = MY_OFF["gla_q"][0] // LANE
    o_k = MY_OFF["gla_k"][0] // LANE
    o_v = MY_OFF["gla_v"][0] // vw
    o_z = MY_OFF["gla_z"][0] // vw
    sums, masks = _gla_tables()
    cos, sin = _gla_rope_tables(t)
    wge, bge = _gla_gate_weights(wg2.astype(F32), bg.astype(F32))

    def seq(n, w, o):
        return pl.BlockSpec((1, n, w), lambda i, j: (i, 0, o + j))

    def whole(n):
        return pl.BlockSpec((1, n, N_SMALL), lambda i, j: (i, 0, 0))

    def const(shape):
        return pl.BlockSpec(shape, lambda i, j: (0,) * len(shape))

    return pl.pallas_call(
        _gla_kernel,
        grid=(b, nj),
        in_specs=[seq(t, LANE, o_q), seq(t, LANE, o_k), seq(t, vw, o_v), seq(t, vw, o_z), whole(t),
                  seq(tc, LANE, o_q), seq(tc, LANE, o_k), seq(tc, vw, o_v), seq(tc, vw, o_z), whole(tc),
                  const((t, LANE)), const((t, LANE)),
                  pl.BlockSpec((2, 1, N_SMALL, LANE), lambda i, j: (0, j, 0, 0)),
                  pl.BlockSpec((2, 1, 1, LANE), lambda i, j: (0, j, 0, 0)),
                  const((1, GLA_DV)), const((2, GLA_SUM_ROWS, CHUNK)),
                  const((2, GLA_LEVELS + 1, GLA_PAIR * CHUNK, CHUNK))],
        out_specs=[pl.BlockSpec((1, t, vw), lambda i, j: (i, 0, j)), pl.BlockSpec((1, tc, vw), lambda i, j: (i, 0, j))],
        out_shape=[jax.ShapeDtypeStruct((b, t, BR_W), BF16), jax.ShapeDtypeStruct((b, tc, BR_W), BF16)],
        scratch_shapes=[pltpu.VMEM((vw, LANE), F32), pltpu.VMEM((vw, LANE), F32),
                        pltpu.VMEM((2, t, LANE), BF16), pltpu.VMEM((2, t, LANE), BF16),
                        pltpu.VMEM((2, t // CHUNK, LANE), F32),
                        pltpu.VMEM((t, vw), F32)],
        compiler_params=pltpu.CompilerParams(vmem_limit_bytes=VMEM_LIMIT),
        name="gla_scan",
    )(p_main, p_main, p_main, p_main, p_small, pc_main, pc_main, pc_main, pc_main, pc_small,
      jnp.asarray(cos), jnp.asarray(sin), wge, bge, norm.astype(F32).reshape(1, GLA_DV),
      jnp.asarray(sums, BF16), jnp.asarray(masks))


GDN_PAIR = 2 * CHUNK
GDN_STACK = 2 * GDN_PAIR


def _neumann(mats):
    rs = [-a for a in mats]
    ps = list(mats)
    for _ in range(int(math.log2(CHUNK)) - 1):
        p16 = [p.astype(BF16) for p in ps]
        ps = [jnp.dot(x, x, preferred_element_type=F32) for x in p16]
        rs = [r + p + jnp.dot(r.astype(BF16), p.astype(BF16), preferred_element_type=F32) for r, p in zip(rs, ps)]
    return rs


def _seg_scan(x, reverse):
    n = x.shape[0]
    pos = lax.broadcasted_iota(jnp.int32, x.shape, 0) % CHUNK
    s = 1
    while s < CHUNK:
        if reverse:
            x = x + jnp.where(pos < CHUNK - s, pltpu.roll(x, n - s, axis=0), 0.0)
        else:
            x = x + jnp.where(pos >= s, pltpu.roll(x, s, axis=0), 0.0)
        s *= 2
    return x


def _gdn_prepare(q_ref, k_ref, v_ref, gs_ref, wq_ref, wk_ref, wv_ref, sel_ref, coef_ref, qkv_s, gb_s):
    n = q_ref.shape[1]
    for i, (x_ref, w_ref) in enumerate(((q_ref, wq_ref), (k_ref, wk_ref), (v_ref, wv_ref))):
        y = _dwconv_rows(x_ref[0], w_ref, GDN_CONV)
        y = y * jax.nn.sigmoid(y)
        if i < 2:
            y = y * lax.rsqrt(jnp.sum(y * y, axis=-1, keepdims=True) + EPS)
        if i == 0:
            y = y * (GDN_DH ** -0.5)
        qkv_s[i, 0:n, :] = y
    gs_hi, gs_lo = _hi_lo(gs_ref[0])

    def pick(i):
        return (jnp.dot(gs_hi, sel_ref[0, i], preferred_element_type=F32)
                + jnp.dot(gs_lo, sel_ref[0, i], preferred_element_type=F32))

    for d in range(2):
        a = pick(d)
        bb = pick(2 + d)
        g = -jnp.exp(coef_ref[0, d:d + 1, :]) * _softplus(a + coef_ref[0, 2 + d:3 + d, :])
        run = _seg_scan(g, reverse=(d == 1))
        gb_s[d, 0:n, :] = run
        gb_s[2 + d, 0:n, :] = jax.nn.sigmoid(bb)
        last = 0 if d else CHUNK - 1
        total = run.reshape(n // CHUNK, CHUNK, LANE)[:, last:last + 1, :]
        total = jnp.broadcast_to(total, (n // CHUNK, CHUNK, LANE)).reshape(n, LANE)
        gb_s[4 + d, 0:n, :] = jnp.exp(total - run)


def _gdn_local(pairs, qkv_s, gb_s, negm_ref, x_s, b_s, qp_s, op_s):
    eye = None
    work = []
    for p in pairs:
        rows = pl.ds(pl.multiple_of(p * GDN_PAIR, GDN_PAIR), GDN_PAIR)
        q = qkv_s[0, rows, :]
        k = qkv_s[1, rows, :]
        v = qkv_s[2, rows, :]
        gam = [gb_s[d, rows, :] for d in range(2)]
        beta = [gb_s[2 + d, rows, :] for d in range(2)]
        egam = [jnp.exp(g) for g in gam]
        kb = [k * bt for bt in beta]
        k16 = k.astype(BF16)
        raw = _nt_dot(jnp.concatenate([kb[0], kb[1], q, q], axis=0).astype(BF16),
                      jnp.concatenate([k16, k16], axis=0))
        g2 = jnp.concatenate(gam, axis=0)
        g2t = g2.T
        diff = jnp.concatenate([g2, g2], axis=1) - jnp.concatenate([g2t, g2t], axis=0)
        dec = jnp.exp(jnp.minimum(diff, 0.0) + negm_ref[...])
        if eye is None:
            eye = (lax.broadcasted_iota(jnp.int32, dec.shape, 0)
                   == lax.broadcasted_iota(jnp.int32, dec.shape, 1)).astype(F32)
        rhs = jnp.concatenate([jnp.concatenate([v * beta[d], kb[d] * egam[d]], axis=1) for d in range(2)], axis=0)
        work.append(dict(p=p, rows=rows, q=q, k=k, egam=egam, rhs=rhs,
                         a_kk=raw[0:GDN_STACK] * dec, a_qk=raw[GDN_STACK:] * (dec + eye)))
    for w, r in zip(work, _neumann([w["a_kk"] for w in work])):
        w["sol"] = w["rhs"] + jnp.dot(r.astype(BF16), w["rhs"].astype(BF16), preferred_element_type=F32)
    for w in work:
        p, rows, q, k, egam = w["p"], w["rows"], w["q"], w["k"], w["egam"]
        sol16 = w["sol"].astype(BF16)
        ao = jnp.dot(w["a_qk"].astype(BF16), sol16, preferred_element_type=F32)
        for d in range(2):
            blk = slice(d * GDN_PAIR, (d + 1) * GDN_PAIR)
            op_s[d, rows, :] = ao[blk, 0:GDN_DH]
            qp_s[d, rows, :] = (q * egam[d] - ao[blk, GDN_DH:]).astype(BF16)
            kd = (k * gb_s[4 + d, rows, :]).astype(BF16)
            for jj in range(2):
                sub = slice(jj * CHUNK, (jj + 1) * CHUNK)
                bx = _tn_dot(kd[sub], sol16[d * GDN_PAIR + jj * CHUNK:d * GDN_PAIR + (jj + 1) * CHUNK])
                b_s[d, 2 * p + jj] = bx[:, 0:GDN_DH]
                x_s[d, 2 * p + jj] = bx[:, GDN_DH:].astype(BF16)


def _gdn_recur(chunks, gb_s, x_s, b_s, qp_s, op_s, st_refs, acc_ref):
    rows = [pl.ds(pl.multiple_of(c * CHUNK, CHUNK), CHUNK) for c in chunks]
    sts = [st_refs[d][...] for d in range(2)]
    xs = [jnp.dot(jnp.concatenate([x_s[d, chunks[d]], qp_s[d, rows[d], :]], axis=0), sts[d].astype(BF16),
                  preferred_element_type=F32) for d in range(2)]
    for d, c in enumerate(chunks):
        last = c * CHUNK + (0 if d else CHUNK - 1)
        st_refs[d][...] = sts[d] * jnp.exp(gb_s[d, pl.ds(last, 1), :]) + (b_s[d, c] - xs[d][0:GDN_DH])
    for d in range(2):
        acc_ref[rows[d], :] += xs[d][GDN_DH:] + op_s[d, rows[d], :]


def _gdn_finish(acc_ref, z_ref, norm_ref, o_ref, n_rows):
    tile = min(256, n_rows)
    for i in range(n_rows // tile):
        rows = slice(i * tile, (i + 1) * tile)
        z = z_ref[0, rows, :]
        o_ref[0, rows, :] = (_rms_rows(acc_ref[rows, :], norm_ref[...]) * (z * jax.nn.sigmoid(z))).astype(o_ref.dtype)


def _gdn_kernel(q_ref, k_ref, v_ref, z_ref, gs_ref, qc_ref, kc_ref, vc_ref, zc_ref, gsc_ref, wq_ref, wk_ref, wv_ref,
                sel_ref, coef_ref, norm_ref, negm_ref, o_ref, oc_ref, stf_ref, stb_ref, qkv_s, gb_s,
                x_s, b_s, qp_s, op_s, acc_ref):
    st_refs = (stf_ref, stb_ref)
    for r in st_refs:
        r[...] = jnp.zeros_like(r)
    for refs, zz_ref, out_ref in (((qc_ref, kc_ref, vc_ref, gsc_ref), zc_ref, oc_ref),
                                  ((q_ref, k_ref, v_ref, gs_ref), z_ref, o_ref)):
        n = refs[0].shape[1]
        n_chunks = n // CHUNK
        _gdn_prepare(*refs, wq_ref, wk_ref, wv_ref, sel_ref, coef_ref, qkv_s, gb_s)
        acc_ref[0:n, :] = jnp.zeros((n, LANE), F32)

        def local_step(i, carry):
            _gdn_local((2 * i, 2 * i + 1), qkv_s, gb_s, negm_ref, x_s, b_s, qp_s, op_s)
            return carry

        def recur_step(i, carry, n_chunks=n_chunks):
            _gdn_recur((i, n_chunks - 1 - i), gb_s, x_s, b_s, qp_s, op_s, st_refs, acc_ref)
            return carry

        lax.fori_loop(0, n // (2 * GDN_PAIR), local_step, 0)
        lax.fori_loop(0, n_chunks, recur_step, 0)
        _gdn_finish(acc_ref, zz_ref, norm_ref, out_ref, n)


def _gdn_order_mask():
    ti = np.arange(GDN_STACK)[:, None]
    si = np.arange(GDN_STACK)[None, :]
    same = (ti // CHUNK) == (si // CHUNK)
    earlier = same & np.where(ti < GDN_PAIR, si < ti, si > ti)
    return np.where(earlier, 0.0, NA_NEG).astype(np.float32)


def gdn_branch(p_main, p_small, pc_main, pc_small, conv_w, a_log, dt_bias, norm):
    b, t, _ = p_main.shape
    tc = pc_main.shape[1]
    assert t % (2 * GDN_PAIR) == 0 and tc % (2 * GDN_PAIR) == 0 and tc <= t
    nh = GDN_HEADS
    o_q = MY_OFF["gdn_qkv"][0] // LANE
    o_z = MY_OFF["gdn_z"][0] // LANE
    wt = conv_w.T.astype(F32)
    sel = np.zeros((nh, 4, N_SMALL, LANE), np.float32)
    for h in range(nh):
        for d in range(2):
            sel[h, d, MY_OFF["gdn_a"][0] + d * nh + h, :] = 1.0
            sel[h, 2 + d, MY_OFF["gdn_b"][0] + d * nh + h, :] = 1.0
    coef = jnp.concatenate([a_log.astype(F32).T, dt_bias.astype(F32).T], axis=1)
    coef = jnp.broadcast_to(coef[:, :, None], (nh, 4, LANE))

    def seq(n, o):
        return pl.BlockSpec((1, n, LANE), lambda i, j: (i, 0, o + j))

    def whole(n):
        return pl.BlockSpec((1, n, N_SMALL), lambda i, j: (i, 0, 0))

    def wspec(o):
        return pl.BlockSpec((GDN_CONV, LANE), lambda i, j: (0, o + j))

    return pl.pallas_call(
        _gdn_kernel,
        grid=(b, nh),
        in_specs=[seq(t, o_q), seq(t, o_q + nh), seq(t, o_q + 2 * nh), seq(t, o_z), whole(t),
                  seq(tc, o_q), seq(tc, o_q + nh), seq(tc, o_q + 2 * nh), seq(tc, o_z), whole(tc),
                  wspec(0), wspec(nh), wspec(2 * nh),
                  pl.BlockSpec((1, 4, N_SMALL, LANE), lambda i, j: (j, 0, 0, 0)),
                  pl.BlockSpec((1, 4, LANE), lambda i, j: (j, 0, 0)),
                  pl.BlockSpec((1, GDN_DH), lambda i, j: (0, 0)),
                  pl.BlockSpec((GDN_STACK, GDN_STACK), lambda i, j: (0, 0))],
        out_specs=[pl.BlockSpec((1, t, LANE), lambda i, j: (i, 0, j)), pl.BlockSpec((1, tc, LANE), lambda i, j: (i, 0, j))],
        out_shape=[jax.ShapeDtypeStruct((b, t, BR_W), BF16), jax.ShapeDtypeStruct((b, tc, BR_W), BF16)],
        scratch_shapes=[pltpu.VMEM((GDN_DH, GDN_DH), F32), pltpu.VMEM((GDN_DH, GDN_DH), F32),
                        pltpu.VMEM((3, t, LANE), F32), pltpu.VMEM((6, t, LANE), F32),
                        pltpu.VMEM((2, t // CHUNK, GDN_DH, GDN_DH), BF16),
                        pltpu.VMEM((2, t // CHUNK, GDN_DH, GDN_DH), F32),
                        pltpu.VMEM((2, t, LANE), BF16), pltpu.VMEM((2, t, LANE), F32),
                        pltpu.VMEM((t, LANE), F32)],
        compiler_params=pltpu.CompilerParams(vmem_limit_bytes=VMEM_LIMIT),
        name="gdn_scan",
    )(p_main, p_main, p_main, p_main, p_small, pc_main, pc_main, pc_main, pc_main, pc_small,
      wt, wt, wt, jnp.asarray(sel, BF16), coef, norm.astype(F32).reshape(1, GDN_DH), jnp.asarray(_gdn_order_mask()))


HY_COLS = 256
HY_ROWS = 512


def dft_matrices(length):
    n = 2 * length
    inner = 64
    assert length % inner == 0
    r = jnp.arange(n, dtype=jnp.int32)
    k = jnp.where(r <= length, r, r - length)[:, None]
    a1 = ((k * (inner * jnp.arange(length // inner, dtype=jnp.int32))[None, :]) % n).astype(F32) * (2.0 * math.pi / n)
    a0 = ((k * jnp.arange(inner, dtype=jnp.int32)[None, :]) % n).astype(F32) * (2.0 * math.pi / n)
    c1, s1 = jnp.cos(a1)[:, :, None], jnp.sin(a1)[:, :, None]
    c0, s0 = jnp.cos(a0)[:, None, :], jnp.sin(a0)[:, None, :]
    is_cos = (r <= length)[:, None, None]
    f = jnp.where(is_cos, c1 * c0 - s1 * s0, -(s1 * c0 + c1 * s0)).reshape(n, length)
    w = (jnp.where((r == 0) | (r == length), 1.0, 2.0) / n)[:, None]
    return f.astype(BF16), (f * w).T.astype(BF16)


def _hy_embedding(length):
    t = np.linspace(0.0, 1.0, length)[:, None]
    bands = (HY_EMB - 1) // 2
    wpos = 2.0 * math.pi * np.arange(length)[:, None] / length
    f = np.linspace(1e-4, bands - 1, bands)[None]
    z = np.concatenate([t, np.cos(f * wpos), -np.sin(f * wpos)], axis=-1)
    z = np.pad(z, ((0, 0), (0, HY_FFN - HY_EMB)))
    deltas = np.abs(np.linspace(math.log(HY_TARGET) / HY_SLOW, math.log(HY_TARGET) / HY_FAST, BR_W))[None]
    return z.astype(np.float32), deltas.astype(np.float32)


def _hy_filter_kernel(z_ref, win_ref, bin_ref, wmid_ref, bmid_ref, freq_ref, wout_ref, delta_ref, o_ref):
    hp = lax.Precision.HIGHEST
    length = z_ref.shape[0]
    h = jnp.sin(freq_ref[0] * (jnp.dot(z_ref[...], win_ref[...], precision=hp, preferred_element_type=F32)
                               + bin_ref[...]))
    for i in range(HY_INNER):
        h = jnp.sin(freq_ref[i + 1] * (jnp.dot(h, wmid_ref[i], precision=hp, preferred_element_type=F32)
                                       + bmid_ref[i]))
    h = jnp.dot(h, wout_ref[...], precision=hp, preferred_element_type=F32)
    t = lax.broadcasted_iota(jnp.int32, (length, BR_W), 0).astype(F32) * (1.0 / (length - 1))
    decay = jnp.exp(-t * delta_ref[...])
    h_f = h[:, :BR_W] * decay
    h_b = h[:, BR_W:] * decay
    for i, a in enumerate((h_f + h_b, h_f - h_b)):
        hi = a.astype(BF16)
        lo = (a - hi.astype(F32)).astype(BF16)
        o_ref[:, (2 * i) * BR_W:(2 * i + 1) * BR_W] = hi
        o_ref[:, (2 * i + 1) * BR_W:(2 * i + 2) * BR_W] = lo


def _hy_spectrum_kernel(f_ref, a_ref, ks_ref, kd_ref, *, length):
    i = pl.program_id(0)
    tm = f_ref.shape[0]
    r = jnp.dot(f_ref[...], a_ref[...], preferred_element_type=F32)
    ks_ref[...] = r[:, :BR_W] + r[:, BR_W:2 * BR_W]
    kd = r[:, 2 * BR_W:3 * BR_W] + r[:, 3 * BR_W:]
    row = lax.broadcasted_iota(jnp.int32, kd.shape, 0) + i * tm
    kd_ref[...] = jnp.where(row == length, 0.0, kd)


def hyena_filter_spectrum(length, fwd, filt):
    w_in, b_in, w_mid, b_mid, freq, w_out = filt
    z, deltas = _hy_embedding(length)
    n = 2 * length
    parts = pl.pallas_call(
        _hy_filter_kernel,
        out_shape=jax.ShapeDtypeStruct((length, 4 * BR_W), BF16),
        compiler_params=pltpu.CompilerParams(vmem_limit_bytes=VMEM_LIMIT),
        name="hyena_filter",
    )(jnp.asarray(z), jnp.pad(w_in, ((0, HY_FFN - HY_EMB), (0, 0))), b_in.reshape(1, HY_FFN), w_mid,
      b_mid.reshape(HY_INNER, 1, HY_FFN), freq.reshape(HY_INNER + 1, 1, HY_FFN), w_out, jnp.asarray(deltas))
    tm = min(HY_ROWS, n)
    return pl.pallas_call(
        functools.partial(_hy_spectrum_kernel, length=length),
        grid=(n // tm,),
        in_specs=[pl.BlockSpec((tm, length), lambda i: (i, 0)),
                  pl.BlockSpec((length, 4 * BR_W), lambda i: (0, 0))],
        out_specs=[pl.BlockSpec((tm, BR_W), lambda i: (i, 0)), pl.BlockSpec((tm, BR_W), lambda i: (i, 0))],
        out_shape=[jax.ShapeDtypeStruct((n, BR_W), F32), jax.ShapeDtypeStruct((n, BR_W), F32)],
        compiler_params=pltpu.CompilerParams(vmem_limit_bytes=VMEM_LIMIT),
        name="hyena_spectrum",
    )(fwd, parts)


def _shift_rows(x, off):
    t = x.shape[0]
    edge = 8
    assert 0 < abs(off) < edge and t % edge == 0
    rolled = pltpu.roll(x, (-off) % t, axis=0)
    row = lax.broadcasted_iota(jnp.int32, (edge, x.shape[1]), 0)
    if off < 0:
        return jnp.concatenate([jnp.where(row >= -off, rolled[0:edge], 0.0), rolled[edge:]], axis=0)
    return jnp.concatenate([rolled[0:t - edge], jnp.where(row < edge - off, rolled[t - edge:], 0.0)], axis=0)


def _dwconv_rows(x, w_ref, k):
    out = x * w_ref[k // 2:k // 2 + 1, :]
    for j in range(k):
        if j != k // 2:
            out = out + _shift_rows(x, j - k // 2) * w_ref[j:j + 1, :]
    return out


def _hy_prep_kernel(x0_ref, x1_ref, v_ref, z_ref, w0_ref, w1_ref, wv_ref, b0_ref, b1_ref, bv_ref, vx_ref, x0g_ref):
    x0 = _dwconv_rows(x0_ref[0], w0_ref, HY_CONV) + b0_ref[...]
    x1 = _dwconv_rows(x1_ref[0], w1_ref, HY_CONV) + b1_ref[...]
    v = _dwconv_rows(v_ref[0], wv_ref, HY_CONV) + bv_ref[...]
    z = z_ref[0]
    vx_ref[0] = v * x1
    x0g_ref[0] = x0 * (z * jax.nn.sigmoid(z))


def hyena_prep(p_main, conv_w, conv_b):
    b, length, _ = p_main.shape
    cb = LANE
    nj = BR_W // cb
    o_xv = MY_OFF["hy_xv"][0] // cb
    o_z = MY_OFF["hy_z"][0] // cb
    wt = conv_w.T
    bt = conv_b.reshape(1, 3 * BR_W)
    dspec = lambda o: pl.BlockSpec((1, length, cb), lambda i, j: (i, 0, o + j))
    wspec = lambda o: pl.BlockSpec((HY_CONV, cb), lambda i, j: (0, o + j))
    bspec = lambda o: pl.BlockSpec((1, cb), lambda i, j: (0, o + j))
    ospec = pl.BlockSpec((1, length, cb), lambda i, j: (i, 0, j))
    return pl.pallas_call(
        _hy_prep_kernel,
        grid=(b, nj),
        in_specs=[dspec(o_xv), dspec(o_xv + nj), dspec(o_xv + 2 * nj), dspec(o_z),
                  wspec(0), wspec(nj), wspec(2 * nj), bspec(0), bspec(nj), bspec(2 * nj)],
        out_specs=[ospec, ospec],
        out_shape=[jax.ShapeDtypeStruct((b, length, BR_W), F32)] * 2,
        compiler_params=pltpu.CompilerParams(vmem_limit_bytes=VMEM_LIMIT),
        name="hyena_prep",
    )(p_main, p_main, p_main, p_main, wt, wt, wt, bt, bt, bt)


def _hy_fwd_kernel(vx_ref, f_ref, kr_ref, ki_ref, knyq_ref, y_ref, *, length):
    v = vx_ref[0].astype(BF16)
    tm = min(HY_ROWS, length)
    for kt in range(length // tm):
        lo = kt * tm
        a = jnp.dot(f_ref[lo:lo + tm, :], v, preferred_element_type=F32)
        bh = jnp.dot(f_ref[length + lo:length + lo + tm, :], v, preferred_element_type=F32)
        kr = kr_ref[lo:lo + tm, :]
        ki = ki_ref[lo:lo + tm, :]
        if kt == 0:
            row = lax.broadcasted_iota(jnp.int32, kr.shape, 0)
            kb = jnp.where(row == 0, knyq_ref[0:1, :], kr)
        else:
            kb = kr
        y_ref[0, lo:lo + tm, :] = (a * kr - bh * ki).astype(BF16)
        y_ref[0, length + lo:length + lo + tm, :] = (a * ki + bh * kb).astype(BF16)


def _hy_inv_kernel(y_ref, inv_ref, vx_ref, x0g_ref, skip_ref, o_ref, *, length):
    tm = min(HY_ROWS, length)
    y = y_ref[0]
    for tt in range(length // tm):
        lo = tt * tm
        conv = jnp.dot(inv_ref[lo:lo + tm, :], y, preferred_element_type=F32)
        o_ref[0, lo:lo + tm, :] = ((conv + skip_ref[...] * vx_ref[0, lo:lo + tm, :])
                                   * x0g_ref[0, lo:lo + tm, :]).astype(o_ref.dtype)


def hyena_branch(p_main, conv_w, conv_b, skip, fwd, inv, ks, kd):
    b, length, _ = p_main.shape
    n = 2 * length
    vx, x0g = hyena_prep(p_main, conv_w, conv_b)
    cb = HY_COLS
    nj = BR_W // cb
    resident = lambda shape: pl.BlockSpec(shape, lambda j, i: (0, 0), pipeline_mode=pl.Buffered(1))
    y = pl.pallas_call(
        functools.partial(_hy_fwd_kernel, length=length),
        grid=(nj, b),
        in_specs=[pl.BlockSpec((1, length, cb), lambda j, i: (i, 0, j)),
                  resident((n, length)),
                  pl.BlockSpec((length, cb), lambda j, i: (0, j)),
                  pl.BlockSpec((length, cb), lambda j, i: (1, j)),
                  pl.BlockSpec((8, cb), lambda j, i: (length // 8, j))],
        out_specs=pl.BlockSpec((1, n, cb), lambda j, i: (i, 0, j)),
        out_shape=jax.ShapeDtypeStruct((b, n, BR_W), BF16),
        compiler_params=pltpu.CompilerParams(vmem_limit_bytes=VMEM_LIMIT),
        name="hyena_dft",
    )(vx, fwd, ks, kd, ks)
    return pl.pallas_call(
        functools.partial(_hy_inv_kernel, length=length),
        grid=(nj, b),
        in_specs=[pl.BlockSpec((1, n, cb), lambda j, i: (i, 0, j)),
                  resident((length, n)),
                  pl.BlockSpec((1, length, cb), lambda j, i: (i, 0, j)),
                  pl.BlockSpec((1, length, cb), lambda j, i: (i, 0, j)),
                  pl.BlockSpec((1, cb), lambda j, i: (0, j))],
        out_specs=pl.BlockSpec((1, length, cb), lambda j, i: (i, 0, j)),
        out_shape=jax.ShapeDtypeStruct((b, length, BR_W), BF16),
        compiler_params=pltpu.CompilerParams(vmem_limit_bytes=VMEM_LIMIT),
        name="hyena_idft",
    )(y, inv, vx, x0g, skip.reshape(1, BR_W))


def mixers(p_main, p_small, pc_main, pc_small, na_rpb, gla_wg2, gla_bg, gla_norm, gdn_conv, gdn_a_log, gdn_dt_bias,
           gdn_norm, hy_conv, hy_conv_b, hy_filt, hy_skip, with_ctx, dft_lat, dft_ctx):
    t = p_main.shape[1]
    ya = na_branch(p_main, pc_main, na_rpb)
    yb, yb_c = gla_branch(p_main, p_small, pc_main, pc_small, gla_wg2, gla_bg, gla_norm)
    yc, yc_c = gdn_branch(p_main, p_small, pc_main, pc_small, gdn_conv, gdn_a_log, gdn_dt_bias, gdn_norm)
    fwd, inv = dft_lat
    ks, kd = hyena_filter_spectrum(t, fwd, hy_filt)
    yd = hyena_branch(p_main, hy_conv, hy_conv_b, hy_skip, fwd, inv, ks, kd)
    if not with_ctx:
        return (ya, yb, yc, yd), None
    ya_c = ctx_attention_branch(pc_main)
    fwd_c, inv_c = dft_ctx
    ks_c, kd_c = hyena_filter_spectrum(pc_main.shape[1], fwd_c, hy_filt)
    yd_c = hyena_branch(pc_main, hy_conv, hy_conv_b, hy_skip, fwd_c, inv_c, ks_c, kd_c)
    return (ya, yb, yc, yd), (ya_c, yb_c, yc_c, yd_c)


def kernel(x, c, ctx, c_ctx, w_mod, b_mod, g_pre, g_post, w_in, w_gate, b_gate, w_branch, w_out, na_rpb, gla_wg2, gla_bg, gla_norm, gdn_conv, gdn_a_log, gdn_dt_bias, gdn_norm, hy_conv, hy_conv_b, hy_w_in, hy_b_in, hy_w_mid, hy_b_mid, hy_freq, hy_w_out, hy_skip):
    b, t, d = x.shape
    tc = ctx.shape[1]
    h = x.reshape(b * t, d)
    hc = ctx.reshape(b * tc, d)
    c_all = jnp.concatenate([c, c_ctx[None], jnp.zeros((7, d), F32)], axis=0)
    dft_lat = dft_matrices(t)
    dft_ctx = dft_matrices(tc)
    for l in range(DEPTH):
        with_ctx = l < DEPTH - 1
        mod = modulation(c_all, w_mod[l], b_mod[l])
        shift, scale, gate = [mod[:, i * d:(i + 1) * d].reshape(-1, 1, d) for i in range(3)]
        u = prenorm_mod(h, g_pre[l], scale[:b], shift[:b], t)
        uc = prenorm_mod(hc, g_pre[l], scale[b:b + 1], shift[b:b + 1], b * tc)
        w_main, w_small = _permute_w_in(w_in[l])
        p_main = matmul(u, w_main, 1024, 1536, name="in_proj").reshape(b, t, N_MAIN)
        p_small = matmul(u, w_small, 1024, N_SMALL, name="in_proj_small").reshape(b, t, N_SMALL)
        pc_main = matmul(uc, w_main, 1024, 1536, name="in_proj_ctx").reshape(b, tc, N_MAIN)
        pc_small = matmul(uc, w_small, 1024, N_SMALL, name="in_proj_ctx_small").reshape(b, tc, N_SMALL)
        ys, ycs = mixers(p_main, p_small, pc_main, pc_small,
                         na_rpb[l], gla_wg2[l], gla_bg[l], gla_norm[l], gdn_conv[l], gdn_a_log[l], gdn_dt_bias[l],
                         gdn_norm[l], hy_conv[l], hy_conv_b[l],
                         (hy_w_in[l], hy_b_in[l], hy_w_mid[l], hy_b_mid[l], hy_freq[l], hy_w_out[l]),
                         hy_skip[l], with_ctx, dft_lat, dft_ctx)
        h_new = merge(u, ys, w_gate[l], b_gate[l], w_branch[l], w_out[l], h, gate[:b], g_post[l], t)
        if with_ctx:
            hc = merge(uc, ycs, w_gate[l], b_gate[l], w_branch[l], w_out[l], hc, gate[b:b + 1], g_post[l], b * tc)
        h = h_new
    return h.reshape(b, t, d)
```

```python
import functools
import math

import numpy as np
import jax
import jax.numpy as jnp
from jax import lax
from jax.experimental import pallas as pl
from jax.experimental.pallas import tpu as pltpu

D_MODEL = 2048
BATCH = 16
SEQ = 2048
DEPTH = 2
GRID_W = 64
CTX_LEN = 256
N_BRANCH = 4
BR_W = D_MODEL // N_BRANCH
NA_HEADS = 8
NA_DH = BR_W // NA_HEADS
WIN_R = 8
WIN_C = 16
QBLK_C = 16
KBLK_C = WIN_C + QBLK_C
N_CBLK = GRID_W // QBLK_C
GLA_HEADS = 4
GLA_DV = BR_W // GLA_HEADS
GLA_DK = GLA_DV // 2
GLA_RANK = 16
GLA_TAU = 16.0
CHUNK = 64
GDN_HEADS = 4
GDN_DH = BR_W // GDN_HEADS
GDN_CONV = 5
HY_CONV = 3
HY_EMB = 33
HY_FFN = 64
HY_INNER = 2
HY_TARGET = 1e-2
HY_FAST = 0.3
HY_SLOW = 1.5
ROPE_THETA = 10000.0
EPS = 1e-6
F32 = jnp.float32
BF16 = jnp.bfloat16

REF_SPLITS = (
    ("na_qkv", 3 * BR_W), ("na_z", BR_W),
    ("gla_q", GLA_HEADS * GLA_DK), ("gla_k", GLA_HEADS * GLA_DK), ("gla_v", BR_W), ("gla_z", BR_W),
    ("gla_g", 2 * GLA_RANK),
    ("gdn_qkv", 3 * BR_W), ("gdn_z", BR_W), ("gdn_a", 2 * GDN_HEADS), ("gdn_b", 2 * GDN_HEADS),
    ("hy_xv", 3 * BR_W), ("hy_z", BR_W),
)
MAIN_ORDER = ("na_qkv", "na_z", "gla_q", "gla_k", "gla_v", "gla_z", "gdn_qkv", "gdn_z", "hy_xv", "hy_z")
SMALL_ORDER = ("gla_g", "gdn_a", "gdn_b")
LANE = 128
N_MAIN = sum(w for n, w in REF_SPLITS if n in MAIN_ORDER)
N_SMALL = LANE
VMEM_LIMIT = 48 * 1024 * 1024

def _ref_offsets():
    out, o = {}, 0
    for name, w in REF_SPLITS:
        out[name] = (o, w)
        o += w
    return out


def _my_offsets():
    out, o = {}, 0
    for name in MAIN_ORDER:
        w = dict(REF_SPLITS)[name]
        out[name] = (o, w)
        o += w
    o = 0
    for name in SMALL_ORDER:
        w = dict(REF_SPLITS)[name]
        out[name] = (o, w)
        o += w
    return out


REF_OFF = _ref_offsets()
MY_OFF = _my_offsets()


def _permute_w_in(w_in_l):
    main = jnp.concatenate([w_in_l[:, REF_OFF[n][0]:REF_OFF[n][0] + REF_OFF[n][1]] for n in MAIN_ORDER], axis=1)
    small = jnp.concatenate([w_in_l[:, REF_OFF[n][0]:REF_OFF[n][0] + REF_OFF[n][1]] for n in SMALL_ORDER], axis=1)
    small = jnp.pad(small, ((0, 0), (0, N_SMALL - small.shape[1])))
    return main.astype(BF16), small.astype(BF16)


def _mod_kernel(c_ref, w_ref, b_ref, o_ref):
    c = c_ref[...]
    a = (c * jax.nn.sigmoid(c)).astype(BF16)
    o_ref[...] = jnp.dot(a, w_ref[...].astype(BF16), preferred_element_type=F32) + b_ref[...]


def modulation(c_all, w_mod_l, b_mod_l):
    r, d = c_all.shape
    n = w_mod_l.shape[1]
    tn = 768
    return pl.pallas_call(
        _mod_kernel,
        grid=(n // tn,),
        in_specs=[pl.BlockSpec((r, d), lambda j: (0, 0)),
                  pl.BlockSpec((d, tn), lambda j: (0, j)),
                  pl.BlockSpec((1, tn), lambda j: (0, j))],
        out_specs=pl.BlockSpec((r, tn), lambda j: (0, j)),
        out_shape=jax.ShapeDtypeStruct((r, n), F32),
        compiler_params=pltpu.CompilerParams(vmem_limit_bytes=VMEM_LIMIT),
        name="modulation",
    )(c_all, w_mod_l, b_mod_l.reshape(1, n))


def _prenorm_kernel(x_ref, g_ref, scale_ref, shift_ref, o_ref):
    x = x_ref[...]
    y = x * lax.rsqrt(jnp.mean(x * x, axis=-1, keepdims=True) + EPS)
    u = (y * g_ref[...]) * (1.0 + scale_ref[0]) + shift_ref[0]
    o_ref[...] = u.astype(o_ref.dtype)


def prenorm_mod(x2d, g_pre_l, scale, shift, rows_per_group, out_dtype=BF16):
    n, d = x2d.shape
    tm = 512
    tiles_per_group = rows_per_group // tm
    return pl.pallas_call(
        _prenorm_kernel,
        grid=(n // tm,),
        in_specs=[pl.BlockSpec((tm, d), lambda i: (i, 0)),
                  pl.BlockSpec((1, d), lambda i: (0, 0)),
                  pl.BlockSpec((1, 1, d), lambda i: (i // tiles_per_group, 0, 0)),
                  pl.BlockSpec((1, 1, d), lambda i: (i // tiles_per_group, 0, 0))],
        out_specs=pl.BlockSpec((tm, d), lambda i: (i, 0)),
        out_shape=jax.ShapeDtypeStruct((n, d), out_dtype),
        compiler_params=pltpu.CompilerParams(vmem_limit_bytes=VMEM_LIMIT),
        name="prenorm_mod",
    )(x2d, g_pre_l.reshape(1, d), scale, shift)


def _matmul_kernel(u_ref, w_ref, o_ref):
    o_ref[...] = jnp.dot(u_ref[...], w_ref[...], preferred_element_type=F32).astype(o_ref.dtype)


def matmul(u, w, tm, tn, out_dtype=F32, name="matmul"):
    n, k = u.shape
    m = w.shape[1]
    return pl.pallas_call(
        _matmul_kernel,
        grid=(m // tn, n // tm),
        in_specs=[pl.BlockSpec((tm, k), lambda j, i: (i, 0)),
                  pl.BlockSpec((k, tn), lambda j, i: (0, j))],
        out_specs=pl.BlockSpec((tm, tn), lambda j, i: (i, j)),
        out_shape=jax.ShapeDtypeStruct((n, m), out_dtype),
        compiler_params=pltpu.CompilerParams(vmem_limit_bytes=VMEM_LIMIT),
        name=name,
    )(u, w)


def _gate_merge_kernel(u_ref, y0_ref, y1_ref, y2_ref, y3_ref, wg_ref, bg_ref, wb_ref, o_ref):
    u = u_ref[...]
    merged = None
    for i, y_ref in enumerate((y0_ref, y1_ref, y2_ref, y3_ref)):
        g = jnp.dot(u, wg_ref[i], preferred_element_type=F32) + bg_ref[i]
        yb = jnp.dot(y_ref[...], wb_ref[i], preferred_element_type=F32)
        term = jax.nn.sigmoid(g) * yb
        merged = term if merged is None else merged + term
    o_ref[...] = merged.astype(o_ref.dtype)


def _out_proj_kernel(m_ref, wo_ref, h_ref, gate_ref, gpost_ref, o_ref):
    y = jnp.dot(m_ref[...], wo_ref[...], preferred_element_type=F32)
    yn = y * lax.rsqrt(jnp.mean(y * y, axis=-1, keepdims=True) + EPS) * gpost_ref[...]
    o_ref[...] = h_ref[...] + gate_ref[0] * yn


def merge(u, ys, w_gate_l, b_gate_l, w_branch_l, w_out_l, h2d, gate, g_post_l, rows_per_group):
    n, d = u.shape
    tm, tn = 1024, 256
    yspec = pl.BlockSpec((tm, BR_W), lambda i, j: (i, 0))
    merged = pl.pallas_call(
        _gate_merge_kernel,
        grid=(n // tm, d // tn),
        in_specs=[pl.BlockSpec((tm, d), lambda i, j: (i, 0)),
                  yspec, yspec, yspec, yspec,
                  pl.BlockSpec((N_BRANCH, d, tn), lambda i, j: (0, 0, j)),
                  pl.BlockSpec((N_BRANCH, 1, tn), lambda i, j: (0, 0, j)),
                  pl.BlockSpec((N_BRANCH, BR_W, tn), lambda i, j: (0, 0, j))],
        out_specs=pl.BlockSpec((tm, tn), lambda i, j: (i, j)),
        out_shape=jax.ShapeDtypeStruct((n, d), BF16),
        compiler_params=pltpu.CompilerParams(vmem_limit_bytes=VMEM_LIMIT),
        name="gate_merge",
    )(u, *[y.reshape(n, BR_W) for y in ys], w_gate_l.astype(BF16), b_gate_l.reshape(N_BRANCH, 1, d),
      w_branch_l.astype(BF16))
    tm = 512
    tiles_per_group = rows_per_group // tm
    return pl.pallas_call(
        _out_proj_kernel,
        grid=(n // tm,),
        in_specs=[pl.BlockSpec((tm, d), lambda i: (i, 0)),
                  pl.BlockSpec((d, d), lambda i: (0, 0), pipeline_mode=pl.Buffered(1)),
                  pl.BlockSpec((tm, d), lambda i: (i, 0)),
                  pl.BlockSpec((1, 1, d), lambda i: (i // tiles_per_group, 0, 0)),
                  pl.BlockSpec((1, d), lambda i: (0, 0))],
        out_specs=pl.BlockSpec((tm, d), lambda i: (i, 0)),
        out_shape=jax.ShapeDtypeStruct((n, d), F32),
        compiler_params=pltpu.CompilerParams(vmem_limit_bytes=VMEM_LIMIT),
        name="out_proj",
    )(merged, w_out_l.astype(BF16), h2d, gate, g_post_l.reshape(1, d))


NA_NEG = -1e30
NA_PAIR = LANE // NA_DH


NA_QROWS = 2
NA_KROWS = WIN_R + 2


def _na_pair_types(rows):
    assert rows % NA_QROWS == 0 and rows >= NA_KROWS
    layouts, index = [], []
    for r0 in range(0, rows, NA_QROWS):
        rs = [int(np.clip(r0 + i - WIN_R // 2, 0, rows - WIN_R)) for i in range(NA_QROWS)]
        u0 = min(rs[0], rows - NA_KROWS)
        lay = (r0 - u0,) + tuple(x - u0 for x in rs)
        if lay not in layouts:
            layouts.append(lay)
        index.append(layouts.index(lay))
    return layouts, np.asarray(index, np.int32)


def na_bias_table(rpb, rows):
    heads = rpb.shape[0]
    c = np.arange(GRID_W)[:, None]
    kc = np.arange(GRID_W)[None, :]
    cstart = np.clip(c - WIN_C // 2, 0, GRID_W - WIN_C)
    valid = (kc >= cstart) & (kc < cstart + WIN_C)
    dc = np.clip(kc - c + WIN_C - 1, 0, 2 * WIN_C - 2)
    onehot = ((dc[None] == np.arange(2 * WIN_C - 1)[:, None, None]) & valid[None]).astype(np.float32)
    by_dr = jnp.einsum("hrd,dck->hrck", rpb.astype(F32), jnp.asarray(onehot), precision=lax.Precision.HIGHEST)
    by_dr = jnp.where(valid, by_dr, NA_NEG)
    outside = jnp.full((heads, GRID_W, GRID_W), NA_NEG, F32)
    layouts, index = _na_pair_types(rows)
    tabs = []
    for lay in layouts:
        a, starts = lay[0], lay[1:]
        strips = []
        for i, st in enumerate(starts):
            tiles = [by_dr[:, j - a - i + WIN_R - 1] if st <= j < st + WIN_R else outside for j in range(NA_KROWS)]
            strips.append(jnp.concatenate(tiles, axis=-1))
        tabs.append(jnp.concatenate(strips, axis=1))
    return jnp.stack(tabs, axis=1), index


def _nt_dot(a, b):
    return lax.dot_general(a, b, (((1,), (1,)), ((), ())), preferred_element_type=F32)


def _na_kernel(lay_ref, q_ref, k_ref, v_ref, kc_ref, vc_ref, z_ref, bias_ref, o_ref, *, rows):
    nq = NA_QROWS * GRID_W
    nk = NA_KROWS * GRID_W
    lane = lax.broadcasted_iota(jnp.int32, (nq, LANE), 1)
    head_mask = [(lane >= hh * NA_DH) & (lane < (hh + 1) * NA_DH) for hh in range(NA_PAIR)]
    kc = kc_ref[0].astype(BF16)
    vc = vc_ref[0].astype(BF16)

    def body(i, carry):
        work = []
        for p in (2 * i, 2 * i + 1):
            r0 = p * NA_QROWS
            u0 = jnp.minimum(jnp.clip(r0 - WIN_R // 2, 0, rows - WIN_R), rows - NA_KROWS)
            q0 = pl.multiple_of(r0 * GRID_W, nq)
            k0 = pl.multiple_of(u0 * GRID_W, GRID_W)
            q = q_ref[0, pl.ds(q0, nq), :] * (NA_DH ** -0.5)
            qs = jnp.concatenate([jnp.where(m, q, 0.0) for m in head_mask], axis=0).astype(BF16)
            work.append(dict(q0=q0, k0=k0, lay=lay_ref[p], qs=qs))
        for w in work:
            w["s"] = _nt_dot(w["qs"], k_ref[0, pl.ds(w["k0"], nk), :].astype(BF16))
        for w in work:
            w["sc"] = _nt_dot(w["qs"], kc)
        for w in work:
            s = w["s"] + jnp.concatenate([bias_ref[hh, w["lay"]] for hh in range(NA_PAIR)], axis=0)
            m = jnp.maximum(jnp.max(s, axis=-1, keepdims=True), jnp.max(w["sc"], axis=-1, keepdims=True))
            pw = jnp.exp(s - m)
            pc = jnp.exp(w["sc"] - m)
            w["l"] = jnp.sum(pw, axis=-1, keepdims=True) + jnp.sum(pc, axis=-1, keepdims=True)
            w["pw"] = pw.astype(BF16)
            w["pc"] = pc.astype(BF16)
        for w in work:
            w["o"] = jnp.dot(w["pw"], v_ref[0, pl.ds(w["k0"], nk), :].astype(BF16), preferred_element_type=F32)
        for w in work:
            o = (w["o"] + jnp.dot(w["pc"], vc, preferred_element_type=F32)) / w["l"]
            out = o[0:nq]
            for hh in range(1, NA_PAIR):
                out = jnp.where(head_mask[hh], o[hh * nq:(hh + 1) * nq], out)
            z = z_ref[0, pl.ds(w["q0"], nq), :]
            o_ref[0, pl.ds(w["q0"], nq), :] = (out * (z * jax.nn.sigmoid(z))).astype(o_ref.dtype)
        return carry

    assert (rows // NA_QROWS) % 2 == 0
    lax.fori_loop(0, rows // NA_QROWS // 2, body, 0)


def na_branch(p_main, pc_main, rpb):
    b, t, _ = p_main.shape
    tc = pc_main.shape[1]
    rows = t // GRID_W
    nj = BR_W // LANE
    o_q = MY_OFF["na_qkv"][0] // LANE
    o_z = MY_OFF["na_z"][0] // LANE
    bias, layout_of_pair = na_bias_table(rpb, rows)
    lat = lambda o: pl.BlockSpec((1, t, LANE), lambda i, j: (i, 0, o + j))
    cx = lambda o: pl.BlockSpec((1, tc, LANE), lambda i, j: (i, 0, o + j))
    return pl.pallas_call(
        functools.partial(_na_kernel, rows=rows),
        grid=(b, nj),
        in_specs=[pl.BlockSpec(memory_space=pltpu.SMEM),
                  lat(o_q), lat(o_q + nj), lat(o_q + 2 * nj), cx(o_q + nj), cx(o_q + 2 * nj), lat(o_z),
                  pl.BlockSpec((NA_PAIR,) + bias.shape[1:], lambda i, j: (j, 0, 0, 0))],
        out_specs=pl.BlockSpec((1, t, LANE), lambda i, j: (i, 0, j)),
        out_shape=jax.ShapeDtypeStruct((b, t, BR_W), BF16),
        compiler_params=pltpu.CompilerParams(vmem_limit_bytes=VMEM_LIMIT),
        name="na_attention",
    )(jnp.asarray(layout_of_pair), p_main, p_main, p_main, pc_main, pc_main, p_main, bias)


def _ctx_attn_kernel(q_ref, k_ref, v_ref, z_ref, o_ref):
    tc = q_ref.shape[1]
    lane = lax.broadcasted_iota(jnp.int32, (tc, LANE), 1)
    q = q_ref[0] * (NA_DH ** -0.5)
    k = k_ref[0].astype(BF16)
    v = v_ref[0].astype(BF16)
    out = None
    for hh in range(NA_PAIR):
        mask = (lane >= hh * NA_DH) & (lane < (hh + 1) * NA_DH)
        s = _nt_dot(jnp.where(mask, q, 0.0).astype(BF16), k)
        p = jnp.exp(s - jnp.max(s, axis=-1, keepdims=True))
        o = jnp.dot(p.astype(BF16), v, preferred_element_type=F32) / jnp.sum(p, axis=-1, keepdims=True)
        out = o if out is None else jnp.where(mask, o, out)
    z = z_ref[0]
    o_ref[0] = (out * (z * jax.nn.sigmoid(z))).astype(o_ref.dtype)


def ctx_attention_branch(pc_main):
    b, tc, _ = pc_main.shape
    nj = BR_W // LANE
    o_q = MY_OFF["na_qkv"][0] // LANE
    o_z = MY_OFF["na_z"][0] // LANE
    cx = lambda o: pl.BlockSpec((1, tc, LANE), lambda i, j: (i, 0, o + j))
    return pl.pallas_call(
        _ctx_attn_kernel,
        grid=(b, nj),
        in_specs=[cx(o_q), cx(o_q + nj), cx(o_q + 2 * nj), cx(o_z)],
        out_specs=pl.BlockSpec((1, tc, LANE), lambda i, j: (i, 0, j)),
        out_shape=jax.ShapeDtypeStruct((b, tc, BR_W), BF16),
        compiler_params=pltpu.CompilerParams(vmem_limit_bytes=VMEM_LIMIT),
        name="ctx_attention",
    )(pc_main, pc_main, pc_main, pc_main)


def _hi_lo(x):
    hi = x.astype(BF16)
    return hi, (x - hi.astype(F32)).astype(BF16)


def _dot_hl(a, b):
    a1, a2 = _hi_lo(a)
    b1, b2 = _hi_lo(b)
    return (jnp.dot(a1, b1, preferred_element_type=F32) + jnp.dot(a1, b2, preferred_element_type=F32)
            + jnp.dot(a2, b1, preferred_element_type=F32))


def _tn_dot(a, b):
    return lax.dot_general(a, b, (((0,), (0,)), ((), ())), preferred_element_type=F32)


def _softplus(x):
    return jnp.maximum(x, 0.0) + jnp.log1p(jnp.exp(-jnp.abs(x)))


def _rms_rows(o, g):
    return o * lax.rsqrt(jnp.mean(o * o, axis=-1, keepdims=True) + EPS) * g


GLA_LEVELS = 6
GLA_PAIR = LANE // GLA_DK
GLA_SUM_ROWS = (2 + GLA_LEVELS) * CHUNK


def _gla_tables():
    assert CHUNK == 1 << GLA_LEVELS
    t = np.arange(CHUNK)[:, None]
    r = np.arange(CHUNK)[None, :]
    sums, masks = [], []
    for rev in (False, True):
        before = (r >= t) if rev else (r <= t)
        after = (r < t) if rev else (r > t)
        blocks = [before, after]
        lev_masks = []
        for lv in range(1, GLA_LEVELS + 1):
            seg = CHUNK >> lv
            st, sr = t // seg, r // seg
            q_half = (st % 2 == 0) if rev else (st % 2 == 1)
            k_half = ~q_half
            blocks.append((st == sr) & ((before & q_half) | (after & k_half)))
            lev_masks.append(q_half & (sr == (st + 1 if rev else st - 1)))
        lev_masks.append(r == t)
        sums.append(np.concatenate(blocks, axis=0))
        masks.append(np.stack([np.concatenate([m, m], axis=0) for m in lev_masks]))
    return np.stack(sums).astype(np.float32), np.stack(masks).astype(np.float32)


def _gla_rope_tables(t_len):
    pos = np.arange(t_len)
    row = (pos // GRID_W).astype(np.float64)
    col = (pos % GRID_W).astype(np.float64)
    n = GLA_DK // 4
    freqs = ROPE_THETA ** (-np.arange(n, dtype=np.float64) / n)
    ang = np.concatenate([row[:, None] * freqs, col[:, None] * freqs], axis=-1)
    cos = np.concatenate([np.cos(ang), np.cos(ang)] * GLA_PAIR, axis=-1)
    sin = np.concatenate([-np.sin(ang), np.sin(ang)] * GLA_PAIR, axis=-1)
    return cos.astype(np.float32), sin.astype(np.float32)


def _gla_gate_weights(wg2, bg):
    half = GLA_DK // 2
    lane = np.arange(LANE)
    w_all, b_all = [], []
    for d in range(2):
        wd, bd = [], []
        for j in range(GLA_HEADS // GLA_PAIR):
            cols = half * (GLA_PAIR * j + lane // GLA_DK) + lane % half
            w = jnp.zeros((N_SMALL, LANE), F32).at[MY_OFF["gla_g"][0] + d * GLA_RANK:
                                                    MY_OFF["gla_g"][0] + (d + 1) * GLA_RANK].set(wg2[d][:, cols])
            wd.append(w)
            bd.append(bg[d][cols].reshape(1, LANE))
        w_all.append(jnp.stack(wd))
        b_all.append(jnp.stack(bd))
    return jnp.stack(w_all), jnp.stack(b_all)


def _gla_local(chunks, q_ref, k_ref, v_ref, gs_ref, rope, wg_ref, bgate_ref, sums_ref, masks_ref, qr_s, ke_s, tot_s,
               acc_ref):
    lane = lax.broadcasted_iota(jnp.int32, (CHUNK, LANE), 1)
    first_half = (lane % GLA_DK) < GLA_DK // 2

    def stack_heads(a):
        return jnp.concatenate([jnp.where((lane >= hh * GLA_DK) & (lane < (hh + 1) * GLA_DK), a, 0.0)
                                for hh in range(GLA_PAIR)], axis=0).astype(BF16)

    per_chunk = []
    for c in chunks:
        rows = pl.ds(pl.multiple_of(c * CHUNK, CHUNK), CHUNK)
        q = q_ref[0, rows, :] * (GLA_DK ** -0.5)
        k = k_ref[0, rows, :]
        if rope is not None:
            cos = rope[0][rows, :]
            sin = rope[1][rows, :]

            def rot(x, cos=cos, sin=sin):
                partner = jnp.where(first_half, pltpu.roll(x, LANE - GLA_DK // 2, axis=1),
                                    pltpu.roll(x, GLA_DK // 2, axis=1))
                return x * cos + partner * sin
            q, k = rot(q), rot(k)
        per_chunk.append(dict(c=c, rows=rows, q=q, k=k, k16=k.astype(BF16), q_heads=stack_heads(q),
                              v16=v_ref[0, rows, :].astype(BF16), gs=_hi_lo(gs_ref[0, rows, :])))
    items = [(w, d) for w in per_chunk for d in range(2)]
    wg = [_hi_lo(wg_ref[d, 0]) for d in range(2)]
    xs = [jnp.dot(w["gs"][0], wg[d][0], preferred_element_type=F32) for w, d in items]
    xs = [x + jnp.dot(w["gs"][0], wg[d][1], preferred_element_type=F32) for x, (w, d) in zip(xs, items)]
    xs = [x + jnp.dot(w["gs"][1], wg[d][0], preferred_element_type=F32) + bgate_ref[d, 0] for x, (w, d) in zip(xs, items)]
    gs = [_hi_lo((jnp.minimum(x, 0.0) - jnp.log1p(jnp.exp(-jnp.abs(x)))) * (1.0 / GLA_TAU)) for x in xs]
    main = [jnp.dot(sums_ref[d, 0:2 * CHUNK], g[0], preferred_element_type=F32) for g, (w, d) in zip(gs, items)]
    main = [m + jnp.dot(sums_ref[d, 0:2 * CHUNK], g[1], preferred_element_type=F32)
            for m, g, (w, d) in zip(main, gs, items)]
    levs = [jnp.dot(sums_ref[d, 2 * CHUNK:], g[0], preferred_element_type=F32) for g, (w, d) in zip(gs, items)]
    for m, (w, d) in zip(main, items):
        run = m[0:CHUNK]
        total = run[0:1] if d else run[CHUNK - 1:CHUNK]
        qr_s[d, w["rows"], :] = (w["q"] * jnp.exp(run)).astype(BF16)
        ke_s[d, w["rows"], :] = (w["k"] * jnp.exp(m[CHUNK:])).astype(BF16)
        tot_s[d, pl.ds(w["c"], 1), :] = jnp.exp(total)
    atts = [_nt_dot(w["q_heads"], w["k16"]) * masks_ref[d, GLA_LEVELS] for w, d in items]
    for lv in range(GLA_LEVELS):
        parts = [jnp.exp(lev[lv * CHUNK:(lv + 1) * CHUNK]) for lev in levs]
        atts = [att + _nt_dot(stack_heads(w["q"] * part), (w["k"] * part).astype(BF16)) * masks_ref[d, lv]
                for att, part, (w, d) in zip(atts, parts, items)]
    for n, w in enumerate(per_chunk):
        av = jnp.dot(jnp.concatenate(atts[2 * n:2 * n + 2], axis=0).astype(BF16), w["v16"],
                     preferred_element_type=F32)
        acc_ref[w["rows"], :] = sum(
            jnp.concatenate([av[(GLA_PAIR * d + hh) * CHUNK:(GLA_PAIR * d + hh + 1) * CHUNK,
                                hh * GLA_DV:(hh + 1) * GLA_DV] for hh in range(GLA_PAIR)], axis=1) for d in range(2))


def _gla_recur(chunks, v_ref, qr_s, ke_s, tot_s, st_refs, acc_ref):
    rows = [pl.ds(pl.multiple_of(c * CHUNK, CHUNK), CHUNK) for c in chunks]
    sts = [st_refs[d][...] for d in range(2)]
    inters = [_nt_dot(qr_s[d, rows[d], :], sts[d].astype(BF16)) for d in range(2)]
    upds = [_tn_dot(v_ref[0, rows[d], :].astype(BF16), ke_s[d, rows[d], :]) for d in range(2)]
    srow = lax.broadcasted_iota(jnp.int32, upds[0].shape, 0) // GLA_DV
    scol = lax.broadcasted_iota(jnp.int32, upds[0].shape, 1) // GLA_DK
    for d in range(2):
        st_refs[d][...] = jnp.where(srow == scol, sts[d] * tot_s[d, pl.ds(chunks[d], 1), :] + upds[d], 0.0)
    for d in range(2):
        acc_ref[rows[d], :] += inters[d]


def _gla_finish(acc_ref, z_ref, norm_ref, o_ref, n_rows):
    tile = min(256, n_rows)
    for i in range(n_rows // tile):
        rows = slice(i * tile, (i + 1) * tile)
        o = acc_ref[rows, :]
        z = z_ref[0, rows, :]
        y = jnp.concatenate([_rms_rows(o[:, hh * GLA_DV:(hh + 1) * GLA_DV], norm_ref[...])
                             for hh in range(GLA_PAIR)], axis=1)
        o_ref[0, rows, :] = (y * (z * jax.nn.sigmoid(z))).astype(o_ref.dtype)


def _gla_kernel(q_ref, k_ref, v_ref, z_ref, gs_ref, qc_ref, kc_ref, vc_ref, zc_ref, gsc_ref, cos_ref, sin_ref,
                wg_ref, bgate_ref, norm_ref, sums_ref, masks_ref, o_ref, oc_ref, stf_ref, stb_ref, qr_s, ke_s, tot_s,
                acc_ref):
    st_refs = (stf_ref, stb_ref)
    for r in st_refs:
        r[...] = jnp.zeros_like(r)
    for (qq, kk, vv, gg), rope, zz_ref, out_ref in (((qc_ref, kc_ref, vc_ref, gsc_ref), None, zc_ref, oc_ref),
                                                    ((q_ref, k_ref, v_ref, gs_ref), (cos_ref, sin_ref), z_ref, o_ref)):
        n = qq.shape[1]
        n_chunks = n // CHUNK

        group = next(g for g in (4, 2) if n_chunks % g == 0)

        def local_step(i, carry, qq=qq, kk=kk, vv=vv, gg=gg, rope=rope, group=group):
            _gla_local(tuple(group * i + j for j in range(group)), qq, kk, vv, gg, rope, wg_ref, bgate_ref, sums_ref,
                       masks_ref, qr_s, ke_s, tot_s, acc_ref)
            return carry

        def recur_step(i, carry, vv=vv, n_chunks=n_chunks):
            _gla_recur((i, n_chunks - 1 - i), vv, qr_s, ke_s, tot_s, st_refs, acc_ref)
            return carry

        lax.fori_loop(0, n_chunks // group, local_step, 0)
        lax.fori_loop(0, n_chunks, recur_step, 0)
        _gla_finish(acc_ref, zz_ref, norm_ref, out_ref, n)


def gla_branch(p_main, p_small, pc_main, pc_small, wg2, bg, norm):
    b, t, _ = p_main.shape
    tc = pc_main.shape[1]
    nj = GLA_HEADS // GLA_PAIR
    vw = GLA_PAIR * GLA_DV
    o_q = MY_OFF["gla_q"][0] // LANE
    o_k = MY_OFF["gla_k"][0] // LANE
    o_v = MY_OFF["gla_v"][0] // vw
    o_z = MY_OFF["gla_z"][0] // vw
    sums, masks = _gla_tables()
    cos, sin = _gla_rope_tables(t)
    wge, bge = _gla_gate_weights(wg2.astype(F32), bg.astype(F32))

    def seq(n, w, o):
        return pl.BlockSpec((1, n, w), lambda i, j: (i, 0, o + j))

    def whole(n):
        return pl.BlockSpec((1, n, N_SMALL), lambda i, j: (i, 0, 0))

    def const(shape):
        return pl.BlockSpec(shape, lambda i, j: (0,) * len(shape))

    return pl.pallas_call(
        _gla_kernel,
        grid=(b, nj),
        in_specs=[seq(t, LANE, o_q), seq(t, LANE, o_k), seq(t, vw, o_v), seq(t, vw, o_z), whole(t),
                  seq(tc, LANE, o_q), seq(tc, LANE, o_k), seq(tc, vw, o_v), seq(tc, vw, o_z), whole(tc),
                  const((t, LANE)), const((t, LANE)),
                  pl.BlockSpec((2, 1, N_SMALL, LANE), lambda i, j: (0, j, 0, 0)),
                  pl.BlockSpec((2, 1, 1, LANE), lambda i, j: (0, j, 0, 0)),
                  const((1, GLA_DV)), const((2, GLA_SUM_ROWS, CHUNK)),
                  const((2, GLA_LEVELS + 1, GLA_PAIR * CHUNK, CHUNK))],
        out_specs=[pl.BlockSpec((1, t, vw), lambda i, j: (i, 0, j)), pl.BlockSpec((1, tc, vw), lambda i, j: (i, 0, j))],
        out_shape=[jax.ShapeDtypeStruct((b, t, BR_W), BF16), jax.ShapeDtypeStruct((b, tc, BR_W), BF16)],
        scratch_shapes=[pltpu.VMEM((vw, LANE), F32), pltpu.VMEM((vw, LANE), F32),
                        pltpu.VMEM((2, t, LANE), BF16), pltpu.VMEM((2, t, LANE), BF16),
                        pltpu.VMEM((2, t // CHUNK, LANE), F32),
                        pltpu.VMEM((t, vw), F32)],
        compiler_params=pltpu.CompilerParams(vmem_limit_bytes=VMEM_LIMIT),
        name="gla_scan",
    )(p_main, p_main, p_main, p_main, p_small, pc_main, pc_main, pc_main, pc_main, pc_small,
      jnp.asarray(cos), jnp.asarray(sin), wge, bge, norm.astype(F32).reshape(1, GLA_DV),
      jnp.asarray(sums, BF16), jnp.asarray(masks))


GDN_PAIR = 2 * CHUNK
GDN_STACK = 2 * GDN_PAIR


def _neumann(mats):
    rs = [-a for a in mats]
    ps = list(mats)
    for _ in range(int(math.log2(CHUNK)) - 1):
        p16 = [p.astype(BF16) for p in ps]
        ps = [jnp.dot(x, x, preferred_element_type=F32) for x in p16]
        rs = [r + p + jnp.dot(r.astype(BF16), p.astype(BF16), preferred_element_type=F32) for r, p in zip(rs, ps)]
    return rs


def _seg_scan(x, reverse):
    n = x.shape[0]
    pos = lax.broadcasted_iota(jnp.int32, x.shape, 0) % CHUNK
    s = 1
    while s < CHUNK:
        if reverse:
            x = x + jnp.where(pos < CHUNK - s, pltpu.roll(x, n - s, axis=0), 0.0)
        else:
            x = x + jnp.where(pos >= s, pltpu.roll(x, s, axis=0), 0.0)
        s *= 2
    return x


def _gdn_prepare(q_ref, k_ref, v_ref, gs_ref, wq_ref, wk_ref, wv_ref, sel_ref, coef_ref, qkv_s, gb_s):
    n = q_ref.shape[1]
    for i, (x_ref, w_ref) in enumerate(((q_ref, wq_ref), (k_ref, wk_ref), (v_ref, wv_ref))):
        y = _dwconv_rows(x_ref[0], w_ref, GDN_CONV)
        y = y * jax.nn.sigmoid(y)
        if i < 2:
            y = y * lax.rsqrt(jnp.sum(y * y, axis=-1, keepdims=True) + EPS)
        if i == 0:
            y = y * (GDN_DH ** -0.5)
        qkv_s[i, 0:n, :] = y
    gs_hi, gs_lo = _hi_lo(gs_ref[0])

    def pick(i):
        return (jnp.dot(gs_hi, sel_ref[0, i], preferred_element_type=F32)
                + jnp.dot(gs_lo, sel_ref[0, i], preferred_element_type=F32))

    for d in range(2):
        a = pick(d)
        bb = pick(2 + d)
        g = -jnp.exp(coef_ref[0, d:d + 1, :]) * _softplus(a + coef_ref[0, 2 + d:3 + d, :])
        run = _seg_scan(g, reverse=(d == 1))
        gb_s[d, 0:n, :] = run
        gb_s[2 + d, 0:n, :] = jax.nn.sigmoid(bb)
        last = 0 if d else CHUNK - 1
        total = run.reshape(n // CHUNK, CHUNK, LANE)[:, last:last + 1, :]
        total = jnp.broadcast_to(total, (n // CHUNK, CHUNK, LANE)).reshape(n, LANE)
        gb_s[4 + d, 0:n, :] = jnp.exp(total - run)


def _gdn_local(pairs, qkv_s, gb_s, negm_ref, x_s, b_s, qp_s, op_s):
    eye = None
    work = []
    for p in pairs:
        rows = pl.ds(pl.multiple_of(p * GDN_PAIR, GDN_PAIR), GDN_PAIR)
        q = qkv_s[0, rows, :]
        k = qkv_s[1, rows, :]
        v = qkv_s[2, rows, :]
        gam = [gb_s[d, rows, :] for d in range(2)]
        beta = [gb_s[2 + d, rows, :] for d in range(2)]
        egam = [jnp.exp(g) for g in gam]
        kb = [k * bt for bt in beta]
        k16 = k.astype(BF16)
        raw = _nt_dot(jnp.concatenate([kb[0], kb[1], q, q], axis=0).astype(BF16),
                      jnp.concatenate([k16, k16], axis=0))
        g2 = jnp.concatenate(gam, axis=0)
        g2t = g2.T
        diff = jnp.concatenate([g2, g2], axis=1) - jnp.concatenate([g2t, g2t], axis=0)
        dec = jnp.exp(jnp.minimum(diff, 0.0) + negm_ref[...])
        if eye is None:
            eye = (lax.broadcasted_iota(jnp.int32, dec.shape, 0)
                   == lax.broadcasted_iota(jnp.int32, dec.shape, 1)).astype(F32)
        rhs = jnp.concatenate([jnp.concatenate([v * beta[d], kb[d] * egam[d]], axis=1) for d in range(2)], axis=0)
        work.append(dict(p=p, rows=rows, q=q, k=k, egam=egam, rhs=rhs,
                         a_kk=raw[0:GDN_STACK] * dec, a_qk=raw[GDN_STACK:] * (dec + eye)))
    for w, r in zip(work, _neumann([w["a_kk"] for w in work])):
        w["sol"] = w["rhs"] + jnp.dot(r.astype(BF16), w["rhs"].astype(BF16), preferred_element_type=F32)
    for w in work:
        p, rows, q, k, egam = w["p"], w["rows"], w["q"], w["k"], w["egam"]
        sol16 = w["sol"].astype(BF16)
        ao = jnp.dot(w["a_qk"].astype(BF16), sol16, preferred_element_type=F32)
        for d in range(2):
            blk = slice(d * GDN_PAIR, (d + 1) * GDN_PAIR)
            op_s[d, rows, :] = ao[blk, 0:GDN_DH]
            qp_s[d, rows, :] = (q * egam[d] - ao[blk, GDN_DH:]).astype(BF16)
            kd = (k * gb_s[4 + d, rows, :]).astype(BF16)
            for jj in range(2):
                sub = slice(jj * CHUNK, (jj + 1) * CHUNK)
                bx = _tn_dot(kd[sub], sol16[d * GDN_PAIR + jj * CHUNK:d * GDN_PAIR + (jj + 1) * CHUNK])
                b_s[d, 2 * p + jj] = bx[:, 0:GDN_DH]
                x_s[d, 2 * p + jj] = bx[:, GDN_DH:].astype(BF16)


def _gdn_recur(chunks, gb_s, x_s, b_s, qp_s, op_s, st_refs, acc_ref):
    rows = [pl.ds(pl.multiple_of(c * CHUNK, CHUNK), CHUNK) for c in chunks]
    sts = [st_refs[d][...] for d in range(2)]
    xs = [jnp.dot(jnp.concatenate([x_s[d, chunks[d]], qp_s[d, rows[d], :]], axis=0), sts[d].astype(BF16),
                  preferred_element_type=F32) for d in range(2)]
    for d, c in enumerate(chunks):
        last = c * CHUNK + (0 if d else CHUNK - 1)
        st_refs[d][...] = sts[d] * jnp.exp(gb_s[d, pl.ds(last, 1), :]) + (b_s[d, c] - xs[d][0:GDN_DH])
    for d in range(2):
        acc_ref[rows[d], :] += xs[d][GDN_DH:] + op_s[d, rows[d], :]


def _gdn_finish(acc_ref, z_ref, norm_ref, o_ref, n_rows):
    tile = min(256, n_rows)
    for i in range(n_rows // tile):
        rows = slice(i * tile, (i + 1) * tile)
        z = z_ref[0, rows, :]
        o_ref[0, rows, :] = (_rms_rows(acc_ref[rows, :], norm_ref[...]) * (z * jax.nn.sigmoid(z))).astype(o_ref.dtype)


def _gdn_kernel(q_ref, k_ref, v_ref, z_ref, gs_ref, qc_ref, kc_ref, vc_ref, zc_ref, gsc_ref, wq_ref, wk_ref, wv_ref,
                sel_ref, coef_ref, norm_ref, negm_ref, o_ref, oc_ref, stf_ref, stb_ref, qkv_s, gb_s,
                x_s, b_s, qp_s, op_s, acc_ref):
    st_refs = (stf_ref, stb_ref)
    for r in st_refs:
        r[...] = jnp.zeros_like(r)
    for refs, zz_ref, out_ref in (((qc_ref, kc_ref, vc_ref, gsc_ref), zc_ref, oc_ref),
                                  ((q_ref, k_ref, v_ref, gs_ref), z_ref, o_ref)):
        n = refs[0].shape[1]
        n_chunks = n // CHUNK
        _gdn_prepare(*refs, wq_ref, wk_ref, wv_ref, sel_ref, coef_ref, qkv_s, gb_s)
        acc_ref[0:n, :] = jnp.zeros((n, LANE), F32)

        group = next(g for g in (4, 2) if (n // GDN_PAIR) % g == 0)

        def local_step(i, carry, group=group):
            _gdn_local(tuple(group * i + j for j in range(group)), qkv_s, gb_s, negm_ref, x_s, b_s, qp_s, op_s)
            return carry

        def recur_step(i, carry, n_chunks=n_chunks):
            _gdn_recur((i, n_chunks - 1 - i), gb_s, x_s, b_s, qp_s, op_s, st_refs, acc_ref)
            return carry

        lax.fori_loop(0, n // (group * GDN_PAIR), local_step, 0)
        lax.fori_loop(0, n_chunks, recur_step, 0)
        _gdn_finish(acc_ref, zz_ref, norm_ref, out_ref, n)


def _gdn_order_mask():
    ti = np.arange(GDN_STACK)[:, None]
    si = np.arange(GDN_STACK)[None, :]
    same = (ti // CHUNK) == (si // CHUNK)
    earlier = same & np.where(ti < GDN_PAIR, si < ti, si > ti)
    return np.where(earlier, 0.0, NA_NEG).astype(np.float32)


def gdn_branch(p_main, p_small, pc_main, pc_small, conv_w, a_log, dt_bias, norm):
    b, t, _ = p_main.shape
    tc = pc_main.shape[1]
    assert t % (2 * GDN_PAIR) == 0 and tc % (2 * GDN_PAIR) == 0 and tc <= t
    nh = GDN_HEADS
    o_q = MY_OFF["gdn_qkv"][0] // LANE
    o_z = MY_OFF["gdn_z"][0] // LANE
    wt = conv_w.T.astype(F32)
    sel = np.zeros((nh, 4, N_SMALL, LANE), np.float32)
    for h in range(nh):
        for d in range(2):
            sel[h, d, MY_OFF["gdn_a"][0] + d * nh + h, :] = 1.0
            sel[h, 2 + d, MY_OFF["gdn_b"][0] + d * nh + h, :] = 1.0
    coef = jnp.concatenate([a_log.astype(F32).T, dt_bias.astype(F32).T], axis=1)
    coef = jnp.broadcast_to(coef[:, :, None], (nh, 4, LANE))

    def seq(n, o):
        return pl.BlockSpec((1, n, LANE), lambda i, j: (i, 0, o + j))

    def whole(n):
        return pl.BlockSpec((1, n, N_SMALL), lambda i, j: (i, 0, 0))

    def wspec(o):
        return pl.BlockSpec((GDN_CONV, LANE), lambda i, j: (0, o + j))

    return pl.pallas_call(
        _gdn_kernel,
        grid=(b, nh),
        in_specs=[seq(t, o_q), seq(t, o_q + nh), seq(t, o_q + 2 * nh), seq(t, o_z), whole(t),
                  seq(tc, o_q), seq(tc, o_q + nh), seq(tc, o_q + 2 * nh), seq(tc, o_z), whole(tc),
                  wspec(0), wspec(nh), wspec(2 * nh),
                  pl.BlockSpec((1, 4, N_SMALL, LANE), lambda i, j: (j, 0, 0, 0)),
                  pl.BlockSpec((1, 4, LANE), lambda i, j: (j, 0, 0)),
                  pl.BlockSpec((1, GDN_DH), lambda i, j: (0, 0)),
                  pl.BlockSpec((GDN_STACK, GDN_STACK), lambda i, j: (0, 0))],
        out_specs=[pl.BlockSpec((1, t, LANE), lambda i, j: (i, 0, j)), pl.BlockSpec((1, tc, LANE), lambda i, j: (i, 0, j))],
        out_shape=[jax.ShapeDtypeStruct((b, t, BR_W), BF16), jax.ShapeDtypeStruct((b, tc, BR_W), BF16)],
        scratch_shapes=[pltpu.VMEM((GDN_DH, GDN_DH), F32), pltpu.VMEM((GDN_DH, GDN_DH), F32),
                        pltpu.VMEM((3, t, LANE), F32), pltpu.VMEM((6, t, LANE), F32),
                        pltpu.VMEM((2, t // CHUNK, GDN_DH, GDN_DH), BF16),
                        pltpu.VMEM((2, t // CHUNK, GDN_DH, GDN_DH), F32),
                        pltpu.VMEM((2, t, LANE), BF16), pltpu.VMEM((2, t, LANE), F32),
                        pltpu.VMEM((t, LANE), F32)],
        compiler_params=pltpu.CompilerParams(vmem_limit_bytes=VMEM_LIMIT),
        name="gdn_scan",
    )(p_main, p_main, p_main, p_main, p_small, pc_main, pc_main, pc_main, pc_main, pc_small,
      wt, wt, wt, jnp.asarray(sel, BF16), coef, norm.astype(F32).reshape(1, GDN_DH), jnp.asarray(_gdn_order_mask()))


HY_COLS = 256
HY_ROWS = 512


def dft_matrices(length):
    n = 2 * length
    inner = 64
    assert length % inner == 0
    r = jnp.arange(n, dtype=jnp.int32)
    k = jnp.where(r <= length, r, r - length)[:, None]
    a1 = ((k * (inner * jnp.arange(length // inner, dtype=jnp.int32))[None, :]) % n).astype(F32) * (2.0 * math.pi / n)
    a0 = ((k * jnp.arange(inner, dtype=jnp.int32)[None, :]) % n).astype(F32) * (2.0 * math.pi / n)
    c1, s1 = jnp.cos(a1)[:, :, None], jnp.sin(a1)[:, :, None]
    c0, s0 = jnp.cos(a0)[:, None, :], jnp.sin(a0)[:, None, :]
    is_cos = (r <= length)[:, None, None]
    f = jnp.where(is_cos, c1 * c0 - s1 * s0, -(s1 * c0 + c1 * s0)).reshape(n, length)
    w = (jnp.where((r == 0) | (r == length), 1.0, 2.0) / n)[:, None]
    return f.astype(BF16), (f * w).T.astype(BF16)


def _hy_embedding(length):
    t = np.linspace(0.0, 1.0, length)[:, None]
    bands = (HY_EMB - 1) // 2
    wpos = 2.0 * math.pi * np.arange(length)[:, None] / length
    f = np.linspace(1e-4, bands - 1, bands)[None]
    z = np.concatenate([t, np.cos(f * wpos), -np.sin(f * wpos)], axis=-1)
    z = np.pad(z, ((0, 0), (0, HY_FFN - HY_EMB)))
    deltas = np.abs(np.linspace(math.log(HY_TARGET) / HY_SLOW, math.log(HY_TARGET) / HY_FAST, BR_W))[None]
    return z.astype(np.float32), deltas.astype(np.float32)


def _hy_filter_kernel(z_ref, win_ref, bin_ref, wmid_ref, bmid_ref, freq_ref, wout_ref, delta_ref, o_ref):
    hp = lax.Precision.HIGHEST
    length = z_ref.shape[0]
    h = jnp.sin(freq_ref[0] * (jnp.dot(z_ref[...], win_ref[...], precision=hp, preferred_element_type=F32)
                               + bin_ref[...]))
    for i in range(HY_INNER):
        h = jnp.sin(freq_ref[i + 1] * (jnp.dot(h, wmid_ref[i], precision=hp, preferred_element_type=F32)
                                       + bmid_ref[i]))
    h = jnp.dot(h, wout_ref[...], precision=hp, preferred_element_type=F32)
    t = lax.broadcasted_iota(jnp.int32, (length, BR_W), 0).astype(F32) * (1.0 / (length - 1))
    decay = jnp.exp(-t * delta_ref[...])
    h_f = h[:, :BR_W] * decay
    h_b = h[:, BR_W:] * decay
    for i, a in enumerate((h_f + h_b, h_f - h_b)):
        hi = a.astype(BF16)
        lo = (a - hi.astype(F32)).astype(BF16)
        o_ref[:, (2 * i) * BR_W:(2 * i + 1) * BR_W] = hi
        o_ref[:, (2 * i + 1) * BR_W:(2 * i + 2) * BR_W] = lo


def _hy_spectrum_kernel(f_ref, a_ref, ks_ref, kd_ref, *, length):
    i = pl.program_id(0)
    tm = f_ref.shape[0]
    r = jnp.dot(f_ref[...], a_ref[...], preferred_element_type=F32)
    ks_ref[...] = r[:, :BR_W] + r[:, BR_W:2 * BR_W]
    kd = r[:, 2 * BR_W:3 * BR_W] + r[:, 3 * BR_W:]
    row = lax.broadcasted_iota(jnp.int32, kd.shape, 0) + i * tm
    kd_ref[...] = jnp.where(row == length, 0.0, kd)


def hyena_filter_spectrum(length, fwd, filt):
    w_in, b_in, w_mid, b_mid, freq, w_out = filt
    z, deltas = _hy_embedding(length)
    n = 2 * length
    parts = pl.pallas_call(
        _hy_filter_kernel,
        out_shape=jax.ShapeDtypeStruct((length, 4 * BR_W), BF16),
        compiler_params=pltpu.CompilerParams(vmem_limit_bytes=VMEM_LIMIT),
        name="hyena_filter",
    )(jnp.asarray(z), jnp.pad(w_in, ((0, HY_FFN - HY_EMB), (0, 0))), b_in.reshape(1, HY_FFN), w_mid,
      b_mid.reshape(HY_INNER, 1, HY_FFN), freq.reshape(HY_INNER + 1, 1, HY_FFN), w_out, jnp.asarray(deltas))
    tm = min(HY_ROWS, n)
    return pl.pallas_call(
        functools.partial(_hy_spectrum_kernel, length=length),
        grid=(n // tm,),
        in_specs=[pl.BlockSpec((tm, length), lambda i: (i, 0)),
                  pl.BlockSpec((length, 4 * BR_W), lambda i: (0, 0))],
        out_specs=[pl.BlockSpec((tm, BR_W), lambda i: (i, 0)), pl.BlockSpec((tm, BR_W), lambda i: (i, 0))],
        out_shape=[jax.ShapeDtypeStruct((n, BR_W), F32), jax.ShapeDtypeStruct((n, BR_W), F32)],
        compiler_params=pltpu.CompilerParams(vmem_limit_bytes=VMEM_LIMIT),
        name="hyena_spectrum",
    )(fwd, parts)


def _shift_rows(x, off):
    t = x.shape[0]
    edge = 8
    assert 0 < abs(off) < edge and t % edge == 0
    rolled = pltpu.roll(x, (-off) % t, axis=0)
    row = lax.broadcasted_iota(jnp.int32, (edge, x.shape[1]), 0)
    if off < 0:
        return jnp.concatenate([jnp.where(row >= -off, rolled[0:edge], 0.0), rolled[edge:]], axis=0)
    return jnp.concatenate([rolled[0:t - edge], jnp.where(row < edge - off, rolled[t - edge:], 0.0)], axis=0)


def _dwconv_rows(x, w_ref, k):
    out = x * w_ref[k // 2:k // 2 + 1, :]
    for j in range(k):
        if j != k // 2:
            out = out + _shift_rows(x, j - k // 2) * w_ref[j:j + 1, :]
    return out


def _hy_prep_kernel(x0_ref, x1_ref, v_ref, z_ref, w0_ref, w1_ref, wv_ref, b0_ref, b1_ref, bv_ref, vx_ref, x0g_ref):
    x0 = _dwconv_rows(x0_ref[0], w0_ref, HY_CONV) + b0_ref[...]
    x1 = _dwconv_rows(x1_ref[0], w1_ref, HY_CONV) + b1_ref[...]
    v = _dwconv_rows(v_ref[0], wv_ref, HY_CONV) + bv_ref[...]
    z = z_ref[0]
    vx_ref[0] = v * x1
    x0g_ref[0] = x0 * (z * jax.nn.sigmoid(z))


def hyena_prep(p_main, conv_w, conv_b):
    b, length, _ = p_main.shape
    cb = LANE
    nj = BR_W // cb
    o_xv = MY_OFF["hy_xv"][0] // cb
    o_z = MY_OFF["hy_z"][0] // cb
    wt = conv_w.T
    bt = conv_b.reshape(1, 3 * BR_W)
    dspec = lambda o: pl.BlockSpec((1, length, cb), lambda i, j: (i, 0, o + j))
    wspec = lambda o: pl.BlockSpec((HY_CONV, cb), lambda i, j: (0, o + j))
    bspec = lambda o: pl.BlockSpec((1, cb), lambda i, j: (0, o + j))
    ospec = pl.BlockSpec((1, length, cb), lambda i, j: (i, 0, j))
    return pl.pallas_call(
        _hy_prep_kernel,
        grid=(b, nj),
        in_specs=[dspec(o_xv), dspec(o_xv + nj), dspec(o_xv + 2 * nj), dspec(o_z),
                  wspec(0), wspec(nj), wspec(2 * nj), bspec(0), bspec(nj), bspec(2 * nj)],
        out_specs=[ospec, ospec],
        out_shape=[jax.ShapeDtypeStruct((b, length, BR_W), F32)] * 2,
        compiler_params=pltpu.CompilerParams(vmem_limit_bytes=VMEM_LIMIT),
        name="hyena_prep",
    )(p_main, p_main, p_main, p_main, wt, wt, wt, bt, bt, bt)


def _hy_fwd_kernel(vx_ref, f_ref, kr_ref, ki_ref, knyq_ref, y_ref, *, length):
    v = vx_ref[0].astype(BF16)
    tm = min(HY_ROWS, length)
    for kt in range(length // tm):
        lo = kt * tm
        a = jnp.dot(f_ref[lo:lo + tm, :], v, preferred_element_type=F32)
        bh = jnp.dot(f_ref[length + lo:length + lo + tm, :], v, preferred_element_type=F32)
        kr = kr_ref[lo:lo + tm, :]
        ki = ki_ref[lo:lo + tm, :]
        if kt == 0:
            row = lax.broadcasted_iota(jnp.int32, kr.shape, 0)
            kb = jnp.where(row == 0, knyq_ref[0:1, :], kr)
        else:
            kb = kr
        y_ref[0, lo:lo + tm, :] = (a * kr - bh * ki).astype(BF16)
        y_ref[0, length + lo:length + lo + tm, :] = (a * ki + bh * kb).astype(BF16)


def _hy_inv_kernel(y_ref, inv_ref, vx_ref, x0g_ref, skip_ref, o_ref, *, length):
    tm = min(HY_ROWS, length)
    y = y_ref[0]
    for tt in range(length // tm):
        lo = tt * tm
        conv = jnp.dot(inv_ref[lo:lo + tm, :], y, preferred_element_type=F32)
        o_ref[0, lo:lo + tm, :] = ((conv + skip_ref[...] * vx_ref[0, lo:lo + tm, :])
                                   * x0g_ref[0, lo:lo + tm, :]).astype(o_ref.dtype)


def hyena_branch(p_main, conv_w, conv_b, skip, fwd, inv, ks, kd):
    b, length, _ = p_main.shape
    n = 2 * length
    vx, x0g = hyena_prep(p_main, conv_w, conv_b)
    cb = HY_COLS
    nj = BR_W // cb
    resident = lambda shape: pl.BlockSpec(shape, lambda j, i: (0, 0), pipeline_mode=pl.Buffered(1))
    y = pl.pallas_call(
        functools.partial(_hy_fwd_kernel, length=length),
        grid=(nj, b),
        in_specs=[pl.BlockSpec((1, length, cb), lambda j, i: (i, 0, j)),
                  resident((n, length)),
                  pl.BlockSpec((length, cb), lambda j, i: (0, j)),
                  pl.BlockSpec((length, cb), lambda j, i: (1, j)),
                  pl.BlockSpec((8, cb), lambda j, i: (length // 8, j))],
        out_specs=pl.BlockSpec((1, n, cb), lambda j, i: (i, 0, j)),
        out_shape=jax.ShapeDtypeStruct((b, n, BR_W), BF16),
        compiler_params=pltpu.CompilerParams(vmem_limit_bytes=VMEM_LIMIT),
        name="hyena_dft",
    )(vx, fwd, ks, kd, ks)
    return pl.pallas_call(
        functools.partial(_hy_inv_kernel, length=length),
        grid=(nj, b),
        in_specs=[pl.BlockSpec((1, n, cb), lambda j, i: (i, 0, j)),
                  resident((length, n)),
                  pl.BlockSpec((1, length, cb), lambda j, i: (i, 0, j)),
                  pl.BlockSpec((1, length, cb), lambda j, i: (i, 0, j)),
                  pl.BlockSpec((1, cb), lambda j, i: (0, j))],
        out_specs=pl.BlockSpec((1, length, cb), lambda j, i: (i, 0, j)),
        out_shape=jax.ShapeDtypeStruct((b, length, BR_W), BF16),
        compiler_params=pltpu.CompilerParams(vmem_limit_bytes=VMEM_LIMIT),
        name="hyena_idft",
    )(y, inv, vx, x0g, skip.reshape(1, BR_W))


def mixers(p_main, p_small, pc_main, pc_small, na_rpb, gla_wg2, gla_bg, gla_norm, gdn_conv, gdn_a_log, gdn_dt_bias,
           gdn_norm, hy_conv, hy_conv_b, hy_filt, hy_skip, with_ctx, dft_lat, dft_ctx):
    t = p_main.shape[1]
    ya = na_branch(p_main, pc_main, na_rpb)
    yb, yb_c = gla_branch(p_main, p_small, pc_main, pc_small, gla_wg2, gla_bg, gla_norm)
    yc, yc_c = gdn_branch(p_main, p_small, pc_main, pc_small, gdn_conv, gdn_a_log, gdn_dt_bias, gdn_norm)
    fwd, inv = dft_lat
    ks, kd = hyena_filter_spectrum(t, fwd, hy_filt)
    yd = hyena_branch(p_main, hy_conv, hy_conv_b, hy_skip, fwd, inv, ks, kd)
    if not with_ctx:
        return (ya, yb, yc, yd), None
    ya_c = ctx_attention_branch(pc_main)
    fwd_c, inv_c = dft_ctx
    ks_c, kd_c = hyena_filter_spectrum(pc_main.shape[1], fwd_c, hy_filt)
    yd_c = hyena_branch(pc_main, hy_conv, hy_conv_b, hy_skip, fwd_c, inv_c, ks_c, kd_c)
    return (ya, yb, yc, yd), (ya_c, yb_c, yc_c, yd_c)


def kernel(x, c, ctx, c_ctx, w_mod, b_mod, g_pre, g_post, w_in, w_gate, b_gate, w_branch, w_out, na_rpb, gla_wg2, gla_bg, gla_norm, gdn_conv, gdn_a_log, gdn_dt_bias, gdn_norm, hy_conv, hy_conv_b, hy_w_in, hy_b_in, hy_w_mid, hy_b_mid, hy_freq, hy_w_out, hy_skip):
    b, t, d = x.shape
    tc = ctx.shape[1]
    h = x.reshape(b * t, d)
    hc = ctx.reshape(b * tc, d)
    c_all = jnp.concatenate([c, c_ctx[None], jnp.zeros((7, d), F32)], axis=0)
    dft_lat = dft_matrices(t)
    dft_ctx = dft_matrices(tc)
    for l in range(DEPTH):
        with_ctx = l < DEPTH - 1
        mod = modulation(c_all, w_mod[l], b_mod[l])
        shift, scale, gate = [mod[:, i * d:(i + 1) * d].reshape(-1, 1, d) for i in range(3)]
        u = prenorm_mod(h, g_pre[l], scale[:b], shift[:b], t)
        uc = prenorm_mod(hc, g_pre[l], scale[b:b + 1], shift[b:b + 1], b * tc)
        w_main, w_small = _permute_w_in(w_in[l])
        p_main = matmul(u, w_main, 1024, 1536, name="in_proj").reshape(b, t, N_MAIN)
        p_small = matmul(u, w_small, 1024, N_SMALL, name="in_proj_small").reshape(b, t, N_SMALL)
        pc_main = matmul(uc, w_main, 1024, 1536, name="in_proj_ctx").reshape(b, tc, N_MAIN)
        pc_small = matmul(uc, w_small, 1024, N_SMALL, name="in_proj_ctx_small").reshape(b, tc, N_SMALL)
        ys, ycs = mixers(p_main, p_small, pc_main, pc_small,
                         na_rpb[l], gla_wg2[l], gla_bg[l], gla_norm[l], gdn_conv[l], gdn_a_log[l], gdn_dt_bias[l],
                         gdn_norm[l], hy_conv[l], hy_conv_b[l],
                         (hy_w_in[l], hy_b_in[l], hy_w_mid[l], hy_b_mid[l], hy_freq[l], hy_w_out[l]),
                         hy_skip[l], with_ctx, dft_lat, dft_ctx)
        h_new = merge(u, ys, w_gate[l], b_gate[l], w_branch[l], w_out[l], h, gate[:b], g_post[l], t)
        if with_ctx:
            hc = merge(uc, ycs, w_gate[l], b_gate[l], w_branch[l], w_out[l], hc, gate[b:b + 1], g_post[l], b * tc)
        h = h_new
    return h.reshape(b, t, d)
```

```python
import functools
import math

import numpy as np
import jax
import jax.numpy as jnp
from jax import lax
from jax.experimental import pallas as pl
from jax.experimental.pallas import tpu as pltpu

D_MODEL = 2048
BATCH = 16
SEQ = 2048
DEPTH = 2
GRID_W = 64
CTX_LEN = 256
N_BRANCH = 4
BR_W = D_MODEL // N_BRANCH
NA_HEADS = 8
NA_DH = BR_W // NA_HEADS
WIN_R = 8
WIN_C = 16
QBLK_C = 16
KBLK_C = WIN_C + QBLK_C
N_CBLK = GRID_W // QBLK_C
GLA_HEADS = 4
GLA_DV = BR_W // GLA_HEADS
GLA_DK = GLA_DV // 2
GLA_RANK = 16
GLA_TAU = 16.0
CHUNK = 64
GDN_HEADS = 4
GDN_DH = BR_W // GDN_HEADS
GDN_CONV = 5
HY_CONV = 3
HY_EMB = 33
HY_FFN = 64
HY_INNER = 2
HY_TARGET = 1e-2
HY_FAST = 0.3
HY_SLOW = 1.5
ROPE_THETA = 10000.0
EPS = 1e-6
F32 = jnp.float32
BF16 = jnp.bfloat16

REF_SPLITS = (
    ("na_qkv", 3 * BR_W), ("na_z", BR_W),
    ("gla_q", GLA_HEADS * GLA_DK), ("gla_k", GLA_HEADS * GLA_DK), ("gla_v", BR_W), ("gla_z", BR_W),
    ("gla_g", 2 * GLA_RANK),
    ("gdn_qkv", 3 * BR_W), ("gdn_z", BR_W), ("gdn_a", 2 * GDN_HEADS), ("gdn_b", 2 * GDN_HEADS),
    ("hy_xv", 3 * BR_W), ("hy_z", BR_W),
)
MAIN_ORDER = ("na_qkv", "na_z", "gla_q", "gla_k", "gla_v", "gla_z", "gdn_qkv", "gdn_z", "hy_xv", "hy_z")
SMALL_ORDER = ("gla_g", "gdn_a", "gdn_b")
LANE = 128
N_MAIN = sum(w for n, w in REF_SPLITS if n in MAIN_ORDER)
N_SMALL = LANE
VMEM_LIMIT = 48 * 1024 * 1024

def _ref_offsets():
    out, o = {}, 0
    for name, w in REF_SPLITS:
        out[name] = (o, w)
        o += w
    return out


def _my_offsets():
    out, o = {}, 0
    for name in MAIN_ORDER:
        w = dict(REF_SPLITS)[name]
        out[name] = (o, w)
        o += w
    o = 0
    for name in SMALL_ORDER:
        w = dict(REF_SPLITS)[name]
        out[name] = (o, w)
        o += w
    return out


REF_OFF = _ref_offsets()
MY_OFF = _my_offsets()


def _permute_w_in(w_in_l):
    main = jnp.concatenate([w_in_l[:, REF_OFF[n][0]:REF_OFF[n][0] + REF_OFF[n][1]] for n in MAIN_ORDER], axis=1)
    small = jnp.concatenate([w_in_l[:, REF_OFF[n][0]:REF_OFF[n][0] + REF_OFF[n][1]] for n in SMALL_ORDER], axis=1)
    small = jnp.pad(small, ((0, 0), (0, N_SMALL - small.shape[1])))
    return main.astype(BF16), small.astype(BF16)


def _mod_kernel(c_ref, w_ref, b_ref, o_ref):
    c = c_ref[...]
    a = (c * jax.nn.sigmoid(c)).astype(BF16)
    o_ref[...] = jnp.dot(a, w_ref[...].astype(BF16), preferred_element_type=F32) + b_ref[...]


def modulation(c_all, w_mod_l, b_mod_l):
    r, d = c_all.shape
    n = w_mod_l.shape[1]
    tn = 768
    return pl.pallas_call(
        _mod_kernel,
        grid=(n // tn,),
        in_specs=[pl.BlockSpec((r, d), lambda j: (0, 0)),
                  pl.BlockSpec((d, tn), lambda j: (0, j)),
                  pl.BlockSpec((1, tn), lambda j: (0, j))],
        out_specs=pl.BlockSpec((r, tn), lambda j: (0, j)),
        out_shape=jax.ShapeDtypeStruct((r, n), F32),
        compiler_params=pltpu.CompilerParams(vmem_limit_bytes=VMEM_LIMIT),
        name="modulation",
    )(c_all, w_mod_l, b_mod_l.reshape(1, n))


def _prenorm_kernel(x_ref, g_ref, scale_ref, shift_ref, o_ref):
    x = x_ref[...]
    y = x * lax.rsqrt(jnp.mean(x * x, axis=-1, keepdims=True) + EPS)
    u = (y * g_ref[...]) * (1.0 + scale_ref[0]) + shift_ref[0]
    o_ref[...] = u.astype(o_ref.dtype)


def prenorm_mod(x2d, g_pre_l, scale, shift, rows_per_group, out_dtype=BF16):
    n, d = x2d.shape
    tm = 512
    tiles_per_group = rows_per_group // tm
    return pl.pallas_call(
        _prenorm_kernel,
        grid=(n // tm,),
        in_specs=[pl.BlockSpec((tm, d), lambda i: (i, 0)),
                  pl.BlockSpec((1, d), lambda i: (0, 0)),
                  pl.BlockSpec((1, 1, d), lambda i: (i // tiles_per_group, 0, 0)),
                  pl.BlockSpec((1, 1, d), lambda i: (i // tiles_per_group, 0, 0))],
        out_specs=pl.BlockSpec((tm, d), lambda i: (i, 0)),
        out_shape=jax.ShapeDtypeStruct((n, d), out_dtype),
        compiler_params=pltpu.CompilerParams(vmem_limit_bytes=VMEM_LIMIT),
        name="prenorm_mod",
    )(x2d, g_pre_l.reshape(1, d), scale, shift)


def _matmul_kernel(u_ref, w_ref, o_ref):
    o_ref[...] = jnp.dot(u_ref[...], w_ref[...], preferred_element_type=F32).astype(o_ref.dtype)


def matmul(u, w, tm, tn, out_dtype=F32, name="matmul"):
    n, k = u.shape
    m = w.shape[1]
    return pl.pallas_call(
        _matmul_kernel,
        grid=(m // tn, n // tm),
        in_specs=[pl.BlockSpec((tm, k), lambda j, i: (i, 0)),
                  pl.BlockSpec((k, tn), lambda j, i: (0, j))],
        out_specs=pl.BlockSpec((tm, tn), lambda j, i: (i, j)),
        out_shape=jax.ShapeDtypeStruct((n, m), out_dtype),
        compiler_params=pltpu.CompilerParams(vmem_limit_bytes=VMEM_LIMIT),
        name=name,
    )(u, w)


def _gate_merge_kernel(u_ref, y0_ref, y1_ref, y2_ref, y3_ref, wg_ref, bg_ref, wb_ref, o_ref):
    u = u_ref[...]
    merged = None
    for i, y_ref in enumerate((y0_ref, y1_ref, y2_ref, y3_ref)):
        g = jnp.dot(u, wg_ref[i], preferred_element_type=F32) + bg_ref[i]
        yb = jnp.dot(y_ref[...], wb_ref[i], preferred_element_type=F32)
        term = jax.nn.sigmoid(g) * yb
        merged = term if merged is None else merged + term
    o_ref[...] = merged.astype(o_ref.dtype)


def _out_proj_kernel(m_ref, wo_ref, h_ref, gate_ref, gpost_ref, o_ref):
    y = jnp.dot(m_ref[...], wo_ref[...], preferred_element_type=F32)
    yn = y * lax.rsqrt(jnp.mean(y * y, axis=-1, keepdims=True) + EPS) * gpost_ref[...]
    o_ref[...] = h_ref[...] + gate_ref[0] * yn


def merge(u, ys, w_gate_l, b_gate_l, w_branch_l, w_out_l, h2d, gate, g_post_l, rows_per_group):
    n, d = u.shape
    tm, tn = 1024, 256
    yspec = pl.BlockSpec((tm, BR_W), lambda i, j: (i, 0))
    merged = pl.pallas_call(
        _gate_merge_kernel,
        grid=(n // tm, d // tn),
        in_specs=[pl.BlockSpec((tm, d), lambda i, j: (i, 0)),
                  yspec, yspec, yspec, yspec,
                  pl.BlockSpec((N_BRANCH, d, tn), lambda i, j: (0, 0, j)),
                  pl.BlockSpec((N_BRANCH, 1, tn), lambda i, j: (0, 0, j)),
                  pl.BlockSpec((N_BRANCH, BR_W, tn), lambda i, j: (0, 0, j))],
        out_specs=pl.BlockSpec((tm, tn), lambda i, j: (i, j)),
        out_shape=jax.ShapeDtypeStruct((n, d), BF16),
        compiler_params=pltpu.CompilerParams(vmem_limit_bytes=VMEM_LIMIT),
        name="gate_merge",
    )(u, *[y.reshape(n, BR_W) for y in ys], w_gate_l.astype(BF16), b_gate_l.reshape(N_BRANCH, 1, d),
      w_branch_l.astype(BF16))
    tm = 512
    tiles_per_group = rows_per_group // tm
    return pl.pallas_call(
        _out_proj_kernel,
        grid=(n // tm,),
        in_specs=[pl.BlockSpec((tm, d), lambda i: (i, 0)),
                  pl.BlockSpec((d, d), lambda i: (0, 0), pipeline_mode=pl.Buffered(1)),
                  pl.BlockSpec((tm, d), lambda i: (i, 0)),
                  pl.BlockSpec((1, 1, d), lambda i: (i // tiles_per_group, 0, 0)),
                  pl.BlockSpec((1, d), lambda i: (0, 0))],
        out_specs=pl.BlockSpec((tm, d), lambda i: (i, 0)),
        out_shape=jax.ShapeDtypeStruct((n, d), F32),
        compiler_params=pltpu.CompilerParams(vmem_limit_bytes=VMEM_LIMIT),
        name="out_proj",
    )(merged, w_out_l.astype(BF16), h2d, gate, g_post_l.reshape(1, d))


NA_NEG = -1e30
NA_PAIR = LANE // NA_DH


NA_QROWS = 2
NA_KROWS = WIN_R + 2


def _na_pair_types(rows):
    assert rows % NA_QROWS == 0 and rows >= NA_KROWS
    layouts, index = [], []
    for r0 in range(0, rows, NA_QROWS):
        rs = [int(np.clip(r0 + i - WIN_R // 2, 0, rows - WIN_R)) for i in range(NA_QROWS)]
        u0 = min(rs[0], rows - NA_KROWS)
        lay = (r0 - u0,) + tuple(x - u0 for x in rs)
        if lay not in layouts:
            layouts.append(lay)
        index.append(layouts.index(lay))
    return layouts, np.asarray(index, np.int32)


def na_bias_table(rpb, rows):
    heads = rpb.shape[0]
    c = np.arange(GRID_W)[:, None]
    kc = np.arange(GRID_W)[None, :]
    cstart = np.clip(c - WIN_C // 2, 0, GRID_W - WIN_C)
    valid = (kc >= cstart) & (kc < cstart + WIN_C)
    dc = np.clip(kc - c + WIN_C - 1, 0, 2 * WIN_C - 2)
    onehot = ((dc[None] == np.arange(2 * WIN_C - 1)[:, None, None]) & valid[None]).astype(np.float32)
    by_dr = jnp.einsum("hrd,dck->hrck", rpb.astype(F32), jnp.asarray(onehot), precision=lax.Precision.HIGHEST)
    by_dr = jnp.where(valid, by_dr, NA_NEG)
    outside = jnp.full((heads, GRID_W, GRID_W), NA_NEG, F32)
    layouts, index = _na_pair_types(rows)
    tabs = []
    for lay in layouts:
        a, starts = lay[0], lay[1:]
        strips = []
        for i, st in enumerate(starts):
            tiles = [by_dr[:, j - a - i + WIN_R - 1] if st <= j < st + WIN_R else outside for j in range(NA_KROWS)]
            strips.append(jnp.concatenate(tiles, axis=-1))
        tabs.append(jnp.concatenate(strips, axis=1))
    return jnp.stack(tabs, axis=1), index


def _nt_dot(a, b):
    return lax.dot_general(a, b, (((1,), (1,)), ((), ())), preferred_element_type=F32)


def _na_kernel(lay_ref, q_ref, k_ref, v_ref, kc_ref, vc_ref, z_ref, bias_ref, o_ref, *, rows):
    nq = NA_QROWS * GRID_W
    nk = NA_KROWS * GRID_W
    lane = lax.broadcasted_iota(jnp.int32, (nq, LANE), 1)
    head_mask = [(lane >= hh * NA_DH) & (lane < (hh + 1) * NA_DH) for hh in range(NA_PAIR)]
    kc = kc_ref[0].astype(BF16)
    vc = vc_ref[0].astype(BF16)

    def body(i, carry):
        work = []
        for p in (2 * i, 2 * i + 1):
            r0 = p * NA_QROWS
            u0 = jnp.minimum(jnp.clip(r0 - WIN_R // 2, 0, rows - WIN_R), rows - NA_KROWS)
            q0 = pl.multiple_of(r0 * GRID_W, nq)
            k0 = pl.multiple_of(u0 * GRID_W, GRID_W)
            q = q_ref[0, pl.ds(q0, nq), :] * (NA_DH ** -0.5)
            qs = jnp.concatenate([jnp.where(m, q, 0.0) for m in head_mask], axis=0).astype(BF16)
            work.append(dict(q0=q0, k0=k0, lay=lay_ref[p], qs=qs))
        for w in work:
            w["s"] = _nt_dot(w["qs"], k_ref[0, pl.ds(w["k0"], nk), :].astype(BF16))
        for w in work:
            w["sc"] = _nt_dot(w["qs"], kc)
        for w in work:
            s = w["s"] + jnp.concatenate([bias_ref[hh, w["lay"]] for hh in range(NA_PAIR)], axis=0)
            m = jnp.maximum(jnp.max(s, axis=-1, keepdims=True), jnp.max(w["sc"], axis=-1, keepdims=True))
            pw = jnp.exp(s - m)
            pc = jnp.exp(w["sc"] - m)
            w["l"] = jnp.sum(pw, axis=-1, keepdims=True) + jnp.sum(pc, axis=-1, keepdims=True)
            w["pw"] = pw.astype(BF16)
            w["pc"] = pc.astype(BF16)
        for w in work:
            w["o"] = jnp.dot(w["pw"], v_ref[0, pl.ds(w["k0"], nk), :].astype(BF16), preferred_element_type=F32)
        for w in work:
            o = (w["o"] + jnp.dot(w["pc"], vc, preferred_element_type=F32)) / w["l"]
            out = o[0:nq]
            for hh in range(1, NA_PAIR):
                out = jnp.where(head_mask[hh], o[hh * nq:(hh + 1) * nq], out)
            z = z_ref[0, pl.ds(w["q0"], nq), :]
            o_ref[0, pl.ds(w["q0"], nq), :] = (out * (z * jax.nn.sigmoid(z))).astype(o_ref.dtype)
        return carry

    assert (rows // NA_QROWS) % 2 == 0
    lax.fori_loop(0, rows // NA_QROWS // 2, body, 0)


def na_branch(p_main, pc_main, rpb):
    b, t, _ = p_main.shape
    tc = pc_main.shape[1]
    rows = t // GRID_W
    nj = BR_W // LANE
    o_q = MY_OFF["na_qkv"][0] // LANE
    o_z = MY_OFF["na_z"][0] // LANE
    bias, layout_of_pair = na_bias_table(rpb, rows)
    lat = lambda o: pl.BlockSpec((1, t, LANE), lambda i, j: (i, 0, o + j))
    cx = lambda o: pl.BlockSpec((1, tc, LANE), lambda i, j: (i, 0, o + j))
    return pl.pallas_call(
        functools.partial(_na_kernel, rows=rows),
        grid=(b, nj),
        in_specs=[pl.BlockSpec(memory_space=pltpu.SMEM),
                  lat(o_q), lat(o_q + nj), lat(o_q + 2 * nj), cx(o_q + nj), cx(o_q + 2 * nj), lat(o_z),
                  pl.BlockSpec((NA_PAIR,) + bias.shape[1:], lambda i, j: (j, 0, 0, 0))],
        out_specs=pl.BlockSpec((1, t, LANE), lambda i, j: (i, 0, j)),
        out_shape=jax.ShapeDtypeStruct((b, t, BR_W), BF16),
        compiler_params=pltpu.CompilerParams(vmem_limit_bytes=VMEM_LIMIT),
        name="na_attention",
    )(jnp.asarray(layout_of_pair), p_main, p_main, p_main, pc_main, pc_main, p_main, bias)


def _ctx_attn_kernel(q_ref, k_ref, v_ref, z_ref, o_ref):
    tc = q_ref.shape[1]
    lane = lax.broadcasted_iota(jnp.int32, (tc, LANE), 1)
    q = q_ref[0] * (NA_DH ** -0.5)
    k = k_ref[0].astype(BF16)
    v = v_ref[0].astype(BF16)
    out = None
    for hh in range(NA_PAIR):
        mask = (lane >= hh * NA_DH) & (lane < (hh + 1) * NA_DH)
        s = _nt_dot(jnp.where(mask, q, 0.0).astype(BF16), k)
        p = jnp.exp(s - jnp.max(s, axis=-1, keepdims=True))
        o = jnp.dot(p.astype(BF16), v, preferred_element_type=F32) / jnp.sum(p, axis=-1, keepdims=True)
        out = o if out is None else jnp.where(mask, o, out)
    z = z_ref[0]
    o_ref[0] = (out * (z * jax.nn.sigmoid(z))).astype(o_ref.dtype)


def ctx_attention_branch(pc_main):
    b, tc, _ = pc_main.shape
    nj = BR_W // LANE
    o_q = MY_OFF["na_qkv"][0] // LANE
    o_z = MY_OFF["na_z"][0] // LANE
    cx = lambda o: pl.BlockSpec((1, tc, LANE), lambda i, j: (i, 0, o + j))
    return pl.pallas_call(
        _ctx_attn_kernel,
        grid=(b, nj),
        in_specs=[cx(o_q), cx(o_q + nj), cx(o_q + 2 * nj), cx(o_z)],
        out_specs=pl.BlockSpec((1, tc, LANE), lambda i, j: (i, 0, j)),
        out_shape=jax.ShapeDtypeStruct((b, tc, BR_W), BF16),
        compiler_params=pltpu.CompilerParams(vmem_limit_bytes=VMEM_LIMIT),
        name="ctx_attention",
    )(pc_main, pc_main, pc_main, pc_main)


def _hi_lo(x):
    hi = x.astype(BF16)
    return hi, (x - hi.astype(F32)).astype(BF16)


def _dot_hl(a, b):
    a1, a2 = _hi_lo(a)
    b1, b2 = _hi_lo(b)
    return (jnp.dot(a1, b1, preferred_element_type=F32) + jnp.dot(a1, b2, preferred_element_type=F32)
            + jnp.dot(a2, b1, preferred_element_type=F32))


def _tn_dot(a, b):
    return lax.dot_general(a, b, (((0,), (0,)), ((), ())), preferred_element_type=F32)


def _softplus(x):
    return jnp.maximum(x, 0.0) + jnp.log(1.0 + jnp.exp(-jnp.abs(x)))


def _rms_rows(o, g):
    return o * lax.rsqrt(jnp.mean(o * o, axis=-1, keepdims=True) + EPS) * g


GLA_LEVELS = 6
GLA_PAIR = LANE // GLA_DK
GLA_SUM_ROWS = (2 + GLA_LEVELS) * CHUNK


def _gla_tables():
    assert CHUNK == 1 << GLA_LEVELS
    t = np.arange(CHUNK)[:, None]
    r = np.arange(CHUNK)[None, :]
    sums, masks = [], []
    for rev in (False, True):
        before = (r >= t) if rev else (r <= t)
        after = (r < t) if rev else (r > t)
        blocks = [before, after]
        lev_masks = []
        for lv in range(1, GLA_LEVELS + 1):
            seg = CHUNK >> lv
            st, sr = t // seg, r // seg
            q_half = (st % 2 == 0) if rev else (st % 2 == 1)
            k_half = ~q_half
            blocks.append((st == sr) & ((before & q_half) | (after & k_half)))
            lev_masks.append(q_half & (sr == (st + 1 if rev else st - 1)))
        lev_masks.append(r == t)
        sums.append(np.concatenate(blocks, axis=0))
        masks.append(np.stack([np.concatenate([m, m], axis=0) for m in lev_masks]))
    return np.stack(sums).astype(np.float32), np.stack(masks).astype(np.float32)


def _gla_rope_tables(t_len):
    pos = np.arange(t_len)
    row = (pos // GRID_W).astype(np.float64)
    col = (pos % GRID_W).astype(np.float64)
    n = GLA_DK // 4
    freqs = ROPE_THETA ** (-np.arange(n, dtype=np.float64) / n)
    ang = np.concatenate([row[:, None] * freqs, col[:, None] * freqs], axis=-1)
    cos = np.concatenate([np.cos(ang), np.cos(ang)] * GLA_PAIR, axis=-1)
    sin = np.concatenate([-np.sin(ang), np.sin(ang)] * GLA_PAIR, axis=-1)
    return cos.astype(np.float32), sin.astype(np.float32)


def _gla_gate_weights(wg2, bg):
    half = GLA_DK // 2
    lane = np.arange(LANE)
    w_all, b_all = [], []
    for d in range(2):
        wd, bd = [], []
        for j in range(GLA_HEADS // GLA_PAIR):
            cols = half * (GLA_PAIR * j + lane // GLA_DK) + lane % half
            w = jnp.zeros((N_SMALL, LANE), F32).at[MY_OFF["gla_g"][0] + d * GLA_RANK:
                                                    MY_OFF["gla_g"][0] + (d + 1) * GLA_RANK].set(wg2[d][:, cols])
            wd.append(w)
            bd.append(bg[d][cols].reshape(1, LANE))
        w_all.append(jnp.stack(wd))
        b_all.append(jnp.stack(bd))
    return jnp.stack(w_all), jnp.stack(b_all)


def _gla_local(chunks, q_ref, k_ref, v_ref, gs_ref, rope, wg_ref, bgate_ref, sums_ref, masks_ref, qr_s, ke_s, tot_s,
               acc_ref):
    lane = lax.broadcasted_iota(jnp.int32, (CHUNK, LANE), 1)
    first_half = (lane % GLA_DK) < GLA_DK // 2

    def stack_heads(a):
        return jnp.concatenate([jnp.where((lane >= hh * GLA_DK) & (lane < (hh + 1) * GLA_DK), a, 0.0)
                                for hh in range(GLA_PAIR)], axis=0).astype(BF16)

    per_chunk = []
    for c in chunks:
        rows = pl.ds(pl.multiple_of(c * CHUNK, CHUNK), CHUNK)
        q = q_ref[0, rows, :] * (GLA_DK ** -0.5)
        k = k_ref[0, rows, :]
        if rope is not None:
            cos = rope[0][rows, :]
            sin = rope[1][rows, :]

            def rot(x, cos=cos, sin=sin):
                partner = jnp.where(first_half, pltpu.roll(x, LANE - GLA_DK // 2, axis=1),
                                    pltpu.roll(x, GLA_DK // 2, axis=1))
                return x * cos + partner * sin
            q, k = rot(q), rot(k)
        per_chunk.append(dict(c=c, rows=rows, q=q, k=k, k16=k.astype(BF16), q_heads=stack_heads(q),
                              v16=v_ref[0, rows, :].astype(BF16), gs=_hi_lo(gs_ref[0, rows, :])))
    items = [(w, d) for w in per_chunk for d in range(2)]
    wg = [_hi_lo(wg_ref[d, 0]) for d in range(2)]
    xs = [jnp.dot(w["gs"][0], wg[d][0], preferred_element_type=F32) for w, d in items]
    xs = [x + jnp.dot(w["gs"][0], wg[d][1], preferred_element_type=F32) for x, (w, d) in zip(xs, items)]
    xs = [x + jnp.dot(w["gs"][1], wg[d][0], preferred_element_type=F32) + bgate_ref[d, 0] for x, (w, d) in zip(xs, items)]
    gs = [_hi_lo((jnp.minimum(x, 0.0) - jnp.log(1.0 + jnp.exp(-jnp.abs(x)))) * (1.0 / GLA_TAU)) for x in xs]
    main = [jnp.dot(sums_ref[d, 0:2 * CHUNK], g[0], preferred_element_type=F32) for g, (w, d) in zip(gs, items)]
    main = [m + jnp.dot(sums_ref[d, 0:2 * CHUNK], g[1], preferred_element_type=F32)
            for m, g, (w, d) in zip(main, gs, items)]
    levs = [jnp.dot(sums_ref[d, 2 * CHUNK:], g[0], preferred_element_type=F32) for g, (w, d) in zip(gs, items)]
    for m, (w, d) in zip(main, items):
        run = m[0:CHUNK]
        total = run[0:1] if d else run[CHUNK - 1:CHUNK]
        qr_s[d, w["rows"], :] = (w["q"] * jnp.exp(run)).astype(BF16)
        ke_s[d, w["rows"], :] = (w["k"] * jnp.exp(m[CHUNK:])).astype(BF16)
        tot_s[d, pl.ds(w["c"], 1), :] = jnp.exp(total)
    atts = [_nt_dot(w["q_heads"], w["k16"]) * masks_ref[d, GLA_LEVELS] for w, d in items]
    for lv in range(GLA_LEVELS):
        parts = [jnp.exp(lev[lv * CHUNK:(lv + 1) * CHUNK]) for lev in levs]
        atts = [att + _nt_dot(stack_heads(w["q"] * part), (w["k"] * part).astype(BF16)) * masks_ref[d, lv]
                for att, part, (w, d) in zip(atts, parts, items)]
    for n, w in enumerate(per_chunk):
        av = jnp.dot(jnp.concatenate(atts[2 * n:2 * n + 2], axis=0).astype(BF16), w["v16"],
                     preferred_element_type=F32)
        acc_ref[w["rows"], :] = sum(
            jnp.concatenate([av[(GLA_PAIR * d + hh) * CHUNK:(GLA_PAIR * d + hh + 1) * CHUNK,
                                hh * GLA_DV:(hh + 1) * GLA_DV] for hh in range(GLA_PAIR)], axis=1) for d in range(2))


def _gla_recur(chunks, v_ref, qr_s, ke_s, tot_s, st_refs, acc_ref):
    rows = [pl.ds(pl.multiple_of(c * CHUNK, CHUNK), CHUNK) for c in chunks]
    sts = [st_refs[d][...] for d in range(2)]
    inters = [_nt_dot(qr_s[d, rows[d], :], sts[d].astype(BF16)) for d in range(2)]
    upds = [_tn_dot(v_ref[0, rows[d], :].astype(BF16), ke_s[d, rows[d], :]) for d in range(2)]
    srow = lax.broadcasted_iota(jnp.int32, upds[0].shape, 0) // GLA_DV
    scol = lax.broadcasted_iota(jnp.int32, upds[0].shape, 1) // GLA_DK
    for d in range(2):
        st_refs[d][...] = jnp.where(srow == scol, sts[d] * tot_s[d, pl.ds(chunks[d], 1), :] + upds[d], 0.0)
    for d in range(2):
        acc_ref[rows[d], :] += inters[d]


def _gla_finish(acc_ref, z_ref, norm_ref, o_ref, n_rows):
    tile = min(256, n_rows)
    for i in range(n_rows // tile):
        rows = slice(i * tile, (i + 1) * tile)
        o = acc_ref[rows, :]
        z = z_ref[0, rows, :]
        y = jnp.concatenate([_rms_rows(o[:, hh * GLA_DV:(hh + 1) * GLA_DV], norm_ref[...])
                             for hh in range(GLA_PAIR)], axis=1)
        o_ref[0, rows, :] = (y * (z * jax.nn.sigmoid(z))).astype(o_ref.dtype)


def _gla_kernel(q_ref, k_ref, v_ref, z_ref, gs_ref, qc_ref, kc_ref, vc_ref, zc_ref, gsc_ref, cos_ref, sin_ref,
                wg_ref, bgate_ref, norm_ref, sums_ref, masks_ref, o_ref, oc_ref, stf_ref, stb_ref, qr_s, ke_s, tot_s,
                acc_ref):
    st_refs = (stf_ref, stb_ref)
    for r in st_refs:
        r[...] = jnp.zeros_like(r)
    for (qq, kk, vv, gg), rope, zz_ref, out_ref in (((qc_ref, kc_ref, vc_ref, gsc_ref), None, zc_ref, oc_ref),
                                                    ((q_ref, k_ref, v_ref, gs_ref), (cos_ref, sin_ref), z_ref, o_ref)):
        n = qq.shape[1]
        n_chunks = n // CHUNK

        group = next(g for g in (4, 2) if n_chunks % g == 0)

        def local_step(i, carry, qq=qq, kk=kk, vv=vv, gg=gg, rope=rope, group=group):
            _gla_local(tuple(group * i + j for j in range(group)), qq, kk, vv, gg, rope, wg_ref, bgate_ref, sums_ref,
                       masks_ref, qr_s, ke_s, tot_s, acc_ref)
            return carry

        def recur_step(i, carry, vv=vv, n_chunks=n_chunks):
            _gla_recur((i, n_chunks - 1 - i), vv, qr_s, ke_s, tot_s, st_refs, acc_ref)
            return carry

        lax.fori_loop(0, n_chunks // group, local_step, 0)
        lax.fori_loop(0, n_chunks, recur_step, 0, unroll=2)
        _gla_finish(acc_ref, zz_ref, norm_ref, out_ref, n)


def gla_branch(p_main, p_small, pc_main, pc_small, wg2, bg, norm):
    b, t, _ = p_main.shape
    tc = pc_main.shape[1]
    nj = GLA_HEADS // GLA_PAIR
    vw = GLA_PAIR * GLA_DV
    o_q = MY_OFF["gla_q"][0] // LANE
    o_k = MY_OFF["gla_k"][0] // LANE
    o_v = MY_OFF["gla_v"][0] // vw
    o_z = MY_OFF["gla_z"][0] // vw
    sums, masks = _gla_tables()
    cos, sin = _gla_rope_tables(t)
    wge, bge = _gla_gate_weights(wg2.astype(F32), bg.astype(F32))

    def seq(n, w, o):
        return pl.BlockSpec((1, n, w), lambda i, j: (i, 0, o + j))

    def whole(n):
        return pl.BlockSpec((1, n, N_SMALL), lambda i, j: (i, 0, 0))

    def const(shape):
        return pl.BlockSpec(shape, lambda i, j: (0,) * len(shape))

    return pl.pallas_call(
        _gla_kernel,
        grid=(b, nj),
        in_specs=[seq(t, LANE, o_q), seq(t, LANE, o_k), seq(t, vw, o_v), seq(t, vw, o_z), whole(t),
                  seq(tc, LANE, o_q), seq(tc, LANE, o_k), seq(tc, vw, o_v), seq(tc, vw, o_z), whole(tc),
                  const((t, LANE)), const((t, LANE)),
                  pl.BlockSpec((2, 1, N_SMALL, LANE), lambda i, j: (0, j, 0, 0)),
                  pl.BlockSpec((2, 1, 1, LANE), lambda i, j: (0, j, 0, 0)),
                  const((1, GLA_DV)), const((2, GLA_SUM_ROWS, CHUNK)),
                  const((2, GLA_LEVELS + 1, GLA_PAIR * CHUNK, CHUNK))],
        out_specs=[pl.BlockSpec((1, t, vw), lambda i, j: (i, 0, j)), pl.BlockSpec((1, tc, vw), lambda i, j: (i, 0, j))],
        out_shape=[jax.ShapeDtypeStruct((b, t, BR_W), BF16), jax.ShapeDtypeStruct((b, tc, BR_W), BF16)],
        scratch_shapes=[pltpu.VMEM((vw, LANE), F32), pltpu.VMEM((vw, LANE), F32),
                        pltpu.VMEM((2, t, LANE), BF16), pltpu.VMEM((2, t, LANE), BF16),
                        pltpu.VMEM((2, t // CHUNK, LANE), F32),
                        pltpu.VMEM((t, vw), F32)],
        compiler_params=pltpu.CompilerParams(vmem_limit_bytes=VMEM_LIMIT),
        name="gla_scan",
    )(p_main, p_main, p_main, p_main, p_small, pc_main, pc_main, pc_main, pc_main, pc_small,
      jnp.asarray(cos), jnp.asarray(sin), wge, bge, norm.astype(F32).reshape(1, GLA_DV),
      jnp.asarray(sums, BF16), jnp.asarray(masks))


GDN_PAIR = 2 * CHUNK
GDN_STACK = 2 * GDN_PAIR


def _neumann(mats):
    rs = [-a for a in mats]
    ps = list(mats)
    for _ in range(int(math.log2(CHUNK)) - 1):
        p16 = [p.astype(BF16) for p in ps]
        ps = [jnp.dot(x, x, preferred_element_type=F32) for x in p16]
        rs = [r + p + jnp.dot(r.astype(BF16), p.astype(BF16), preferred_element_type=F32) for r, p in zip(rs, ps)]
    return rs


def _seg_scan(x, reverse):
    n = x.shape[0]
    pos = lax.broadcasted_iota(jnp.int32, x.shape, 0) % CHUNK
    s = 1
    while s < CHUNK:
        if reverse:
            x = x + jnp.where(pos < CHUNK - s, pltpu.roll(x, n - s, axis=0), 0.0)
        else:
            x = x + jnp.where(pos >= s, pltpu.roll(x, s, axis=0), 0.0)
        s *= 2
    return x


def _gdn_prepare(q_ref, k_ref, v_ref, gs_ref, wq_ref, wk_ref, wv_ref, sel_ref, coef_ref, qkv_s, gb_s, pad_s):
    n = q_ref.shape[1]
    for i, (x_ref, w_ref) in enumerate(((q_ref, wq_ref), (k_ref, wk_ref), (v_ref, wv_ref))):
        y = _dwconv_rows(x_ref[0], w_ref, GDN_CONV, pad_s)
        y = y * jax.nn.sigmoid(y)
        if i < 2:
            y = y * lax.rsqrt(jnp.sum(y * y, axis=-1, keepdims=True) + EPS)
        if i == 0:
            y = y * (GDN_DH ** -0.5)
        qkv_s[i, 0:n, :] = y
    gs_hi, gs_lo = _hi_lo(gs_ref[0])

    def pick(i):
        return (jnp.dot(gs_hi, sel_ref[0, i], preferred_element_type=F32)
                + jnp.dot(gs_lo, sel_ref[0, i], preferred_element_type=F32))

    for d in range(2):
        a = pick(d)
        bb = pick(2 + d)
        g = -jnp.exp(coef_ref[0, d:d + 1, :]) * _softplus(a + coef_ref[0, 2 + d:3 + d, :])
        run = _seg_scan(g, reverse=(d == 1))
        gb_s[d, 0:n, :] = run
        gb_s[2 + d, 0:n, :] = jax.nn.sigmoid(bb)
        last = 0 if d else CHUNK - 1
        total = run.reshape(n // CHUNK, CHUNK, LANE)[:, last:last + 1, :]
        total = jnp.broadcast_to(total, (n // CHUNK, CHUNK, LANE)).reshape(n, LANE)
        gb_s[4 + d, 0:n, :] = jnp.exp(total - run)


def _gdn_local(pairs, qkv_s, gb_s, negm_ref, x_s, b_s, qp_s, op_s):
    eye = None
    work = []
    for p in pairs:
        rows = pl.ds(pl.multiple_of(p * GDN_PAIR, GDN_PAIR), GDN_PAIR)
        q = qkv_s[0, rows, :]
        k = qkv_s[1, rows, :]
        v = qkv_s[2, rows, :]
        gam = [gb_s[d, rows, :] for d in range(2)]
        beta = [gb_s[2 + d, rows, :] for d in range(2)]
        egam = [jnp.exp(g) for g in gam]
        kb = [k * bt for bt in beta]
        k16 = k.astype(BF16)
        raw = _nt_dot(jnp.concatenate([kb[0], kb[1], q, q], axis=0).astype(BF16),
                      jnp.concatenate([k16, k16], axis=0))
        g2 = jnp.concatenate(gam, axis=0)
        g2t = g2.T
        diff = jnp.concatenate([g2, g2], axis=1) - jnp.concatenate([g2t, g2t], axis=0)
        dec = jnp.exp(jnp.minimum(diff, 0.0) + negm_ref[...])
        if eye is None:
            eye = (lax.broadcasted_iota(jnp.int32, dec.shape, 0)
                   == lax.broadcasted_iota(jnp.int32, dec.shape, 1)).astype(F32)
        rhs = jnp.concatenate([jnp.concatenate([v * beta[d], kb[d] * egam[d]], axis=1) for d in range(2)], axis=0)
        work.append(dict(p=p, rows=rows, q=q, k=k, egam=egam, rhs=rhs,
                         a_kk=raw[0:GDN_STACK] * dec, a_qk=raw[GDN_STACK:] * (dec + eye)))
    for w, r in zip(work, _neumann([w["a_kk"] for w in work])):
        w["sol"] = w["rhs"] + jnp.dot(r.astype(BF16), w["rhs"].astype(BF16), preferred_element_type=F32)
    for w in work:
        p, rows, q, k, egam = w["p"], w["rows"], w["q"], w["k"], w["egam"]
        sol16 = w["sol"].astype(BF16)
        ao = jnp.dot(w["a_qk"].astype(BF16), sol16, preferred_element_type=F32)
        for d in range(2):
            blk = slice(d * GDN_PAIR, (d + 1) * GDN_PAIR)
            op_s[d, rows, :] = ao[blk, 0:GDN_DH]
            qp_s[d, rows, :] = (q * egam[d] - ao[blk, GDN_DH:]).astype(BF16)
            kd = (k * gb_s[4 + d, rows, :]).astype(BF16)
            for jj in range(2):
                sub = slice(jj * CHUNK, (jj + 1) * CHUNK)
                bx = _tn_dot(kd[sub], sol16[d * GDN_PAIR + jj * CHUNK:d * GDN_PAIR + (jj + 1) * CHUNK])
                b_s[d, 2 * p + jj] = bx[:, 0:GDN_DH]
                x_s[d, 2 * p + jj] = bx[:, GDN_DH:].astype(BF16)


def _gdn_recur(chunks, gb_s, x_s, b_s, qp_s, op_s, st_refs, acc_ref):
    rows = [pl.ds(pl.multiple_of(c * CHUNK, CHUNK), CHUNK) for c in chunks]
    sts = [st_refs[d][...] for d in range(2)]
    xs = [jnp.dot(jnp.concatenate([x_s[d, chunks[d]], qp_s[d, rows[d], :]], axis=0), sts[d].astype(BF16),
                  preferred_element_type=F32) for d in range(2)]
    for d, c in enumerate(chunks):
        last = c * CHUNK + (0 if d else CHUNK - 1)
        st_refs[d][...] = sts[d] * jnp.exp(gb_s[d, pl.ds(last, 1), :]) + (b_s[d, c] - xs[d][0:GDN_DH])
    for d in range(2):
        acc_ref[rows[d], :] += xs[d][GDN_DH:] + op_s[d, rows[d], :]


def _gdn_finish(acc_ref, z_ref, norm_ref, o_ref, n_rows):
    tile = min(256, n_rows)
    for i in range(n_rows // tile):
        rows = slice(i * tile, (i + 1) * tile)
        z = z_ref[0, rows, :]
        o_ref[0, rows, :] = (_rms_rows(acc_ref[rows, :], norm_ref[...]) * (z * jax.nn.sigmoid(z))).astype(o_ref.dtype)


def _gdn_kernel(q_ref, k_ref, v_ref, z_ref, gs_ref, qc_ref, kc_ref, vc_ref, zc_ref, gsc_ref, wq_ref, wk_ref, wv_ref,
                sel_ref, coef_ref, norm_ref, negm_ref, o_ref, oc_ref, stf_ref, stb_ref, qkv_s, gb_s,
                x_s, b_s, qp_s, op_s, acc_ref, pad_s):
    st_refs = (stf_ref, stb_ref)
    for r in st_refs:
        r[...] = jnp.zeros_like(r)
    for refs, zz_ref, out_ref in (((qc_ref, kc_ref, vc_ref, gsc_ref), zc_ref, oc_ref),
                                  ((q_ref, k_ref, v_ref, gs_ref), z_ref, o_ref)):
        n = refs[0].shape[1]
        n_chunks = n // CHUNK
        _gdn_prepare(*refs, wq_ref, wk_ref, wv_ref, sel_ref, coef_ref, qkv_s, gb_s, pad_s)
        acc_ref[0:n, :] = jnp.zeros((n, LANE), F32)

        group = next(g for g in (4, 2) if (n // GDN_PAIR) % g == 0)

        def local_step(i, carry, group=group):
            _gdn_local(tuple(group * i + j for j in range(group)), qkv_s, gb_s, negm_ref, x_s, b_s, qp_s, op_s)
            return carry

        def recur_step(i, carry, n_chunks=n_chunks):
            _gdn_recur((i, n_chunks - 1 - i), gb_s, x_s, b_s, qp_s, op_s, st_refs, acc_ref)
            return carry

        lax.fori_loop(0, n // (group * GDN_PAIR), local_step, 0)
        lax.fori_loop(0, n_chunks, recur_step, 0, unroll=2)
        _gdn_finish(acc_ref, zz_ref, norm_ref, out_ref, n)


def _gdn_order_mask():
    ti = np.arange(GDN_STACK)[:, None]
    si = np.arange(GDN_STACK)[None, :]
    same = (ti // CHUNK) == (si // CHUNK)
    earlier = same & np.where(ti < GDN_PAIR, si < ti, si > ti)
    return np.where(earlier, 0.0, NA_NEG).astype(np.float32)


def gdn_branch(p_main, p_small, pc_main, pc_small, conv_w, a_log, dt_bias, norm):
    b, t, _ = p_main.shape
    tc = pc_main.shape[1]
    assert t % (2 * GDN_PAIR) == 0 and tc % (2 * GDN_PAIR) == 0 and tc <= t
    nh = GDN_HEADS
    o_q = MY_OFF["gdn_qkv"][0] // LANE
    o_z = MY_OFF["gdn_z"][0] // LANE
    wt = conv_w.T.astype(F32)
    sel = np.zeros((nh, 4, N_SMALL, LANE), np.float32)
    for h in range(nh):
        for d in range(2):
            sel[h, d, MY_OFF["gdn_a"][0] + d * nh + h, :] = 1.0
            sel[h, 2 + d, MY_OFF["gdn_b"][0] + d * nh + h, :] = 1.0
    coef = jnp.concatenate([a_log.astype(F32).T, dt_bias.astype(F32).T], axis=1)
    coef = jnp.broadcast_to(coef[:, :, None], (nh, 4, LANE))

    def seq(n, o):
        return pl.BlockSpec((1, n, LANE), lambda i, j: (i, 0, o + j))

    def whole(n):
        return pl.BlockSpec((1, n, N_SMALL), lambda i, j: (i, 0, 0))

    def wspec(o):
        return pl.BlockSpec((GDN_CONV, LANE), lambda i, j: (0, o + j))

    return pl.pallas_call(
        _gdn_kernel,
        grid=(b, nh),
        in_specs=[seq(t, o_q), seq(t, o_q + nh), seq(t, o_q + 2 * nh), seq(t, o_z), whole(t),
                  seq(tc, o_q), seq(tc, o_q + nh), seq(tc, o_q + 2 * nh), seq(tc, o_z), whole(tc),
                  wspec(0), wspec(nh), wspec(2 * nh),
                  pl.BlockSpec((1, 4, N_SMALL, LANE), lambda i, j: (j, 0, 0, 0)),
                  pl.BlockSpec((1, 4, LANE), lambda i, j: (j, 0, 0)),
                  pl.BlockSpec((1, GDN_DH), lambda i, j: (0, 0)),
                  pl.BlockSpec((GDN_STACK, GDN_STACK), lambda i, j: (0, 0))],
        out_specs=[pl.BlockSpec((1, t, LANE), lambda i, j: (i, 0, j)), pl.BlockSpec((1, tc, LANE), lambda i, j: (i, 0, j))],
        out_shape=[jax.ShapeDtypeStruct((b, t, BR_W), BF16), jax.ShapeDtypeStruct((b, tc, BR_W), BF16)],
        scratch_shapes=[pltpu.VMEM((GDN_DH, GDN_DH), F32), pltpu.VMEM((GDN_DH, GDN_DH), F32),
                        pltpu.VMEM((3, t, LANE), F32), pltpu.VMEM((6, t, LANE), F32),
                        pltpu.VMEM((2, t // CHUNK, GDN_DH, GDN_DH), BF16),
                        pltpu.VMEM((2, t // CHUNK, GDN_DH, GDN_DH), F32),
                        pltpu.VMEM((2, t, LANE), BF16), pltpu.VMEM((2, t, LANE), F32),
                        pltpu.VMEM((t, LANE), F32),
                        pltpu.VMEM((t + 2 * CONV_HALO, LANE), F32)],
        compiler_params=pltpu.CompilerParams(vmem_limit_bytes=VMEM_LIMIT),
        name="gdn_scan",
    )(p_main, p_main, p_main, p_main, p_small, pc_main, pc_main, pc_main, pc_main, pc_small,
      wt, wt, wt, jnp.asarray(sel, BF16), coef, norm.astype(F32).reshape(1, GDN_DH), jnp.asarray(_gdn_order_mask()))


HY_COLS = 256
HY_ROWS = 512


def dft_matrices(length):
    n = 2 * length
    inner = 64
    assert length % inner == 0
    r = jnp.arange(n, dtype=jnp.int32)
    k = jnp.where(r <= length, r, r - length)[:, None]
    a1 = ((k * (inner * jnp.arange(length // inner, dtype=jnp.int32))[None, :]) % n).astype(F32) * (2.0 * math.pi / n)
    a0 = ((k * jnp.arange(inner, dtype=jnp.int32)[None, :]) % n).astype(F32) * (2.0 * math.pi / n)
    c1, s1 = jnp.cos(a1)[:, :, None], jnp.sin(a1)[:, :, None]
    c0, s0 = jnp.cos(a0)[:, None, :], jnp.sin(a0)[:, None, :]
    is_cos = (r <= length)[:, None, None]
    f = jnp.where(is_cos, c1 * c0 - s1 * s0, -(s1 * c0 + c1 * s0)).reshape(n, length)
    w = (jnp.where((r == 0) | (r == length), 1.0, 2.0) / n)[:, None]
    return f.astype(BF16), (f * w).T.astype(BF16)


def _hy_embedding(length):
    t = np.linspace(0.0, 1.0, length)[:, None]
    bands = (HY_EMB - 1) // 2
    wpos = 2.0 * math.pi * np.arange(length)[:, None] / length
    f = np.linspace(1e-4, bands - 1, bands)[None]
    z = np.concatenate([t, np.cos(f * wpos), -np.sin(f * wpos)], axis=-1)
    z = np.pad(z, ((0, 0), (0, HY_FFN - HY_EMB)))
    deltas = np.abs(np.linspace(math.log(HY_TARGET) / HY_SLOW, math.log(HY_TARGET) / HY_FAST, BR_W))[None]
    return z.astype(np.float32), deltas.astype(np.float32)


def _hy_filter_kernel(z_ref, win_ref, bin_ref, wmid_ref, bmid_ref, freq_ref, wout_ref, delta_ref, o_ref):
    hp = lax.Precision.HIGHEST
    length = z_ref.shape[0]
    h = jnp.sin(freq_ref[0] * (jnp.dot(z_ref[...], win_ref[...], precision=hp, preferred_element_type=F32)
                               + bin_ref[...]))
    for i in range(HY_INNER):
        h = jnp.sin(freq_ref[i + 1] * (jnp.dot(h, wmid_ref[i], precision=hp, preferred_element_type=F32)
                                       + bmid_ref[i]))
    h = jnp.dot(h, wout_ref[...], precision=hp, preferred_element_type=F32)
    t = lax.broadcasted_iota(jnp.int32, (length, BR_W), 0).astype(F32) * (1.0 / (length - 1))
    decay = jnp.exp(-t * delta_ref[...])
    h_f = h[:, :BR_W] * decay
    h_b = h[:, BR_W:] * decay
    for i, a in enumerate((h_f + h_b, h_f - h_b)):
        hi = a.astype(BF16)
        lo = (a - hi.astype(F32)).astype(BF16)
        o_ref[:, (2 * i) * BR_W:(2 * i + 1) * BR_W] = hi
        o_ref[:, (2 * i + 1) * BR_W:(2 * i + 2) * BR_W] = lo


def _hy_spectrum_kernel(f_ref, a_ref, ks_ref, kd_ref, *, length):
    i = pl.program_id(0)
    tm = f_ref.shape[0]
    r = jnp.dot(f_ref[...], a_ref[...], preferred_element_type=F32)
    ks_ref[...] = r[:, :BR_W] + r[:, BR_W:2 * BR_W]
    kd = r[:, 2 * BR_W:3 * BR_W] + r[:, 3 * BR_W:]
    row = lax.broadcasted_iota(jnp.int32, kd.shape, 0) + i * tm
    kd_ref[...] = jnp.where(row == length, 0.0, kd)


def hyena_filter_spectrum(length, fwd, filt):
    w_in, b_in, w_mid, b_mid, freq, w_out = filt
    z, deltas = _hy_embedding(length)
    n = 2 * length
    parts = pl.pallas_call(
        _hy_filter_kernel,
        out_shape=jax.ShapeDtypeStruct((length, 4 * BR_W), BF16),
        compiler_params=pltpu.CompilerParams(vmem_limit_bytes=VMEM_LIMIT),
        name="hyena_filter",
    )(jnp.asarray(z), jnp.pad(w_in, ((0, HY_FFN - HY_EMB), (0, 0))), b_in.reshape(1, HY_FFN), w_mid,
      b_mid.reshape(HY_INNER, 1, HY_FFN), freq.reshape(HY_INNER + 1, 1, HY_FFN), w_out, jnp.asarray(deltas))
    tm = min(HY_ROWS, n)
    return pl.pallas_call(
        functools.partial(_hy_spectrum_kernel, length=length),
        grid=(n // tm,),
        in_specs=[pl.BlockSpec((tm, length), lambda i: (i, 0)),
                  pl.BlockSpec((length, 4 * BR_W), lambda i: (0, 0))],
        out_specs=[pl.BlockSpec((tm, BR_W), lambda i: (i, 0)), pl.BlockSpec((tm, BR_W), lambda i: (i, 0))],
        out_shape=[jax.ShapeDtypeStruct((n, BR_W), F32), jax.ShapeDtypeStruct((n, BR_W), F32)],
        compiler_params=pltpu.CompilerParams(vmem_limit_bytes=VMEM_LIMIT),
        name="hyena_spectrum",
    )(fwd, parts)


CONV_HALO = 8


def _dwconv_rows(x, w_ref, k, pad_ref):
    t = x.shape[0]
    assert k // 2 <= CONV_HALO and t % CONV_HALO == 0
    zeros = jnp.zeros((CONV_HALO, x.shape[1]), F32)
    pad_ref[0:CONV_HALO, :] = zeros
    pad_ref[CONV_HALO + t:2 * CONV_HALO + t, :] = zeros
    pad_ref[CONV_HALO:CONV_HALO + t, :] = x
    out = x * w_ref[k // 2:k // 2 + 1, :]
    for j in range(k):
        off = j - k // 2
        if off:
            out = out + pad_ref[CONV_HALO + off:CONV_HALO + off + t, :] * w_ref[j:j + 1, :]
    return out


def _hy_prep_kernel(x0_ref, x1_ref, v_ref, z_ref, w0_ref, w1_ref, wv_ref, b0_ref, b1_ref, bv_ref, vx_ref, x0g_ref,
                    pad_ref):
    x0 = _dwconv_rows(x0_ref[0], w0_ref, HY_CONV, pad_ref) + b0_ref[...]
    x1 = _dwconv_rows(x1_ref[0], w1_ref, HY_CONV, pad_ref) + b1_ref[...]
    v = _dwconv_rows(v_ref[0], wv_ref, HY_CONV, pad_ref) + bv_ref[...]
    z = z_ref[0]
    vx_ref[0] = v * x1
    x0g_ref[0] = x0 * (z * jax.nn.sigmoid(z))


def hyena_prep(p_main, conv_w, conv_b):
    b, length, _ = p_main.shape
    cb = LANE
    nj = BR_W // cb
    o_xv = MY_OFF["hy_xv"][0] // cb
    o_z = MY_OFF["hy_z"][0] // cb
    wt = conv_w.T
    bt = conv_b.reshape(1, 3 * BR_W)
    dspec = lambda o: pl.BlockSpec((1, length, cb), lambda i, j: (i, 0, o + j))
    wspec = lambda o: pl.BlockSpec((HY_CONV, cb), lambda i, j: (0, o + j))
    bspec = lambda o: pl.BlockSpec((1, cb), lambda i, j: (0, o + j))
    ospec = pl.BlockSpec((1, length, cb), lambda i, j: (i, 0, j))
    return pl.pallas_call(
        _hy_prep_kernel,
        grid=(b, nj),
        in_specs=[dspec(o_xv), dspec(o_xv + nj), dspec(o_xv + 2 * nj), dspec(o_z),
                  wspec(0), wspec(nj), wspec(2 * nj), bspec(0), bspec(nj), bspec(2 * nj)],
        out_specs=[ospec, ospec],
        out_shape=[jax.ShapeDtypeStruct((b, length, BR_W), F32)] * 2,
        scratch_shapes=[pltpu.VMEM((length + 2 * CONV_HALO, cb), F32)],
        compiler_params=pltpu.CompilerParams(vmem_limit_bytes=VMEM_LIMIT),
        name="hyena_prep",
    )(p_main, p_main, p_main, p_main, wt, wt, wt, bt, bt, bt)


def _hy_fwd_kernel(vx_ref, f_ref, kr_ref, ki_ref, knyq_ref, y_ref, *, length):
    v = vx_ref[0].astype(BF16)
    tm = min(HY_ROWS, length)
    for kt in range(length // tm):
        lo = kt * tm
        a = jnp.dot(f_ref[lo:lo + tm, :], v, preferred_element_type=F32)
        bh = jnp.dot(f_ref[length + lo:length + lo + tm, :], v, preferred_element_type=F32)
        kr = kr_ref[lo:lo + tm, :]
        ki = ki_ref[lo:lo + tm, :]
        if kt == 0:
            row = lax.broadcasted_iota(jnp.int32, kr.shape, 0)
            kb = jnp.where(row == 0, knyq_ref[0:1, :], kr)
        else:
            kb = kr
        y_ref[0, lo:lo + tm, :] = (a * kr - bh * ki).astype(BF16)
        y_ref[0, length + lo:length + lo + tm, :] = (a * ki + bh * kb).astype(BF16)


def _hy_inv_kernel(y_ref, inv_ref, vx_ref, x0g_ref, skip_ref, o_ref, *, length):
    tm = min(HY_ROWS, length)
    y = y_ref[0]
    for tt in range(length // tm):
        lo = tt * tm
        conv = jnp.dot(inv_ref[lo:lo + tm, :], y, preferred_element_type=F32)
        o_ref[0, lo:lo + tm, :] = ((conv + skip_ref[...] * vx_ref[0, lo:lo + tm, :])
                                   * x0g_ref[0, lo:lo + tm, :]).astype(o_ref.dtype)


def hyena_branch(p_main, conv_w, conv_b, skip, fwd, inv, ks, kd):
    b, length, _ = p_main.shape
    n = 2 * length
    vx, x0g = hyena_prep(p_main, conv_w, conv_b)
    cb = HY_COLS
    nj = BR_W // cb
    resident = lambda shape: pl.BlockSpec(shape, lambda j, i: (0, 0), pipeline_mode=pl.Buffered(1))
    y = pl.pallas_call(
        functools.partial(_hy_fwd_kernel, length=length),
        grid=(nj, b),
        in_specs=[pl.BlockSpec((1, length, cb), lambda j, i: (i, 0, j)),
                  resident((n, length)),
                  pl.BlockSpec((length, cb), lambda j, i: (0, j)),
                  pl.BlockSpec((length, cb), lambda j, i: (1, j)),
                  pl.BlockSpec((8, cb), lambda j, i: (length // 8, j))],
        out_specs=pl.BlockSpec((1, n, cb), lambda j, i: (i, 0, j)),
        out_shape=jax.ShapeDtypeStruct((b, n, BR_W), BF16),
        compiler_params=pltpu.CompilerParams(vmem_limit_bytes=VMEM_LIMIT),
        name="hyena_dft",
    )(vx, fwd, ks, kd, ks)
    return pl.pallas_call(
        functools.partial(_hy_inv_kernel, length=length),
        grid=(nj, b),
        in_specs=[pl.BlockSpec((1, n, cb), lambda j, i: (i, 0, j)),
                  resident((length, n)),
                  pl.BlockSpec((1, length, cb), lambda j, i: (i, 0, j)),
                  pl.BlockSpec((1, length, cb), lambda j, i: (i, 0, j)),
                  pl.BlockSpec((1, cb), lambda j, i: (0, j))],
        out_specs=pl.BlockSpec((1, length, cb), lambda j, i: (i, 0, j)),
        out_shape=jax.ShapeDtypeStruct((b, length, BR_W), BF16),
        compiler_params=pltpu.CompilerParams(vmem_limit_bytes=VMEM_LIMIT),
        name="hyena_idft",
    )(y, inv, vx, x0g, skip.reshape(1, BR_W))


def mixers(p_main, p_small, pc_main, pc_small, na_rpb, gla_wg2, gla_bg, gla_norm, gdn_conv, gdn_a_log, gdn_dt_bias,
           gdn_norm, hy_conv, hy_conv_b, hy_filt, hy_skip, with_ctx, dft_lat, dft_ctx):
    t = p_main.shape[1]
    ya = na_branch(p_main, pc_main, na_rpb)
    yb, yb_c = gla_branch(p_main, p_small, pc_main, pc_small, gla_wg2, gla_bg, gla_norm)
    yc, yc_c = gdn_branch(p_main, p_small, pc_main, pc_small, gdn_conv, gdn_a_log, gdn_dt_bias, gdn_norm)
    fwd, inv = dft_lat
    ks, kd = hyena_filter_spectrum(t, fwd, hy_filt)
    yd = hyena_branch(p_main, hy_conv, hy_conv_b, hy_skip, fwd, inv, ks, kd)
    if not with_ctx:
        return (ya, yb, yc, yd), None
    ya_c = ctx_attention_branch(pc_main)
    fwd_c, inv_c = dft_ctx
    ks_c, kd_c = hyena_filter_spectrum(pc_main.shape[1], fwd_c, hy_filt)
    yd_c = hyena_branch(pc_main, hy_conv, hy_conv_b, hy_skip, fwd_c, inv_c, ks_c, kd_c)
    return (ya, yb, yc, yd), (ya_c, yb_c, yc_c, yd_c)


def kernel(x, c, ctx, c_ctx, w_mod, b_mod, g_pre, g_post, w_in, w_gate, b_gate, w_branch, w_out, na_rpb, gla_wg2, gla_bg, gla_norm, gdn_conv, gdn_a_log, gdn_dt_bias, gdn_norm, hy_conv, hy_conv_b, hy_w_in, hy_b_in, hy_w_mid, hy_b_mid, hy_freq, hy_w_out, hy_skip):
    b, t, d = x.shape
    tc = ctx.shape[1]
    h = x.reshape(b * t, d)
    hc = ctx.reshape(b * tc, d)
    c_all = jnp.concatenate([c, c_ctx[None], jnp.zeros((7, d), F32)], axis=0)
    dft_lat = dft_matrices(t)
    dft_ctx = dft_matrices(tc)
    for l in range(DEPTH):
        with_ctx = l < DEPTH - 1
        mod = modulation(c_all, w_mod[l], b_mod[l])
        shift, scale, gate = [mod[:, i * d:(i + 1) * d].reshape(-1, 1, d) for i in range(3)]
        u = prenorm_mod(h, g_pre[l], scale[:b], shift[:b], t)
        uc = prenorm_mod(hc, g_pre[l], scale[b:b + 1], shift[b:b + 1], b * tc)
        w_main, w_small = _permute_w_in(w_in[l])
        p_main = matmul(u, w_main, 1024, 1536, name="in_proj").reshape(b, t, N_MAIN)
        p_small = matmul(u, w_small, 1024, N_SMALL, name="in_proj_small").reshape(b, t, N_SMALL)
        pc_main = matmul(uc, w_main, 1024, 1536, name="in_proj_ctx").reshape(b, tc, N_MAIN)
        pc_small = matmul(uc, w_small, 1024, N_SMALL, name="in_proj_ctx_small").reshape(b, tc, N_SMALL)
        ys, ycs = mixers(p_main, p_small, pc_main, pc_small,
                         na_rpb[l], gla_wg2[l], gla_bg[l], gla_norm[l], gdn_conv[l], gdn_a_log[l], gdn_dt_bias[l],
                         gdn_norm[l], hy_conv[l], hy_conv_b[l],
                         (hy_w_in[l], hy_b_in[l], hy_w_mid[l], hy_b_mid[l], hy_freq[l], hy_w_out[l]),
                         hy_skip[l], with_ctx, dft_lat, dft_ctx)
        h_new = merge(u, ys, w_gate[l], b_gate[l], w_branch[l], w_out[l], h, gate[:b], g_post[l], t)
        if with_ctx:
            hc = merge(uc, ycs, w_gate[l], b_gate[l], w_branch[l], w_out[l], hc, gate[b:b + 1], g_post[l], b * tc)
        h = h_new
    return h.reshape(b, t, d)
```

```python
import functools
import math

import numpy as np
import jax
import jax.numpy as jnp
from jax import lax
from jax.experimental import pallas as pl
from jax.experimental.pallas import tpu as pltpu

D_MODEL = 2048
BATCH = 16
SEQ = 2048
DEPTH = 2
GRID_W = 64
CTX_LEN = 256
N_BRANCH = 4
BR_W = D_MODEL // N_BRANCH
NA_HEADS = 8
NA_DH = BR_W // NA_HEADS
WIN_R = 8
WIN_C = 16
QBLK_C = 16
KBLK_C = WIN_C + QBLK_C
N_CBLK = GRID_W // QBLK_C
GLA_HEADS = 4
GLA_DV = BR_W // GLA_HEADS
GLA_DK = GLA_DV // 2
GLA_RANK = 16
GLA_TAU = 16.0
CHUNK = 64
GDN_HEADS = 4
GDN_DH = BR_W // GDN_HEADS
GDN_CONV = 5
HY_CONV = 3
HY_EMB = 33
HY_FFN = 64
HY_INNER = 2
HY_TARGET = 1e-2
HY_FAST = 0.3
HY_SLOW = 1.5
ROPE_THETA = 10000.0
EPS = 1e-6
F32 = jnp.float32
BF16 = jnp.bfloat16

REF_SPLITS = (
    ("na_qkv", 3 * BR_W), ("na_z", BR_W),
    ("gla_q", GLA_HEADS * GLA_DK), ("gla_k", GLA_HEADS * GLA_DK), ("gla_v", BR_W), ("gla_z", BR_W),
    ("gla_g", 2 * GLA_RANK),
    ("gdn_qkv", 3 * BR_W), ("gdn_z", BR_W), ("gdn_a", 2 * GDN_HEADS), ("gdn_b", 2 * GDN_HEADS),
    ("hy_xv", 3 * BR_W), ("hy_z", BR_W),
)
MAIN_ORDER = ("na_qkv", "na_z", "gla_q", "gla_k", "gla_v", "gla_z", "gdn_qkv", "gdn_z", "hy_xv", "hy_z")
SMALL_ORDER = ("gla_g", "gdn_a", "gdn_b")
LANE = 128
N_MAIN = sum(w for n, w in REF_SPLITS if n in MAIN_ORDER)
N_SMALL = LANE
VMEM_LIMIT = 48 * 1024 * 1024

def _ref_offsets():
    out, o = {}, 0
    for name, w in REF_SPLITS:
        out[name] = (o, w)
        o += w
    return out


def _my_offsets():
    out, o = {}, 0
    for name in MAIN_ORDER:
        w = dict(REF_SPLITS)[name]
        out[name] = (o, w)
        o += w
    o = 0
    for name in SMALL_ORDER:
        w = dict(REF_SPLITS)[name]
        out[name] = (o, w)
        o += w
    return out


REF_OFF = _ref_offsets()
MY_OFF = _my_offsets()


def _permute_w_in(w_in_l):
    main = jnp.concatenate([w_in_l[:, REF_OFF[n][0]:REF_OFF[n][0] + REF_OFF[n][1]] for n in MAIN_ORDER], axis=1)
    small = jnp.concatenate([w_in_l[:, REF_OFF[n][0]:REF_OFF[n][0] + REF_OFF[n][1]] for n in SMALL_ORDER], axis=1)
    small = jnp.pad(small, ((0, 0), (0, N_SMALL - small.shape[1])))
    return main.astype(BF16), small.astype(BF16)


def _mod_kernel(c_ref, w_ref, b_ref, o_ref):
    c = c_ref[...]
    a = (c * jax.nn.sigmoid(c)).astype(BF16)
    o_ref[...] = jnp.dot(a, w_ref[...].astype(BF16), preferred_element_type=F32) + b_ref[...]


def modulation(c_all, w_mod_l, b_mod_l):
    r, d = c_all.shape
    n = w_mod_l.shape[1]
    tn = 768
    return pl.pallas_call(
        _mod_kernel,
        grid=(n // tn,),
        in_specs=[pl.BlockSpec((r, d), lambda j: (0, 0)),
                  pl.BlockSpec((d, tn), lambda j: (0, j)),
                  pl.BlockSpec((1, tn), lambda j: (0, j))],
        out_specs=pl.BlockSpec((r, tn), lambda j: (0, j)),
        out_shape=jax.ShapeDtypeStruct((r, n), F32),
        compiler_params=pltpu.CompilerParams(vmem_limit_bytes=VMEM_LIMIT),
        name="modulation",
    )(c_all, w_mod_l, b_mod_l.reshape(1, n))


def _prenorm_kernel(x_ref, g_ref, scale_ref, shift_ref, o_ref):
    x = x_ref[...]
    y = x * lax.rsqrt(jnp.mean(x * x, axis=-1, keepdims=True) + EPS)
    u = (y * g_ref[...]) * (1.0 + scale_ref[0]) + shift_ref[0]
    o_ref[...] = u.astype(o_ref.dtype)


def prenorm_mod(x2d, g_pre_l, scale, shift, rows_per_group, out_dtype=BF16):
    n, d = x2d.shape
    tm = 512
    tiles_per_group = rows_per_group // tm
    return pl.pallas_call(
        _prenorm_kernel,
        grid=(n // tm,),
        in_specs=[pl.BlockSpec((tm, d), lambda i: (i, 0)),
                  pl.BlockSpec((1, d), lambda i: (0, 0)),
                  pl.BlockSpec((1, 1, d), lambda i: (i // tiles_per_group, 0, 0)),
                  pl.BlockSpec((1, 1, d), lambda i: (i // tiles_per_group, 0, 0))],
        out_specs=pl.BlockSpec((tm, d), lambda i: (i, 0)),
        out_shape=jax.ShapeDtypeStruct((n, d), out_dtype),
        compiler_params=pltpu.CompilerParams(vmem_limit_bytes=VMEM_LIMIT),
        name="prenorm_mod",
    )(x2d, g_pre_l.reshape(1, d), scale, shift)


def _matmul_kernel(u_ref, w_ref, o_ref):
    o_ref[...] = jnp.dot(u_ref[...], w_ref[...], preferred_element_type=F32).astype(o_ref.dtype)


def matmul(u, w, tm, tn, out_dtype=F32, name="matmul"):
    n, k = u.shape
    m = w.shape[1]
    return pl.pallas_call(
        _matmul_kernel,
        grid=(m // tn, n // tm),
        in_specs=[pl.BlockSpec((tm, k), lambda j, i: (i, 0)),
                  pl.BlockSpec((k, tn), lambda j, i: (0, j))],
        out_specs=pl.BlockSpec((tm, tn), lambda j, i: (i, j)),
        out_shape=jax.ShapeDtypeStruct((n, m), out_dtype),
        compiler_params=pltpu.CompilerParams(vmem_limit_bytes=VMEM_LIMIT),
        name=name,
    )(u, w)


def _gate_merge_kernel(u_ref, y0_ref, y1_ref, y2_ref, y3_ref, wg_ref, bg_ref, wb_ref, o_ref):
    u = u_ref[...]
    merged = None
    for i, y_ref in enumerate((y0_ref, y1_ref, y2_ref, y3_ref)):
        g = jnp.dot(u, wg_ref[i], preferred_element_type=F32) + bg_ref[i]
        yb = jnp.dot(y_ref[...], wb_ref[i], preferred_element_type=F32)
        term = jax.nn.sigmoid(g) * yb
        merged = term if merged is None else merged + term
    o_ref[...] = merged.astype(o_ref.dtype)


def _out_proj_kernel(m_ref, wo_ref, h_ref, gate_ref, gpost_ref, o_ref):
    y = jnp.dot(m_ref[...], wo_ref[...], preferred_element_type=F32)
    yn = y * lax.rsqrt(jnp.mean(y * y, axis=-1, keepdims=True) + EPS) * gpost_ref[...]
    o_ref[...] = h_ref[...] + gate_ref[0] * yn


def _out_proj_next_kernel(m_ref, wo_ref, h_ref, gate_ref, gpost_ref, gpre_ref, scale_ref, shift_ref, o_ref, u_ref):
    y = jnp.dot(m_ref[...], wo_ref[...], preferred_element_type=F32)
    yn = y * lax.rsqrt(jnp.mean(y * y, axis=-1, keepdims=True) + EPS) * gpost_ref[...]
    hn = h_ref[...] + gate_ref[0] * yn
    o_ref[...] = hn
    xn = hn * lax.rsqrt(jnp.mean(hn * hn, axis=-1, keepdims=True) + EPS)
    u_ref[...] = ((xn * gpre_ref[...]) * (1.0 + scale_ref[0]) + shift_ref[0]).astype(u_ref.dtype)


def merge(u, ys, w_gate_l, b_gate_l, w_branch_l, w_out_l, h2d, gate, g_post_l, rows_per_group, next_norm=None):
    n, d = u.shape
    tm, tn = 1024, 256
    yspec = pl.BlockSpec((tm, BR_W), lambda i, j: (i, 0))
    merged = pl.pallas_call(
        _gate_merge_kernel,
        grid=(n // tm, d // tn),
        in_specs=[pl.BlockSpec((tm, d), lambda i, j: (i, 0)),
                  yspec, yspec, yspec, yspec,
                  pl.BlockSpec((N_BRANCH, d, tn), lambda i, j: (0, 0, j)),
                  pl.BlockSpec((N_BRANCH, 1, tn), lambda i, j: (0, 0, j)),
                  pl.BlockSpec((N_BRANCH, BR_W, tn), lambda i, j: (0, 0, j))],
        out_specs=pl.BlockSpec((tm, tn), lambda i, j: (i, j)),
        out_shape=jax.ShapeDtypeStruct((n, d), BF16),
        compiler_params=pltpu.CompilerParams(vmem_limit_bytes=VMEM_LIMIT),
        name="gate_merge",
    )(u, *[y.reshape(n, BR_W) for y in ys], w_gate_l.astype(BF16), b_gate_l.reshape(N_BRANCH, 1, d),
      w_branch_l.astype(BF16))
    tm = 512
    tiles_per_group = rows_per_group // tm
    tile = pl.BlockSpec((tm, d), lambda i: (i, 0))
    per_group = pl.BlockSpec((1, 1, d), lambda i: (i // tiles_per_group, 0, 0))
    vec = pl.BlockSpec((1, d), lambda i: (0, 0))
    in_specs = [tile, pl.BlockSpec((d, d), lambda i: (0, 0), pipeline_mode=pl.Buffered(1)), tile, per_group, vec]
    args = (merged, w_out_l.astype(BF16), h2d, gate, g_post_l.reshape(1, d))
    if next_norm is None:
        return pl.pallas_call(
            _out_proj_kernel, grid=(n // tm,), in_specs=in_specs, out_specs=tile,
            out_shape=jax.ShapeDtypeStruct((n, d), F32),
            compiler_params=pltpu.CompilerParams(vmem_limit_bytes=VMEM_LIMIT), name="out_proj",
        )(*args)
    g_pre_next, scale_next, shift_next = next_norm
    return pl.pallas_call(
        _out_proj_next_kernel, grid=(n // tm,), in_specs=in_specs + [vec, per_group, per_group],
        out_specs=[tile, tile],
        out_shape=[jax.ShapeDtypeStruct((n, d), F32), jax.ShapeDtypeStruct((n, d), BF16)],
        compiler_params=pltpu.CompilerParams(vmem_limit_bytes=VMEM_LIMIT), name="out_proj_next",
    )(*args, g_pre_next.reshape(1, d), scale_next, shift_next)


NA_NEG = -1e30
NA_PAIR = LANE // NA_DH


NA_QROWS = 2
NA_KROWS = WIN_R + 2


def _na_pair_types(rows):
    assert rows % NA_QROWS == 0 and rows >= NA_KROWS
    layouts, index = [], []
    for r0 in range(0, rows, NA_QROWS):
        rs = [int(np.clip(r0 + i - WIN_R // 2, 0, rows - WIN_R)) for i in range(NA_QROWS)]
        u0 = min(rs[0], rows - NA_KROWS)
        lay = (r0 - u0,) + tuple(x - u0 for x in rs)
        if lay not in layouts:
            layouts.append(lay)
        index.append(layouts.index(lay))
    return layouts, np.asarray(index, np.int32)


def na_bias_table(rpb, rows):
    heads = rpb.shape[0]
    c = np.arange(GRID_W)[:, None]
    kc = np.arange(GRID_W)[None, :]
    cstart = np.clip(c - WIN_C // 2, 0, GRID_W - WIN_C)
    valid = (kc >= cstart) & (kc < cstart + WIN_C)
    dc = np.clip(kc - c + WIN_C - 1, 0, 2 * WIN_C - 2)
    onehot = ((dc[None] == np.arange(2 * WIN_C - 1)[:, None, None]) & valid[None]).astype(np.float32)
    by_dr = jnp.einsum("hrd,dck->hrck", rpb.astype(F32), jnp.asarray(onehot), precision=lax.Precision.HIGHEST)
    by_dr = jnp.where(valid, by_dr, NA_NEG)
    outside = jnp.full((heads, GRID_W, GRID_W), NA_NEG, F32)
    layouts, index = _na_pair_types(rows)
    tabs = []
    for lay in layouts:
        a, starts = lay[0], lay[1:]
        strips = []
        for i, st in enumerate(starts):
            tiles = [by_dr[:, j - a - i + WIN_R - 1] if st <= j < st + WIN_R else outside for j in range(NA_KROWS)]
            strips.append(jnp.concatenate(tiles, axis=-1))
        tabs.append(jnp.concatenate(strips, axis=1))
    return jnp.stack(tabs, axis=1), index


def _nt_dot(a, b):
    return lax.dot_general(a, b, (((1,), (1,)), ((), ())), preferred_element_type=F32)


def _na_kernel(lay_ref, q_ref, k_ref, v_ref, kc_ref, vc_ref, z_ref, bias_ref, o_ref, *, rows):
    nq = NA_QROWS * GRID_W
    nk = NA_KROWS * GRID_W
    lane = lax.broadcasted_iota(jnp.int32, (nq, LANE), 1)
    head_mask = [(lane >= hh * NA_DH) & (lane < (hh + 1) * NA_DH) for hh in range(NA_PAIR)]
    kc = kc_ref[0].astype(BF16)
    vc = vc_ref[0].astype(BF16)

    def body(i, carry):
        work = []
        for p in (2 * i, 2 * i + 1):
            r0 = p * NA_QROWS
            u0 = jnp.minimum(jnp.clip(r0 - WIN_R // 2, 0, rows - WIN_R), rows - NA_KROWS)
            q0 = pl.multiple_of(r0 * GRID_W, nq)
            k0 = pl.multiple_of(u0 * GRID_W, GRID_W)
            q = q_ref[0, pl.ds(q0, nq), :] * (NA_DH ** -0.5)
            qs = jnp.concatenate([jnp.where(m, q, 0.0) for m in head_mask], axis=0).astype(BF16)
            work.append(dict(q0=q0, k0=k0, lay=lay_ref[p], qs=qs))
        for w in work:
            w["s"] = _nt_dot(w["qs"], k_ref[0, pl.ds(w["k0"], nk), :].astype(BF16))
        for w in work:
            w["sc"] = _nt_dot(w["qs"], kc)
        for w in work:
            s = w["s"] + jnp.concatenate([bias_ref[hh, w["lay"]] for hh in range(NA_PAIR)], axis=0)
            m = jnp.maximum(jnp.max(s, axis=-1, keepdims=True), jnp.max(w["sc"], axis=-1, keepdims=True))
            pw = jnp.exp(s - m)
            pc = jnp.exp(w["sc"] - m)
            w["l"] = jnp.sum(pw, axis=-1, keepdims=True) + jnp.sum(pc, axis=-1, keepdims=True)
            w["pw"] = pw.astype(BF16)
            w["pc"] = pc.astype(BF16)
        for w in work:
            w["o"] = jnp.dot(w["pw"], v_ref[0, pl.ds(w["k0"], nk), :].astype(BF16), preferred_element_type=F32)
        for w in work:
            o = (w["o"] + jnp.dot(w["pc"], vc, preferred_element_type=F32)) / w["l"]
            out = o[0:nq]
            for hh in range(1, NA_PAIR):
                out = jnp.where(head_mask[hh], o[hh * nq:(hh + 1) * nq], out)
            z = z_ref[0, pl.ds(w["q0"], nq), :]
            o_ref[0, pl.ds(w["q0"], nq), :] = (out * (z * jax.nn.sigmoid(z))).astype(o_ref.dtype)
        return carry

    assert (rows // NA_QROWS) % 2 == 0
    lax.fori_loop(0, rows // NA_QROWS // 2, body, 0)


def na_branch(p_main, pc_main, rpb):
    b, t, _ = p_main.shape
    tc = pc_main.shape[1]
    rows = t // GRID_W
    nj = BR_W // LANE
    o_q = MY_OFF["na_qkv"][0] // LANE
    o_z = MY_OFF["na_z"][0] // LANE
    bias, layout_of_pair = na_bias_table(rpb, rows)
    lat = lambda o: pl.BlockSpec((1, t, LANE), lambda i, j: (i, 0, o + j))
    cx = lambda o: pl.BlockSpec((1, tc, LANE), lambda i, j: (i, 0, o + j))
    return pl.pallas_call(
        functools.partial(_na_kernel, rows=rows),
        grid=(b, nj),
        in_specs=[pl.BlockSpec(memory_space=pltpu.SMEM),
                  lat(o_q), lat(o_q + nj), lat(o_q + 2 * nj), cx(o_q + nj), cx(o_q + 2 * nj), lat(o_z),
                  pl.BlockSpec((NA_PAIR,) + bias.shape[1:], lambda i, j: (j, 0, 0, 0))],
        out_specs=pl.BlockSpec((1, t, LANE), lambda i, j: (i, 0, j)),
        out_shape=jax.ShapeDtypeStruct((b, t, BR_W), BF16),
        compiler_params=pltpu.CompilerParams(vmem_limit_bytes=VMEM_LIMIT),
        name="na_attention",
    )(jnp.asarray(layout_of_pair), p_main, p_main, p_main, pc_main, pc_main, p_main, bias)


def _ctx_attn_kernel(q_ref, k_ref, v_ref, z_ref, o_ref):
    tc = q_ref.shape[1]
    lane = lax.broadcasted_iota(jnp.int32, (tc, LANE), 1)
    q = q_ref[0] * (NA_DH ** -0.5)
    k = k_ref[0].astype(BF16)
    v = v_ref[0].astype(BF16)
    out = None
    for hh in range(NA_PAIR):
        mask = (lane >= hh * NA_DH) & (lane < (hh + 1) * NA_DH)
        s = _nt_dot(jnp.where(mask, q, 0.0).astype(BF16), k)
        p = jnp.exp(s - jnp.max(s, axis=-1, keepdims=True))
        o = jnp.dot(p.astype(BF16), v, preferred_element_type=F32) / jnp.sum(p, axis=-1, keepdims=True)
        out = o if out is None else jnp.where(mask, o, out)
    z = z_ref[0]
    o_ref[0] = (out * (z * jax.nn.sigmoid(z))).astype(o_ref.dtype)


def ctx_attention_branch(pc_main):
    b, tc, _ = pc_main.shape
    nj = BR_W // LANE
    o_q = MY_OFF["na_qkv"][0] // LANE
    o_z = MY_OFF["na_z"][0] // LANE
    cx = lambda o: pl.BlockSpec((1, tc, LANE), lambda i, j: (i, 0, o + j))
    return pl.pallas_call(
        _ctx_attn_kernel,
        grid=(b, nj),
        in_specs=[cx(o_q), cx(o_q + nj), cx(o_q + 2 * nj), cx(o_z)],
        out_specs=pl.BlockSpec((1, tc, LANE), lambda i, j: (i, 0, j)),
        out_shape=jax.ShapeDtypeStruct((b, tc, BR_W), BF16),
        compiler_params=pltpu.CompilerParams(vmem_limit_bytes=VMEM_LIMIT),
        name="ctx_attention",
    )(pc_main, pc_main, pc_main, pc_main)


def _hi_lo(x):
    hi = x.astype(BF16)
    return hi, (x - hi.astype(F32)).astype(BF16)


def _dot_hl(a, b):
    a1, a2 = _hi_lo(a)
    b1, b2 = _hi_lo(b)
    return (jnp.dot(a1, b1, preferred_element_type=F32) + jnp.dot(a1, b2, preferred_element_type=F32)
            + jnp.dot(a2, b1, preferred_element_type=F32))


def _tn_dot(a, b):
    return lax.dot_general(a, b, (((0,), (0,)), ((), ())), preferred_element_type=F32)


def _softplus(x):
    return jnp.maximum(x, 0.0) + jnp.log(1.0 + jnp.exp(-jnp.abs(x)))


def _rms_rows(o, g):
    return o * lax.rsqrt(jnp.mean(o * o, axis=-1, keepdims=True) + EPS) * g


GLA_LEVELS = 6
GLA_PAIR = LANE // GLA_DK
GLA_SUM_ROWS = (2 + GLA_LEVELS) * CHUNK


def _gla_tables():
    assert CHUNK == 1 << GLA_LEVELS
    t = np.arange(CHUNK)[:, None]
    r = np.arange(CHUNK)[None, :]
    sums, masks = [], []
    for rev in (False, True):
        before = (r >= t) if rev else (r <= t)
        after = (r < t) if rev else (r > t)
        blocks = [before, after]
        lev_masks = []
        for lv in range(1, GLA_LEVELS + 1):
            seg = CHUNK >> lv
            st, sr = t // seg, r // seg
            q_half = (st % 2 == 0) if rev else (st % 2 == 1)
            k_half = ~q_half
            blocks.append((st == sr) & ((before & q_half) | (after & k_half)))
            lev_masks.append(q_half & (sr == (st + 1 if rev else st - 1)))
        lev_masks.append(r == t)
        sums.append(np.concatenate(blocks, axis=0))
        masks.append(np.stack([np.concatenate([m, m], axis=0) for m in lev_masks]))
    return np.stack(sums).astype(np.float32), np.stack(masks).astype(np.float32)


def _gla_rope_tables(t_len):
    pos = np.arange(t_len)
    row = (pos // GRID_W).astype(np.float64)
    col = (pos % GRID_W).astype(np.float64)
    n = GLA_DK // 4
    freqs = ROPE_THETA ** (-np.arange(n, dtype=np.float64) / n)
    ang = np.concatenate([row[:, None] * freqs, col[:, None] * freqs], axis=-1)
    cos = np.concatenate([np.cos(ang), np.cos(ang)] * GLA_PAIR, axis=-1)
    sin = np.concatenate([-np.sin(ang), np.sin(ang)] * GLA_PAIR, axis=-1)
    return cos.astype(np.float32), sin.astype(np.float32)


def _gla_gate_weights(wg2, bg):
    half = GLA_DK // 2
    lane = np.arange(LANE)
    w_all, b_all = [], []
    for d in range(2):
        wd, bd = [], []
        for j in range(GLA_HEADS // GLA_PAIR):
            cols = half * (GLA_PAIR * j + lane // GLA_DK) + lane % half
            w = jnp.zeros((N_SMALL, LANE), F32).at[MY_OFF["gla_g"][0] + d * GLA_RANK:
                                                    MY_OFF["gla_g"][0] + (d + 1) * GLA_RANK].set(wg2[d][:, cols])
            wd.append(w)
            bd.append(bg[d][cols].reshape(1, LANE))
        w_all.append(jnp.stack(wd))
        b_all.append(jnp.stack(bd))
    return jnp.stack(w_all), jnp.stack(b_all)


def _gla_local(chunks, q_ref, k_ref, v_ref, gs_ref, rope, wg_ref, bgate_ref, sums_ref, masks_ref, qr_s, ke_s, tot_s,
               acc_ref):
    lane = lax.broadcasted_iota(jnp.int32, (CHUNK, LANE), 1)
    first_half = (lane % GLA_DK) < GLA_DK // 2

    def stack_heads(a):
        return jnp.concatenate([jnp.where((lane >= hh * GLA_DK) & (lane < (hh + 1) * GLA_DK), a, 0.0)
                                for hh in range(GLA_PAIR)], axis=0).astype(BF16)

    per_chunk = []
    for c in chunks:
        rows = pl.ds(pl.multiple_of(c * CHUNK, CHUNK), CHUNK)
        q = q_ref[0, rows, :] * (GLA_DK ** -0.5)
        k = k_ref[0, rows, :]
        if rope is not None:
            cos = rope[0][rows, :]
            sin = rope[1][rows, :]

            def rot(x, cos=cos, sin=sin):
                partner = jnp.where(first_half, pltpu.roll(x, LANE - GLA_DK // 2, axis=1),
                                    pltpu.roll(x, GLA_DK // 2, axis=1))
                return x * cos + partner * sin
            q, k = rot(q), rot(k)
        per_chunk.append(dict(c=c, rows=rows, q=q, k=k, k16=k.astype(BF16), q_heads=stack_heads(q),
                              v16=v_ref[0, rows, :].astype(BF16), gs=_hi_lo(gs_ref[0, rows, :])))
    items = [(w, d) for w in per_chunk for d in range(2)]
    wg = [_hi_lo(wg_ref[d, 0]) for d in range(2)]
    xs = [jnp.dot(w["gs"][0], wg[d][0], preferred_element_type=F32) for w, d in items]
    xs = [x + jnp.dot(w["gs"][0], wg[d][1], preferred_element_type=F32) for x, (w, d) in zip(xs, items)]
    xs = [x + jnp.dot(w["gs"][1], wg[d][0], preferred_element_type=F32) + bgate_ref[d, 0] for x, (w, d) in zip(xs, items)]
    gs = [_hi_lo((jnp.minimum(x, 0.0) - jnp.log(1.0 + jnp.exp(-jnp.abs(x)))) * (1.0 / GLA_TAU)) for x in xs]
    main = [jnp.dot(sums_ref[d, 0:2 * CHUNK], g[0], preferred_element_type=F32) for g, (w, d) in zip(gs, items)]
    main = [m + jnp.dot(sums_ref[d, 0:2 * CHUNK], g[1], preferred_element_type=F32)
            for m, g, (w, d) in zip(main, gs, items)]
    levs = [jnp.dot(sums_ref[d, 2 * CHUNK:], g[0], preferred_element_type=F32) for g, (w, d) in zip(gs, items)]
    for m, (w, d) in zip(main, items):
        run = m[0:CHUNK]
        total = run[0:1] if d else run[CHUNK - 1:CHUNK]
        qr_s[d, w["rows"], :] = (w["q"] * jnp.exp(run)).astype(BF16)
        ke_s[d, w["rows"], :] = (w["k"] * jnp.exp(m[CHUNK:])).astype(BF16)
        tot_s[d, pl.ds(w["c"], 1), :] = jnp.exp(total)
    atts = [_nt_dot(w["q_heads"], w["k16"]) * masks_ref[d, GLA_LEVELS] for w, d in items]
    for lv in range(GLA_LEVELS):
        parts = [jnp.exp(lev[lv * CHUNK:(lv + 1) * CHUNK]) for lev in levs]
        atts = [att + _nt_dot(stack_heads(w["q"] * part), (w["k"] * part).astype(BF16)) * masks_ref[d, lv]
                for att, part, (w, d) in zip(atts, parts, items)]
    for n, w in enumerate(per_chunk):
        av = jnp.dot(jnp.concatenate(atts[2 * n:2 * n + 2], axis=0).astype(BF16), w["v16"],
                     preferred_element_type=F32)
        acc_ref[w["rows"], :] = sum(
            jnp.concatenate([av[(GLA_PAIR * d + hh) * CHUNK:(GLA_PAIR * d + hh + 1) * CHUNK,
                                hh * GLA_DV:(hh + 1) * GLA_DV] for hh in range(GLA_PAIR)], axis=1) for d in range(2))


def _gla_recur(chunks, v_ref, qr_s, ke_s, tot_s, st_refs, acc_ref):
    rows = [pl.ds(pl.multiple_of(c * CHUNK, CHUNK), CHUNK) for c in chunks]
    sts = [st_refs[d][...] for d in range(2)]
    inters = [_nt_dot(qr_s[d, rows[d], :], sts[d].astype(BF16)) for d in range(2)]
    upds = [_tn_dot(v_ref[0, rows[d], :].astype(BF16), ke_s[d, rows[d], :]) for d in range(2)]
    srow = lax.broadcasted_iota(jnp.int32, upds[0].shape, 0) // GLA_DV
    scol = lax.broadcasted_iota(jnp.int32, upds[0].shape, 1) // GLA_DK
    for d in range(2):
        st_refs[d][...] = jnp.where(srow == scol, sts[d] * tot_s[d, pl.ds(chunks[d], 1), :] + upds[d], 0.0)
    for d in range(2):
        acc_ref[rows[d], :] += inters[d]


def _gla_finish(acc_ref, z_ref, norm_ref, o_ref, n_rows):
    tile = min(256, n_rows)
    for i in range(n_rows // tile):
        rows = slice(i * tile, (i + 1) * tile)
        o = acc_ref[rows, :]
        z = z_ref[0, rows, :]
        y = jnp.concatenate([_rms_rows(o[:, hh * GLA_DV:(hh + 1) * GLA_DV], norm_ref[...])
                             for hh in range(GLA_PAIR)], axis=1)
        o_ref[0, rows, :] = (y * (z * jax.nn.sigmoid(z))).astype(o_ref.dtype)


def _gla_kernel(q_ref, k_ref, v_ref, z_ref, gs_ref, qc_ref, kc_ref, vc_ref, zc_ref, gsc_ref, cos_ref, sin_ref,
                wg_ref, bgate_ref, norm_ref, sums_ref, masks_ref, o_ref, oc_ref, stf_ref, stb_ref, qr_s, ke_s, tot_s,
                acc_ref):
    st_refs = (stf_ref, stb_ref)
    for r in st_refs:
        r[...] = jnp.zeros_like(r)
    for (qq, kk, vv, gg), rope, zz_ref, out_ref in (((qc_ref, kc_ref, vc_ref, gsc_ref), None, zc_ref, oc_ref),
                                                    ((q_ref, k_ref, v_ref, gs_ref), (cos_ref, sin_ref), z_ref, o_ref)):
        n = qq.shape[1]
        n_chunks = n // CHUNK

        group = next(g for g in (4, 2) if n_chunks % g == 0)

        def local_step(i, carry, qq=qq, kk=kk, vv=vv, gg=gg, rope=rope, group=group):
            _gla_local(tuple(group * i + j for j in range(group)), qq, kk, vv, gg, rope, wg_ref, bgate_ref, sums_ref,
                       masks_ref, qr_s, ke_s, tot_s, acc_ref)
            return carry

        def recur_step(i, carry, vv=vv, n_chunks=n_chunks):
            _gla_recur((i, n_chunks - 1 - i), vv, qr_s, ke_s, tot_s, st_refs, acc_ref)
            return carry

        lax.fori_loop(0, n_chunks // group, local_step, 0)
        lax.fori_loop(0, n_chunks, recur_step, 0, unroll=2)
        _gla_finish(acc_ref, zz_ref, norm_ref, out_ref, n)


def gla_branch(p_main, p_small, pc_main, pc_small, wg2, bg, norm):
    b, t, _ = p_main.shape
    tc = pc_main.shape[1]
    nj = GLA_HEADS // GLA_PAIR
    vw = GLA_PAIR * GLA_DV
    o_q = MY_OFF["gla_q"][0] // LANE
    o_k = MY_OFF["gla_k"][0] // LANE
    o_v = MY_OFF["gla_v"][0] // vw
    o_z = MY_OFF["gla_z"][0] // vw
    sums, masks = _gla_tables()
    cos, sin = _gla_rope_tables(t)
    wge, bge = _gla_gate_weights(wg2.astype(F32), bg.astype(F32))

    def seq(n, w, o):
        return pl.BlockSpec((1, n, w), lambda i, j: (i, 0, o + j))

    def whole(n):
        return pl.BlockSpec((1, n, N_SMALL), lambda i, j: (i, 0, 0))

    def const(shape):
        return pl.BlockSpec(shape, lambda i, j: (0,) * len(shape))

    return pl.pallas_call(
        _gla_kernel,
        grid=(b, nj),
        in_specs=[seq(t, LANE, o_q), seq(t, LANE, o_k), seq(t, vw, o_v), seq(t, vw, o_z), whole(t),
                  seq(tc, LANE, o_q), seq(tc, LANE, o_k), seq(tc, vw, o_v), seq(tc, vw, o_z), whole(tc),
                  const((t, LANE)), const((t, LANE)),
                  pl.BlockSpec((2, 1, N_SMALL, LANE), lambda i, j: (0, j, 0, 0)),
                  pl.BlockSpec((2, 1, 1, LANE), lambda i, j: (0, j, 0, 0)),
                  const((1, GLA_DV)), const((2, GLA_SUM_ROWS, CHUNK)),
                  const((2, GLA_LEVELS + 1, GLA_PAIR * CHUNK, CHUNK))],
        out_specs=[pl.BlockSpec((1, t, vw), lambda i, j: (i, 0, j)), pl.BlockSpec((1, tc, vw), lambda i, j: (i, 0, j))],
        out_shape=[jax.ShapeDtypeStruct((b, t, BR_W), BF16), jax.ShapeDtypeStruct((b, tc, BR_W), BF16)],
        scratch_shapes=[pltpu.VMEM((vw, LANE), F32), pltpu.VMEM((vw, LANE), F32),
                        pltpu.VMEM((2, t, LANE), BF16), pltpu.VMEM((2, t, LANE), BF16),
                        pltpu.VMEM((2, t // CHUNK, LANE), F32),
                        pltpu.VMEM((t, vw), F32)],
        compiler_params=pltpu.CompilerParams(vmem_limit_bytes=VMEM_LIMIT),
        name="gla_scan",
    )(p_main, p_main, p_main, p_main, p_small, pc_main, pc_main, pc_main, pc_main, pc_small,
      jnp.asarray(cos), jnp.asarray(sin), wge, bge, norm.astype(F32).reshape(1, GLA_DV),
      jnp.asarray(sums, BF16), jnp.asarray(masks))


GDN_PAIR = 2 * CHUNK
GDN_STACK = 2 * GDN_PAIR


def _neumann(mats):
    rs = [-a for a in mats]
    ps = list(mats)
    for _ in range(int(math.log2(CHUNK)) - 1):
        p16 = [p.astype(BF16) for p in ps]
        ps = [jnp.dot(x, x, preferred_element_type=F32) for x in p16]
        rs = [r + p + jnp.dot(r.astype(BF16), p.astype(BF16), preferred_element_type=F32) for r, p in zip(rs, ps)]
    return rs


def _seg_scan(x, reverse):
    n = x.shape[0]
    pos = lax.broadcasted_iota(jnp.int32, x.shape, 0) % CHUNK
    s = 1
    while s < CHUNK:
        if reverse:
            x = x + jnp.where(pos < CHUNK - s, pltpu.roll(x, n - s, axis=0), 0.0)
        else:
            x = x + jnp.where(pos >= s, pltpu.roll(x, s, axis=0), 0.0)
        s *= 2
    return x


def _gdn_prepare(q_ref, k_ref, v_ref, gs_ref, wq_ref, wk_ref, wv_ref, sel_ref, coef_ref, qkv_s, gb_s, pad_s):
    n = q_ref.shape[1]
    for i, (x_ref, w_ref) in enumerate(((q_ref, wq_ref), (k_ref, wk_ref), (v_ref, wv_ref))):
        y = _dwconv_rows(x_ref[0], w_ref, GDN_CONV, pad_s)
        y = y * jax.nn.sigmoid(y)
        if i < 2:
            y = y * lax.rsqrt(jnp.sum(y * y, axis=-1, keepdims=True) + EPS)
        if i == 0:
            y = y * (GDN_DH ** -0.5)
        qkv_s[i, 0:n, :] = y
    gs_hi, gs_lo = _hi_lo(gs_ref[0])

    def pick(i):
        return (jnp.dot(gs_hi, sel_ref[0, i], preferred_element_type=F32)
                + jnp.dot(gs_lo, sel_ref[0, i], preferred_element_type=F32))

    for d in range(2):
        a = pick(d)
        bb = pick(2 + d)
        g = -jnp.exp(coef_ref[0, d:d + 1, :]) * _softplus(a + coef_ref[0, 2 + d:3 + d, :])
        run = _seg_scan(g, reverse=(d == 1))
        gb_s[d, 0:n, :] = run
        gb_s[2 + d, 0:n, :] = jax.nn.sigmoid(bb)
        last = 0 if d else CHUNK - 1
        total = run.reshape(n // CHUNK, CHUNK, LANE)[:, last:last + 1, :]
        total = jnp.broadcast_to(total, (n // CHUNK, CHUNK, LANE)).reshape(n, LANE)
        gb_s[4 + d, 0:n, :] = jnp.exp(total - run)


def _gdn_local(pairs, qkv_s, gb_s, negm_ref, x_s, b_s, qp_s, op_s):
    eye = None
    work = []
    for p in pairs:
        rows = pl.ds(pl.multiple_of(p * GDN_PAIR, GDN_PAIR), GDN_PAIR)
        q = qkv_s[0, rows, :]
        k = qkv_s[1, rows, :]
        v = qkv_s[2, rows, :]
        gam = [gb_s[d, rows, :] for d in range(2)]
        beta = [gb_s[2 + d, rows, :] for d in range(2)]
        egam = [jnp.exp(g) for g in gam]
        kb = [k * bt for bt in beta]
        k16 = k.astype(BF16)
        raw = _nt_dot(jnp.concatenate([kb[0], kb[1], q, q], axis=0).astype(BF16),
                      jnp.concatenate([k16, k16], axis=0))
        g2 = jnp.concatenate(gam, axis=0)
        g2t = g2.T
        diff = jnp.concatenate([g2, g2], axis=1) - jnp.concatenate([g2t, g2t], axis=0)
        dec = jnp.exp(jnp.minimum(diff, 0.0) + negm_ref[...])
        if eye is None:
            eye = (lax.broadcasted_iota(jnp.int32, dec.shape, 0)
                   == lax.broadcasted_iota(jnp.int32, dec.shape, 1)).astype(F32)
        rhs = jnp.concatenate([jnp.concatenate([v * beta[d], kb[d] * egam[d]], axis=1) for d in range(2)], axis=0)
        work.append(dict(p=p, rows=rows, q=q, k=k, egam=egam, rhs=rhs,
                         a_kk=raw[0:GDN_STACK] * dec, a_qk=raw[GDN_STACK:] * (dec + eye)))
    for w, r in zip(work, _neumann([w["a_kk"] for w in work])):
        w["sol"] = w["rhs"] + jnp.dot(r.astype(BF16), w["rhs"].astype(BF16), preferred_element_type=F32)
    for w in work:
        p, rows, q, k, egam = w["p"], w["rows"], w["q"], w["k"], w["egam"]
        sol16 = w["sol"].astype(BF16)
        ao = jnp.dot(w["a_qk"].astype(BF16), sol16, preferred_element_type=F32)
        for d in range(2):
            blk = slice(d * GDN_PAIR, (d + 1) * GDN_PAIR)
            op_s[d, rows, :] = ao[blk, 0:GDN_DH]
            qp_s[d, rows, :] = (q * egam[d] - ao[blk, GDN_DH:]).astype(BF16)
            kd = (k * gb_s[4 + d, rows, :]).astype(BF16)
            for jj in range(2):
                sub = slice(jj * CHUNK, (jj + 1) * CHUNK)
                bx = _tn_dot(kd[sub], sol16[d * GDN_PAIR + jj * CHUNK:d * GDN_PAIR + (jj + 1) * CHUNK])
                b_s[d, 2 * p + jj] = bx[:, 0:GDN_DH]
                x_s[d, 2 * p + jj] = bx[:, GDN_DH:].astype(BF16)


def _gdn_recur(chunks, gb_s, x_s, b_s, qp_s, op_s, st_refs, acc_ref):
    rows = [pl.ds(pl.multiple_of(c * CHUNK, CHUNK), CHUNK) for c in chunks]
    sts = [st_refs[d][...] for d in range(2)]
    xs = [jnp.dot(jnp.concatenate([x_s[d, chunks[d]], qp_s[d, rows[d], :]], axis=0), sts[d].astype(BF16),
                  preferred_element_type=F32) for d in range(2)]
    for d, c in enumerate(chunks):
        last = c * CHUNK + (0 if d else CHUNK - 1)
        st_refs[d][...] = sts[d] * jnp.exp(gb_s[d, pl.ds(last, 1), :]) + (b_s[d, c] - xs[d][0:GDN_DH])
    for d in range(2):
        acc_ref[rows[d], :] += xs[d][GDN_DH:] + op_s[d, rows[d], :]


def _gdn_finish(acc_ref, z_ref, norm_ref, o_ref, n_rows):
    tile = min(256, n_rows)
    for i in range(n_rows // tile):
        rows = slice(i * tile, (i + 1) * tile)
        z = z_ref[0, rows, :]
        o_ref[0, rows, :] = (_rms_rows(acc_ref[rows, :], norm_ref[...]) * (z * jax.nn.sigmoid(z))).astype(o_ref.dtype)


def _gdn_kernel(q_ref, k_ref, v_ref, z_ref, gs_ref, qc_ref, kc_ref, vc_ref, zc_ref, gsc_ref, wq_ref, wk_ref, wv_ref,
                sel_ref, coef_ref, norm_ref, negm_ref, o_ref, oc_ref, stf_ref, stb_ref, qkv_s, gb_s,
                x_s, b_s, qp_s, op_s, acc_ref, pad_s):
    st_refs = (stf_ref, stb_ref)
    for r in st_refs:
        r[...] = jnp.zeros_like(r)
    for refs, zz_ref, out_ref in (((qc_ref, kc_ref, vc_ref, gsc_ref), zc_ref, oc_ref),
                                  ((q_ref, k_ref, v_ref, gs_ref), z_ref, o_ref)):
        n = refs[0].shape[1]
        n_chunks = n // CHUNK
        _gdn_prepare(*refs, wq_ref, wk_ref, wv_ref, sel_ref, coef_ref, qkv_s, gb_s, pad_s)
        acc_ref[0:n, :] = jnp.zeros((n, LANE), F32)

        group = next(g for g in (4, 2) if (n // GDN_PAIR) % g == 0)

        def local_step(i, carry, group=group):
            _gdn_local(tuple(group * i + j for j in range(group)), qkv_s, gb_s, negm_ref, x_s, b_s, qp_s, op_s)
            return carry

        def recur_step(i, carry, n_chunks=n_chunks):
            _gdn_recur((i, n_chunks - 1 - i), gb_s, x_s, b_s, qp_s, op_s, st_refs, acc_ref)
            return carry

        lax.fori_loop(0, n // (group * GDN_PAIR), local_step, 0)
        lax.fori_loop(0, n_chunks, recur_step, 0, unroll=2)
        _gdn_finish(acc_ref, zz_ref, norm_ref, out_ref, n)


def _gdn_order_mask():
    ti = np.arange(GDN_STACK)[:, None]
    si = np.arange(GDN_STACK)[None, :]
    same = (ti // CHUNK) == (si // CHUNK)
    earlier = same & np.where(ti < GDN_PAIR, si < ti, si > ti)
    return np.where(earlier, 0.0, NA_NEG).astype(np.float32)


def gdn_branch(p_main, p_small, pc_main, pc_small, conv_w, a_log, dt_bias, norm):
    b, t, _ = p_main.shape
    tc = pc_main.shape[1]
    assert t % (2 * GDN_PAIR) == 0 and tc % (2 * GDN_PAIR) == 0 and tc <= t
    nh = GDN_HEADS
    o_q = MY_OFF["gdn_qkv"][0] // LANE
    o_z = MY_OFF["gdn_z"][0] // LANE
    wt = conv_w.T.astype(F32)
    sel = np.zeros((nh, 4, N_SMALL, LANE), np.float32)
    for h in range(nh):
        for d in range(2):
            sel[h, d, MY_OFF["gdn_a"][0] + d * nh + h, :] = 1.0
            sel[h, 2 + d, MY_OFF["gdn_b"][0] + d * nh + h, :] = 1.0
    coef = jnp.concatenate([a_log.astype(F32).T, dt_bias.astype(F32).T], axis=1)
    coef = jnp.broadcast_to(coef[:, :, None], (nh, 4, LANE))

    def seq(n, o):
        return pl.BlockSpec((1, n, LANE), lambda i, j: (i, 0, o + j))

    def whole(n):
        return pl.BlockSpec((1, n, N_SMALL), lambda i, j: (i, 0, 0))

    def wspec(o):
        return pl.BlockSpec((GDN_CONV, LANE), lambda i, j: (0, o + j))

    return pl.pallas_call(
        _gdn_kernel,
        grid=(b, nh),
        in_specs=[seq(t, o_q), seq(t, o_q + nh), seq(t, o_q + 2 * nh), seq(t, o_z), whole(t),
                  seq(tc, o_q), seq(tc, o_q + nh), seq(tc, o_q + 2 * nh), seq(tc, o_z), whole(tc),
                  wspec(0), wspec(nh), wspec(2 * nh),
                  pl.BlockSpec((1, 4, N_SMALL, LANE), lambda i, j: (j, 0, 0, 0)),
                  pl.BlockSpec((1, 4, LANE), lambda i, j: (j, 0, 0)),
                  pl.BlockSpec((1, GDN_DH), lambda i, j: (0, 0)),
                  pl.BlockSpec((GDN_STACK, GDN_STACK), lambda i, j: (0, 0))],
        out_specs=[pl.BlockSpec((1, t, LANE), lambda i, j: (i, 0, j)), pl.BlockSpec((1, tc, LANE), lambda i, j: (i, 0, j))],
        out_shape=[jax.ShapeDtypeStruct((b, t, BR_W), BF16), jax.ShapeDtypeStruct((b, tc, BR_W), BF16)],
        scratch_shapes=[pltpu.VMEM((GDN_DH, GDN_DH), F32), pltpu.VMEM((GDN_DH, GDN_DH), F32),
                        pltpu.VMEM((3, t, LANE), F32), pltpu.VMEM((6, t, LANE), F32),
                        pltpu.VMEM((2, t // CHUNK, GDN_DH, GDN_DH), BF16),
                        pltpu.VMEM((2, t // CHUNK, GDN_DH, GDN_DH), F32),
                        pltpu.VMEM((2, t, LANE), BF16), pltpu.VMEM((2, t, LANE), F32),
                        pltpu.VMEM((t, LANE), F32),
                        pltpu.VMEM((t + 2 * CONV_HALO, LANE), F32)],
        compiler_params=pltpu.CompilerParams(vmem_limit_bytes=VMEM_LIMIT),
        name="gdn_scan",
    )(p_main, p_main, p_main, p_main, p_small, pc_main, pc_main, pc_main, pc_main, pc_small,
      wt, wt, wt, jnp.asarray(sel, BF16), coef, norm.astype(F32).reshape(1, GDN_DH), jnp.asarray(_gdn_order_mask()))


HY_COLS = 256
HY_ROWS = 512


def dft_matrices(length):
    n = 2 * length
    inner = 64
    assert length % inner == 0
    r = jnp.arange(n, dtype=jnp.int32)
    k = jnp.where(r <= length, r, r - length)[:, None]
    a1 = ((k * (inner * jnp.arange(length // inner, dtype=jnp.int32))[None, :]) % n).astype(F32) * (2.0 * math.pi / n)
    a0 = ((k * jnp.arange(inner, dtype=jnp.int32)[None, :]) % n).astype(F32) * (2.0 * math.pi / n)
    c1, s1 = jnp.cos(a1)[:, :, None], jnp.sin(a1)[:, :, None]
    c0, s0 = jnp.cos(a0)[:, None, :], jnp.sin(a0)[:, None, :]
    is_cos = (r <= length)[:, None, None]
    f = jnp.where(is_cos, c1 * c0 - s1 * s0, -(s1 * c0 + c1 * s0)).reshape(n, length)
    w = (jnp.where((r == 0) | (r == length), 1.0, 2.0) / n)[:, None]
    return f.astype(BF16), (f * w).T.astype(BF16)


def _hy_embedding(length):
    t = np.linspace(0.0, 1.0, length)[:, None]
    bands = (HY_EMB - 1) // 2
    wpos = 2.0 * math.pi * np.arange(length)[:, None] / length
    f = np.linspace(1e-4, bands - 1, bands)[None]
    z = np.concatenate([t, np.cos(f * wpos), -np.sin(f * wpos)], axis=-1)
    z = np.pad(z, ((0, 0), (0, HY_FFN - HY_EMB)))
    deltas = np.abs(np.linspace(math.log(HY_TARGET) / HY_SLOW, math.log(HY_TARGET) / HY_FAST, BR_W))[None]
    return z.astype(np.float32), deltas.astype(np.float32)


def _hy_filter_kernel(z_ref, win_ref, bin_ref, wmid_ref, bmid_ref, freq_ref, wout_ref, delta_ref, o_ref):
    hp = lax.Precision.HIGHEST
    length = z_ref.shape[0]
    h = jnp.sin(freq_ref[0] * (jnp.dot(z_ref[...], win_ref[...], precision=hp, preferred_element_type=F32)
                               + bin_ref[...]))
    for i in range(HY_INNER):
        h = jnp.sin(freq_ref[i + 1] * (jnp.dot(h, wmid_ref[i], precision=hp, preferred_element_type=F32)
                                       + bmid_ref[i]))
    h = jnp.dot(h, wout_ref[...], precision=hp, preferred_element_type=F32)
    t = lax.broadcasted_iota(jnp.int32, (length, BR_W), 0).astype(F32) * (1.0 / (length - 1))
    decay = jnp.exp(-t * delta_ref[...])
    h_f = h[:, :BR_W] * decay
    h_b = h[:, BR_W:] * decay
    for i, a in enumerate((h_f + h_b, h_f - h_b)):
        hi = a.astype(BF16)
        lo = (a - hi.astype(F32)).astype(BF16)
        o_ref[:, (2 * i) * BR_W:(2 * i + 1) * BR_W] = hi
        o_ref[:, (2 * i + 1) * BR_W:(2 * i + 2) * BR_W] = lo


def _hy_spectrum_kernel(f_ref, a_ref, ks_ref, kd_ref, *, length):
    i = pl.program_id(0)
    tm = f_ref.shape[0]
    r = jnp.dot(f_ref[...], a_ref[...], preferred_element_type=F32)
    ks_ref[...] = r[:, :BR_W] + r[:, BR_W:2 * BR_W]
    kd = r[:, 2 * BR_W:3 * BR_W] + r[:, 3 * BR_W:]
    row = lax.broadcasted_iota(jnp.int32, kd.shape, 0) + i * tm
    kd_ref[...] = jnp.where(row == length, 0.0, kd)


def hyena_filter_spectrum(length, fwd, filt):
    w_in, b_in, w_mid, b_mid, freq, w_out = filt
    z, deltas = _hy_embedding(length)
    n = 2 * length
    parts = pl.pallas_call(
        _hy_filter_kernel,
        out_shape=jax.ShapeDtypeStruct((length, 4 * BR_W), BF16),
        compiler_params=pltpu.CompilerParams(vmem_limit_bytes=VMEM_LIMIT),
        name="hyena_filter",
    )(jnp.asarray(z), jnp.pad(w_in, ((0, HY_FFN - HY_EMB), (0, 0))), b_in.reshape(1, HY_FFN), w_mid,
      b_mid.reshape(HY_INNER, 1, HY_FFN), freq.reshape(HY_INNER + 1, 1, HY_FFN), w_out, jnp.asarray(deltas))
    tm = min(HY_ROWS, n)
    return pl.pallas_call(
        functools.partial(_hy_spectrum_kernel, length=length),
        grid=(n // tm,),
        in_specs=[pl.BlockSpec((tm, length), lambda i: (i, 0)),
                  pl.BlockSpec((length, 4 * BR_W), lambda i: (0, 0))],
        out_specs=[pl.BlockSpec((tm, BR_W), lambda i: (i, 0)), pl.BlockSpec((tm, BR_W), lambda i: (i, 0))],
        out_shape=[jax.ShapeDtypeStruct((n, BR_W), F32), jax.ShapeDtypeStruct((n, BR_W), F32)],
        compiler_params=pltpu.CompilerParams(vmem_limit_bytes=VMEM_LIMIT),
        name="hyena_spectrum",
    )(fwd, parts)


CONV_HALO = 8


def _dwconv_rows(x, w_ref, k, pad_ref):
    t = x.shape[0]
    assert k // 2 <= CONV_HALO and t % CONV_HALO == 0
    zeros = jnp.zeros((CONV_HALO, x.shape[1]), F32)
    pad_ref[0:CONV_HALO, :] = zeros
    pad_ref[CONV_HALO + t:2 * CONV_HALO + t, :] = zeros
    pad_ref[CONV_HALO:CONV_HALO + t, :] = x
    out = x * w_ref[k // 2:k // 2 + 1, :]
    for j in range(k):
        off = j - k // 2
        if off:
            out = out + pad_ref[CONV_HALO + off:CONV_HALO + off + t, :] * w_ref[j:j + 1, :]
    return out


def _hy_prep_kernel(x0_ref, x1_ref, v_ref, z_ref, w0_ref, w1_ref, wv_ref, b0_ref, b1_ref, bv_ref, vx_ref, x0g_ref,
                    pad_ref):
    x0 = _dwconv_rows(x0_ref[0], w0_ref, HY_CONV, pad_ref) + b0_ref[...]
    x1 = _dwconv_rows(x1_ref[0], w1_ref, HY_CONV, pad_ref) + b1_ref[...]
    v = _dwconv_rows(v_ref[0], wv_ref, HY_CONV, pad_ref) + bv_ref[...]
    z = z_ref[0]
    vx_ref[0] = v * x1
    x0g_ref[0] = x0 * (z * jax.nn.sigmoid(z))


def hyena_prep(p_main, conv_w, conv_b):
    b, length, _ = p_main.shape
    cb = LANE
    nj = BR_W // cb
    o_xv = MY_OFF["hy_xv"][0] // cb
    o_z = MY_OFF["hy_z"][0] // cb
    wt = conv_w.T
    bt = conv_b.reshape(1, 3 * BR_W)
    dspec = lambda o: pl.BlockSpec((1, length, cb), lambda i, j: (i, 0, o + j))
    wspec = lambda o: pl.BlockSpec((HY_CONV, cb), lambda i, j: (0, o + j))
    bspec = lambda o: pl.BlockSpec((1, cb), lambda i, j: (0, o + j))
    ospec = pl.BlockSpec((1, length, cb), lambda i, j: (i, 0, j))
    return pl.pallas_call(
        _hy_prep_kernel,
        grid=(b, nj),
        in_specs=[dspec(o_xv), dspec(o_xv + nj), dspec(o_xv + 2 * nj), dspec(o_z),
                  wspec(0), wspec(nj), wspec(2 * nj), bspec(0), bspec(nj), bspec(2 * nj)],
        out_specs=[ospec, ospec],
        out_shape=[jax.ShapeDtypeStruct((b, length, BR_W), F32)] * 2,
        scratch_shapes=[pltpu.VMEM((length + 2 * CONV_HALO, cb), F32)],
        compiler_params=pltpu.CompilerParams(vmem_limit_bytes=VMEM_LIMIT),
        name="hyena_prep",
    )(p_main, p_main, p_main, p_main, wt, wt, wt, bt, bt, bt)


def _hy_fwd_kernel(vx_ref, f_ref, kr_ref, ki_ref, knyq_ref, y_ref, *, length):
    v = vx_ref[0].astype(BF16)
    tm = min(HY_ROWS, length)
    for kt in range(length // tm):
        lo = kt * tm
        a = jnp.dot(f_ref[lo:lo + tm, :], v, preferred_element_type=F32)
        bh = jnp.dot(f_ref[length + lo:length + lo + tm, :], v, preferred_element_type=F32)
        kr = kr_ref[lo:lo + tm, :]
        ki = ki_ref[lo:lo + tm, :]
        if kt == 0:
            row = lax.broadcasted_iota(jnp.int32, kr.shape, 0)
            kb = jnp.where(row == 0, knyq_ref[0:1, :], kr)
        else:
            kb = kr
        y_ref[0, lo:lo + tm, :] = (a * kr - bh * ki).astype(BF16)
        y_ref[0, length + lo:length + lo + tm, :] = (a * ki + bh * kb).astype(BF16)


def _hy_inv_kernel(y_ref, inv_ref, vx_ref, x0g_ref, skip_ref, o_ref, *, length):
    tm = min(HY_ROWS, length)
    y = y_ref[0]
    for tt in range(length // tm):
        lo = tt * tm
        conv = jnp.dot(inv_ref[lo:lo + tm, :], y, preferred_element_type=F32)
        o_ref[0, lo:lo + tm, :] = ((conv + skip_ref[...] * vx_ref[0, lo:lo + tm, :])
                                   * x0g_ref[0, lo:lo + tm, :]).astype(o_ref.dtype)


def hyena_branch(p_main, conv_w, conv_b, skip, fwd, inv, ks, kd):
    b, length, _ = p_main.shape
    n = 2 * length
    vx, x0g = hyena_prep(p_main, conv_w, conv_b)
    cb = HY_COLS
    nj = BR_W // cb
    resident = lambda shape: pl.BlockSpec(shape, lambda j, i: (0, 0), pipeline_mode=pl.Buffered(1))
    y = pl.pallas_call(
        functools.partial(_hy_fwd_kernel, length=length),
        grid=(nj, b),
        in_specs=[pl.BlockSpec((1, length, cb), lambda j, i: (i, 0, j)),
                  resident((n, length)),
                  pl.BlockSpec((length, cb), lambda j, i: (0, j)),
                  pl.BlockSpec((length, cb), lambda j, i: (1, j)),
                  pl.BlockSpec((8, cb), lambda j, i: (length // 8, j))],
        out_specs=pl.BlockSpec((1, n, cb), lambda j, i: (i, 0, j)),
        out_shape=jax.ShapeDtypeStruct((b, n, BR_W), BF16),
        compiler_params=pltpu.CompilerParams(vmem_limit_bytes=VMEM_LIMIT),
        name="hyena_dft",
    )(vx, fwd, ks, kd, ks)
    return pl.pallas_call(
        functools.partial(_hy_inv_kernel, length=length),
        grid=(nj, b),
        in_specs=[pl.BlockSpec((1, n, cb), lambda j, i: (i, 0, j)),
                  resident((length, n)),
                  pl.BlockSpec((1, length, cb), lambda j, i: (i, 0, j)),
                  pl.BlockSpec((1, length, cb), lambda j, i: (i, 0, j)),
                  pl.BlockSpec((1, cb), lambda j, i: (0, j))],
        out_specs=pl.BlockSpec((1, length, cb), lambda j, i: (i, 0, j)),
        out_shape=jax.ShapeDtypeStruct((b, length, BR_W), BF16),
        compiler_params=pltpu.CompilerParams(vmem_limit_bytes=VMEM_LIMIT),
        name="hyena_idft",
    )(y, inv, vx, x0g, skip.reshape(1, BR_W))


def mixers(p_main, p_small, pc_main, pc_small, na_rpb, gla_wg2, gla_bg, gla_norm, gdn_conv, gdn_a_log, gdn_dt_bias,
           gdn_norm, hy_conv, hy_conv_b, hy_filt, hy_skip, with_ctx, dft_lat, dft_ctx):
    t = p_main.shape[1]
    ya = na_branch(p_main, pc_main, na_rpb)
    yb, yb_c = gla_branch(p_main, p_small, pc_main, pc_small, gla_wg2, gla_bg, gla_norm)
    yc, yc_c = gdn_branch(p_main, p_small, pc_main, pc_small, gdn_conv, gdn_a_log, gdn_dt_bias, gdn_norm)
    fwd, inv = dft_lat
    ks, kd = hyena_filter_spectrum(t, fwd, hy_filt)
    yd = hyena_branch(p_main, hy_conv, hy_conv_b, hy_skip, fwd, inv, ks, kd)
    if not with_ctx:
        return (ya, yb, yc, yd), None
    ya_c = ctx_attention_branch(pc_main)
    fwd_c, inv_c = dft_ctx
    ks_c, kd_c = hyena_filter_spectrum(pc_main.shape[1], fwd_c, hy_filt)
    yd_c = hyena_branch(pc_main, hy_conv, hy_conv_b, hy_skip, fwd_c, inv_c, ks_c, kd_c)
    return (ya, yb, yc, yd), (ya_c, yb_c, yc_c, yd_c)


def kernel(x, c, ctx, c_ctx, w_mod, b_mod, g_pre, g_post, w_in, w_gate, b_gate, w_branch, w_out, na_rpb, gla_wg2, gla_bg, gla_norm, gdn_conv, gdn_a_log, gdn_dt_bias, gdn_norm, hy_conv, hy_conv_b, hy_w_in, hy_b_in, hy_w_mid, hy_b_mid, hy_freq, hy_w_out, hy_skip):
    b, t, d = x.shape
    tc = ctx.shape[1]
    h = x.reshape(b * t, d)
    hc = ctx.reshape(b * tc, d)
    c_all = jnp.concatenate([c, c_ctx[None], jnp.zeros((7, d), F32)], axis=0)
    dft_lat = dft_matrices(t)
    dft_ctx = dft_matrices(tc)
    mods = []
    for l in range(DEPTH):
        mod = modulation(c_all, w_mod[l], b_mod[l])
        mods.append([mod[:, i * d:(i + 1) * d].reshape(-1, 1, d) for i in range(3)])
    u = uc = None
    for l in range(DEPTH):
        with_ctx = l < DEPTH - 1
        shift, scale, gate = mods[l]
        if u is None:
            u = prenorm_mod(h, g_pre[l], scale[:b], shift[:b], t)
            uc = prenorm_mod(hc, g_pre[l], scale[b:b + 1], shift[b:b + 1], b * tc)
        w_main, w_small = _permute_w_in(w_in[l])
        p_main = matmul(u, w_main, 1024, 1536, name="in_proj").reshape(b, t, N_MAIN)
        p_small = matmul(u, w_small, 1024, N_SMALL, name="in_proj_small").reshape(b, t, N_SMALL)
        pc_main = matmul(uc, w_main, 1024, 1536, name="in_proj_ctx").reshape(b, tc, N_MAIN)
        pc_small = matmul(uc, w_small, 1024, N_SMALL, name="in_proj_ctx_small").reshape(b, tc, N_SMALL)
        ys, ycs = mixers(p_main, p_small, pc_main, pc_small,
                         na_rpb[l], gla_wg2[l], gla_bg[l], gla_norm[l], gdn_conv[l], gdn_a_log[l], gdn_dt_bias[l],
                         gdn_norm[l], hy_conv[l], hy_conv_b[l],
                         (hy_w_in[l], hy_b_in[l], hy_w_mid[l], hy_b_mid[l], hy_freq[l], hy_w_out[l]),
                         hy_skip[l], with_ctx, dft_lat, dft_ctx)
        if not with_ctx:
            h = merge(u, ys, w_gate[l], b_gate[l], w_branch[l], w_out[l], h, gate[:b], g_post[l], t)
            break
        shift_n, scale_n, _ = mods[l + 1]
        h, u_next = merge(u, ys, w_gate[l], b_gate[l], w_branch[l], w_out[l], h, gate[:b], g_post[l], t,
                          next_norm=(g_pre[l + 1], scale_n[:b], shift_n[:b]))
        hc, uc = merge(uc, ycs, w_gate[l], b_gate[l], w_branch[l], w_out[l], hc, gate[b:b + 1], g_post[l], b * tc,
                       next_norm=(g_pre[l + 1], scale_n[b:b + 1], shift_n[b:b + 1]))
        u = u_next
    return h.reshape(b, t, d)
```
